```python
import jax, jax.numpy as jnp
from jax import lax
import numpy as np

D_MODEL = 1024
BATCH = 8
SEQ = 2048
DEPTH = 2

HEAD_DIM = 64
N_ATTN_HEADS = 8
N_SGU_GROUPS = 8
ATTN_WIDTH = N_ATTN_HEADS * HEAD_DIM
SGU_WIDTH = N_SGU_GROUPS * HEAD_DIM
MIX_WIDTH = ATTN_WIDTH + SGU_WIDTH
IN_COLS = 3 * ATTN_WIDTH + N_ATTN_HEADS + 2 * SGU_WIDTH
Q_BLOCK = 128
CHUNK = 128
D_FF_DENSE = 2816
N_EXPERTS = 8
TOP_K = 2
D_FF_EXPERT = 3584
N_MOD = 6
EPS = 1e-6
N_DENSE = (DEPTH + 1) // 2
N_MOE = DEPTH // 2

kernel_name = "hybrid_fox_sgu_moe_block"


def rmsnorm(x, g):
    xf = x.astype(jnp.float32)
    y = xf * lax.rsqrt(jnp.mean(xf * xf, axis=-1, keepdims=True) + EPS)
    return (y * g.astype(jnp.float32)).astype(x.dtype)


def layernorm_nobias(x, g):
    xf = x.astype(jnp.float32)
    xc = xf - jnp.mean(xf, axis=-1, keepdims=True)
    y = xc * lax.rsqrt(jnp.mean(xc * xc, axis=-1, keepdims=True) + EPS)
    return (y * g.astype(jnp.float32)).astype(x.dtype)


def forgetting_attention(q, k, v, log_f):
    B, H, S, dh = q.shape
    nb = S // Q_BLOCK
    scale = HEAD_DIM ** -0.5
    L = jnp.cumsum(log_f, axis=-1)
    qb = q.reshape(B, H, nb, Q_BLOCK, dh).transpose(2, 0, 1, 3, 4)
    Lb = L.reshape(B, H, nb, Q_BLOCK).transpose(2, 0, 1, 3)
    k_pos = jnp.arange(S)

    def one_block(args):
        q_blk, L_blk, i = args
        q_pos = i * Q_BLOCK + jnp.arange(Q_BLOCK)
        logits = jnp.einsum('bhqd,bhkd->bhqk', q_blk, k).astype(jnp.float32) * scale
        logits = logits + L_blk[..., :, None] - L[..., None, :]
        logits = jnp.where(k_pos[None, :] <= q_pos[:, None], logits, -jnp.inf)
        p = jax.nn.softmax(logits, axis=-1).astype(v.dtype)
        return jnp.einsum('bhqk,bhkd->bhqd', p, v)

    out = lax.map(one_block, (qb, Lb, jnp.arange(nb)))
    return out.transpose(1, 2, 0, 3, 4).reshape(B, H, S, dh)


def spatial_gating(u, v, w_spatial, b_spatial):
    B, S, _ = u.shape
    nc = S // CHUNK
    causal = jnp.tril(jnp.ones((CHUNK, CHUNK), dtype=w_spatial.dtype))
    w = w_spatial * causal[None]
    vc = v.reshape(B, nc, CHUNK, N_SGU_GROUPS, HEAD_DIM)
    mixed = jnp.einsum('gts,bcsgd->bctgd', w, vc) + b_spatial.T[:, :, None]
    out = u.reshape(B, nc, CHUNK, N_SGU_GROUPS, HEAD_DIM) * mixed
    return out.reshape(B, S, SGU_WIDTH)


def token_mixer(h, w_in, b_forget, g_v, w_spatial, b_spatial, g_out_attn, g_out_sgu, w_out):
    B, S, _ = h.shape
    z = jnp.einsum('bsd,dc->bsc', h, w_in)
    o1 = ATTN_WIDTH
    o2 = 2 * ATTN_WIDTH
    o3 = 3 * ATTN_WIDTH
    o4 = o3 + N_ATTN_HEADS
    def heads(t):
        return t.reshape(B, S, N_ATTN_HEADS, HEAD_DIM).transpose(0, 2, 1, 3)
    q = heads(z[..., :o1])
    k = heads(z[..., o1:o2])
    v = heads(z[..., o2:o3])
    log_f = jax.nn.log_sigmoid(z[..., o3:o4].astype(jnp.float32)
                               + b_forget.astype(jnp.float32)).transpose(0, 2, 1)
    attn = forgetting_attention(q, k, v, log_f)
    attn = attn.transpose(0, 2, 1, 3).reshape(B, S, ATTN_WIDTH)
    uv = jax.nn.gelu(z[..., o4:])
    u = uv[..., :SGU_WIDTH]
    sv = layernorm_nobias(uv[..., SGU_WIDTH:], g_v)
    sgu = spatial_gating(u, sv, w_spatial, b_spatial)
    merged = jnp.concatenate([rmsnorm(attn, g_out_attn), rmsnorm(sgu, g_out_sgu)], axis=-1)
    return jnp.einsum('bsm,md->bsd', merged, w_out)


def swiglu(h, w_gate_up, w_down, d_ff):
    gu = jnp.einsum('...d,df->...f', h, w_gate_up)
    act = jax.nn.silu(gu[..., :d_ff]) * gu[..., d_ff:]
    return jnp.einsum('...f,fd->...d', act, w_down)


def moe_swiglu(h, w_router, w_gate_up_exp, w_down_exp):
    B, S, D = h.shape
    xt = h.reshape(B * S, D)
    logits = jnp.einsum('nd,de->ne', xt, w_router).astype(jnp.float32)
    top_vals, top_idx = lax.top_k(logits, TOP_K)
    top_w = jax.nn.softmax(top_vals, axis=-1)
    gates = jnp.sum(jax.nn.one_hot(top_idx, N_EXPERTS, dtype=jnp.float32) * top_w[..., None], axis=1)
    gates = gates.astype(h.dtype)
    out = jnp.zeros_like(xt)
    for e in range(N_EXPERTS):
        out = out + gates[:, e:e + 1] * swiglu(xt, w_gate_up_exp[e], w_down_exp[e], D_FF_EXPERT)
    return out.reshape(B, S, D)


def setup_inputs(seed: int = 0) -> dict:
    key = jax.random.key(seed)
    ks = jax.random.split(key, 24)
    f32 = jnp.float32
    D = D_MODEL
    def nrm(k, shape, scale):
        return jax.random.normal(k, shape, f32) * scale
    def gain(k, shape):
        return 1.0 + 0.02 * jax.random.normal(k, shape, f32)
    return {
        "x": jax.random.normal(ks[0], (BATCH, SEQ, D), f32),
        "c": jax.random.normal(ks[1], (BATCH, D), f32),
        "w_ada": nrm(ks[2], (DEPTH, D, N_MOD * D), 0.1 * D ** -0.5),
        "b_ada": nrm(ks[3], (DEPTH, N_MOD * D), 0.01),
        "g_pre_mix": gain(ks[4], (DEPTH, D)),
        "g_post_mix": gain(ks[5], (DEPTH, D)),
        "g_pre_ffn": gain(ks[6], (DEPTH, D)),
        "g_post_ffn": gain(ks[7], (DEPTH, D)),
        "w_in": nrm(ks[8], (DEPTH, D, IN_COLS), D ** -0.5),
        "b_forget": jax.random.uniform(ks[9], (DEPTH, N_ATTN_HEADS), f32, 1.0, 5.0),
        "g_v": gain(ks[10], (DEPTH, SGU_WIDTH)),
        "w_spatial": nrm(ks[11], (DEPTH, N_SGU_GROUPS, CHUNK, CHUNK), 0.5 * CHUNK ** -0.5),
        "b_spatial": 1.0 + nrm(ks[12], (DEPTH, N_SGU_GROUPS, CHUNK), 0.1),
        "g_out_attn": gain(ks[13], (DEPTH, ATTN_WIDTH)),
        "g_out_sgu": gain(ks[14], (DEPTH, SGU_WIDTH)),
        "w_out": nrm(ks[15], (DEPTH, MIX_WIDTH, D), MIX_WIDTH ** -0.5),
        "w_gate_up_dense": nrm(ks[16], (N_DENSE, D, 2 * D_FF_DENSE), D ** -0.5),
        "w_down_dense": nrm(ks[17], (N_DENSE, D_FF_DENSE, D), D_FF_DENSE ** -0.5),
        "w_router": nrm(ks[18], (N_MOE, D, N_EXPERTS), D ** -0.5),
        "w_gate_up_exp": nrm(ks[19], (N_MOE, N_EXPERTS, D, 2 * D_FF_EXPERT), D ** -0.5),
        "w_down_exp": nrm(ks[20], (N_MOE, N_EXPERTS, D_FF_EXPERT, D), D_FF_EXPERT ** -0.5),
    }


def reference(x, c, w_ada, b_ada, g_pre_mix, g_post_mix, g_pre_ffn, g_post_ffn,
              w_in, b_forget, g_v, w_spatial, b_spatial, g_out_attn, g_out_sgu, w_out,
              w_gate_up_dense, w_down_dense, w_router, w_gate_up_exp, w_down_exp):
    c_act = jax.nn.silu(c)
    for l in range(DEPTH):
        mod = jnp.einsum('bd,dm->bm', c_act, w_ada[l]) + b_ada[l]
        sh_m, sc_m, gt_m, sh_f, sc_f, gt_f = [m[:, None, :] for m in jnp.split(mod, N_MOD, axis=-1)]
        h = rmsnorm(x, g_pre_mix[l]) * (1.0 + sc_m) + sh_m
        y = token_mixer(h, w_in[l], b_forget[l], g_v[l], w_spatial[l], b_spatial[l],
                        g_out_attn[l], g_out_sgu[l], w_out[l])
        x = x + (1.0 + gt_m) * rmsnorm(y, g_post_mix[l])
        h = rmsnorm(x, g_pre_ffn[l]) * (1.0 + sc_f) + sh_f
        if l % 2 == 0:
            y = swiglu(h, w_gate_up_dense[l // 2], w_down_dense[l // 2], D_FF_DENSE)
        else:
            y = moe_swiglu(h, w_router[l // 2], w_gate_up_exp[l // 2], w_down_exp[l // 2])
        x = x + (1.0 + gt_f) * rmsnorm(y, g_post_ffn[l])
    return x
```

```python
import functools
import math

import jax
import jax.numpy as jnp
from jax import lax
from jax.experimental import pallas as pl
from jax.experimental.pallas import tpu as pltpu

F32 = jnp.float32
BF16 = jnp.bfloat16

D_MODEL = 1024
HEAD_DIM = 64
N_HEADS = 8
N_GROUPS = 8
ATTN_W = N_HEADS * HEAD_DIM
SGU_W = N_GROUPS * HEAD_DIM
CHUNK = 128
D_FF_DENSE = 2816
N_EXPERTS = 8
D_FF_EXPERT = 3584
N_MOD = 6
EPS = 1e-6

LANES = 128
TOK_TILE = 512
ATT_TQ = 256
FFN_TF = 1408
EXP_TM = 512
EXP_TF = 512
DISP_TD = 256
GELU_C = math.sqrt(2.0 / math.pi)


def _rms(x, g):
    return x * lax.rsqrt(jnp.mean(x * x, axis=-1, keepdims=True) + EPS) * g


def _dot(a, b):
    return jnp.dot(a, b, preferred_element_type=F32)


def _mod_kernel(c_ref, w_ref, b_ref, o_ref):
    c = c_ref[...]
    ca = (c * jax.nn.sigmoid(c)).astype(BF16)
    o_ref[0] = _dot(ca, w_ref[0].astype(BF16)) + b_ref[0]


def _modulation(c, w_ada, b_ada):
    depth, d, m = w_ada.shape
    bsz = c.shape[0]
    nblk = m // d
    return pl.pallas_call(
        _mod_kernel,
        grid=(depth, nblk),
        in_specs=[
            pl.BlockSpec((bsz, d), lambda l, j: (0, 0)),
            pl.BlockSpec((1, d, d), lambda l, j: (l, 0, j)),
            pl.BlockSpec((1, 1, d), lambda l, j: (l, 0, j)),
        ],
        out_specs=pl.BlockSpec((1, bsz, d), lambda l, j: (l, 0, j)),
        out_shape=jax.ShapeDtypeStruct((depth, bsz, m), F32),
        name="adaln_mod",
    )(c, w_ada, b_ada.reshape(depth, 1, m))


def _inproj_kernel(x_ref, mod_ref, gpre_ref, wqkv_ref, wf_ref, wuv_ref, gv_ref,
                   q_ref, k_ref, v_ref, zf_ref, u_ref, sv_ref):
    x = x_ref[...]
    sh = mod_ref[0, 0:1, :]
    sc = mod_ref[0, 1:2, :]
    h = (_rms(x, gpre_ref[...]) * (1.0 + sc) + sh).astype(BF16)
    qkv = _dot(h, wqkv_ref[...])
    q_ref[...] = (qkv[:, :ATTN_W] * (HEAD_DIM ** -0.5)).astype(BF16)
    k_ref[...] = qkv[:, ATTN_W:2 * ATTN_W].astype(BF16)
    v_ref[...] = qkv[:, 2 * ATTN_W:].astype(BF16)
    zf_ref[...] = _dot(h, wf_ref[...])
    uv = _dot(h, wuv_ref[...])
    uv = uv * (0.5 * (1.0 + jnp.tanh(GELU_C * (uv + 0.044715 * (uv * uv * uv)))))
    u_ref[...] = uv[:, :SGU_W].astype(BF16)
    s = uv[:, SGU_W:]
    sc_ = s - jnp.mean(s, axis=-1, keepdims=True)
    sv = sc_ * lax.rsqrt(jnp.mean(sc_ * sc_, axis=-1, keepdims=True) + EPS) * gv_ref[...]
    sv_ref[...] = sv.astype(BF16)


def _inproj(x2, mod_l, g_pre, w_qkv, w_f, w_uv, g_v, seq):
    n, d = x2.shape
    tm = TOK_TILE
    per_b = seq // tm
    row = lambda i: (i, 0)
    const = lambda i: (0, 0)
    outs = [jax.ShapeDtypeStruct((n, ATTN_W), BF16)] * 3 + [
        jax.ShapeDtypeStruct((n, LANES), F32),
        jax.ShapeDtypeStruct((n, SGU_W), BF16),
        jax.ShapeDtypeStruct((n, SGU_W), BF16)]
    return pl.pallas_call(
        _inproj_kernel,
        grid=(n // tm,),
        in_specs=[
            pl.BlockSpec((tm, d), row),
            pl.BlockSpec((1, N_MOD, d), lambda i: (i // per_b, 0, 0)),
            pl.BlockSpec((1, d), const),
            pl.BlockSpec(w_qkv.shape, const),
            pl.BlockSpec(w_f.shape, const),
            pl.BlockSpec(w_uv.shape, const),
            pl.BlockSpec((1, SGU_W), const),
        ],
        out_specs=[pl.BlockSpec((tm, ATTN_W), row)] * 3 + [
            pl.BlockSpec((tm, LANES), row),
            pl.BlockSpec((tm, SGU_W), row),
            pl.BlockSpec((tm, SGU_W), row)],
        out_shape=outs,
        name="inproj",
    )(x2, mod_l, g_pre, w_qkv, w_f, w_uv, g_v)


def _gates_kernel(z_ref, b_ref, o_ref):
    z = z_ref[0] + b_ref[...]
    ls = jnp.minimum(z, 0.0) - jnp.log1p(jnp.exp(-jnp.abs(z)))
    seq = ls.shape[-1]
    pos = lax.broadcasted_iota(jnp.int32, ls.shape, 1)
    shift = 1
    while shift < seq:
        ls = ls + jnp.where(pos >= shift, pltpu.roll(ls, shift, 1), 0.0)
        shift *= 2
    o_ref[0] = ls


def _forget_cumsum(zf_t, b_forget):
    bsz, h, seq = zf_t.shape
    return pl.pallas_call(
        _gates_kernel,
        grid=(bsz,),
        in_specs=[pl.BlockSpec((1, h, seq), lambda b: (b, 0, 0)),
                  pl.BlockSpec((h, 1), lambda b: (0, 0))],
        out_specs=pl.BlockSpec((1, h, seq), lambda b: (b, 0, 0)),
        out_shape=jax.ShapeDtypeStruct((bsz, h, seq), F32),
        name="forget_cumsum",
    )(zf_t, b_forget.reshape(h, 1))


def _attn_kernel(q_ref, k_ref, v_ref, lq_ref, lk_ref, o_ref):
    tq = q_ref.shape[0]
    tk = tq
    qi = pl.program_id(2)
    lane = lax.broadcasted_iota(jnp.int32, (tq, LANES), 1)
    left = lane < HEAD_DIM
    q2 = q_ref[...]
    zero = jnp.zeros_like(q2)
    rows = lax.broadcasted_iota(jnp.int32, (tq, tk), 0)
    cols = lax.broadcasted_iota(jnp.int32, (tq, tk), 1)
    outs = []
    for hh in range(2):
        qm = jnp.where(left, q2, zero) if hh == 0 else jnp.where(left, zero, q2)
        lq = lq_ref[0, :, hh:hh + 1]

        def step(j, carry, diag):
            m, l, acc = carry
            start = pl.multiple_of(j * tk, tk)
            kb = k_ref[pl.ds(start, tk), :]
            vb = v_ref[pl.ds(start, tk), :]
            s = lax.dot_general(qm, kb, (((1,), (1,)), ((), ())), preferred_element_type=F32)
            lk = lk_ref[0, 0, hh:hh + 1, pl.ds(start, tk)]
            s = s + (lq - lk)
            if diag:
                s = jnp.where(cols <= rows, s, -jnp.inf)
            m_new = jnp.maximum(m, jnp.max(s, axis=-1, keepdims=True))
            alpha = jnp.exp(m - m_new)
            p = jnp.exp(s - m_new)
            l = alpha * l + jnp.sum(p, axis=-1, keepdims=True)
            acc = alpha * acc + _dot(p.astype(BF16), vb)
            return m_new, l, acc

        init = (jnp.full((tq, 1), -jnp.inf, F32), jnp.zeros((tq, 1), F32),
                jnp.zeros((tq, LANES), F32))
        carry = lax.fori_loop(0, qi, functools.partial(step, diag=False), init)
        m, l, acc = step(qi, carry, True)
        outs.append(acc / l)
    o_ref[...] = jnp.where(left, outs[0], outs[1]).astype(o_ref.dtype)


def _attention(q, k, v, l_pairs_col, l_pairs_row, bsz, seq):
    n = q.shape[0]
    tq = ATT_TQ
    nq = seq // tq
    npair = N_HEADS // 2
    return pl.pallas_call(
        _attn_kernel,
        grid=(bsz, npair, nq),
        in_specs=[
            pl.BlockSpec((tq, LANES), lambda b, p, i: (b * nq + i, p)),
            pl.BlockSpec((seq, LANES), lambda b, p, i: (b, p)),
            pl.BlockSpec((seq, LANES), lambda b, p, i: (b, p)),
            pl.BlockSpec((1, tq, 2), lambda b, p, i: (p, b * nq + i, 0)),
            pl.BlockSpec((1, 1, 2, seq), lambda b, p, i: (b, p, 0, 0)),
        ],
        out_specs=pl.BlockSpec((tq, LANES), lambda b, p, i: (b * nq + i, p)),
        out_shape=jax.ShapeDtypeStruct((n, ATTN_W), BF16),
        name="fox_attention",
    )(q, k, v, l_pairs_col, l_pairs_row)


def _sgu_kernel(u_ref, sv_ref, w_ref, bt_ref, g_ref, o_ref):
    tm = u_ref.shape[0]
    npair = N_GROUPS // 2
    lane = lax.broadcasted_iota(jnp.int32, (CHUNK, LANES), 1)
    left = lane < HEAD_DIM
    r = lax.broadcasted_iota(jnp.int32, (CHUNK, CHUNK), 0)
    c = lax.broadcasted_iota(jnp.int32, (CHUNK, CHUNK), 1)
    causal = c <= r
    lhs, bias = [], []
    for j in range(npair):
        wa = jnp.where(causal, w_ref[2 * j], jnp.zeros((), BF16))
        wb = jnp.where(causal, w_ref[2 * j + 1], jnp.zeros((), BF16))
        lhs.append(jnp.concatenate([wa, wb], axis=1))
        bias.append(jnp.where(left, bt_ref[:, 2 * j:2 * j + 1], bt_ref[:, 2 * j + 1:2 * j + 2]))
    for ci in range(tm // CHUNK):
        rs = slice(ci * CHUNK, (ci + 1) * CHUNK)
        blks = []
        ssq = jnp.zeros((CHUNK, 1), F32)
        for j in range(npair):
            cs = slice(j * LANES, (j + 1) * LANES)
            svb = sv_ref[rs, cs]
            zero = jnp.zeros_like(svb)
            rhs = jnp.concatenate([jnp.where(left, svb, zero), jnp.where(left, zero, svb)], axis=0)
            mixed = _dot(lhs[j], rhs) + bias[j]
            ob = u_ref[rs, cs].astype(F32) * mixed
            ssq = ssq + jnp.sum(ob * ob, axis=-1, keepdims=True)
            blks.append(ob)
        inv = lax.rsqrt(ssq * (1.0 / SGU_W) + EPS)
        for j in range(npair):
            cs = slice(j * LANES, (j + 1) * LANES)
            o_ref[rs, cs] = (blks[j] * inv * g_ref[:, cs]).astype(o_ref.dtype)


def _sgu(u, sv, w_sp, b_sp_t, g_out):
    n = u.shape[0]
    tm = TOK_TILE
    row = lambda i: (i, 0)
    return pl.pallas_call(
        _sgu_kernel,
        grid=(n // tm,),
        in_specs=[
            pl.BlockSpec((tm, SGU_W), row),
            pl.BlockSpec((tm, SGU_W), row),
            pl.BlockSpec(w_sp.shape, lambda i: (0, 0, 0)),
            pl.BlockSpec(b_sp_t.shape, lambda i: (0, 0)),
            pl.BlockSpec((1, SGU_W), lambda i: (0, 0)),
        ],
        out_specs=pl.BlockSpec((tm, SGU_W), row),
        out_shape=jax.ShapeDtypeStruct((n, SGU_W), BF16),
        name="sgu",
    )(u, sv, w_sp, b_sp_t, g_out)


def _outproj_kernel(a_ref, s_ref, ga_ref, wa_ref, ws_ref, x_ref, mod_ref, gpost_ref, gffn_ref,
                    *rest, with_router):
    if with_router:
        wr_ref, xo_ref, h_ref, lg_ref = rest
    else:
        xo_ref, h_ref = rest
    a = _rms(a_ref[...].astype(F32), ga_ref[...]).astype(BF16)
    y = _dot(a, wa_ref[...]) + _dot(s_ref[...], ws_ref[...])
    gt_m = mod_ref[0, 2:3, :]
    sh_f = mod_ref[0, 3:4, :]
    sc_f = mod_ref[0, 4:5, :]
    xn = x_ref[...] + (1.0 + gt_m) * _rms(y, gpost_ref[...])
    xo_ref[...] = xn
    h = _rms(xn, gffn_ref[...]) * (1.0 + sc_f) + sh_f
    h_ref[...] = h.astype(BF16)
    if with_router:
        lg_ref[...] = jnp.dot(h, wr_ref[...], preferred_element_type=F32,
                              precision=lax.Precision.HIGHEST)


def _outproj(attn, sgun, g_attn, w_a, w_s, x2, mod_l, g_post, g_ffn, w_router, seq):
    n, d = x2.shape
    tm = TOK_TILE
    per_b = seq // tm
    row = lambda i: (i, 0)
    const = lambda i: (0, 0)
    with_router = w_router is not None
    in_specs = [
        pl.BlockSpec((tm, ATTN_W), row),
        pl.BlockSpec((tm, SGU_W), row),
        pl.BlockSpec((1, ATTN_W), const),
        pl.BlockSpec(w_a.shape, const),
        pl.BlockSpec(w_s.shape, const),
        pl.BlockSpec((tm, d), row),
        pl.BlockSpec((1, N_MOD, d), lambda i: (i // per_b, 0, 0)),
        pl.BlockSpec((1, d), const),
        pl.BlockSpec((1, d), const),
    ]
    args = [attn, sgun, g_attn, w_a, w_s, x2, mod_l, g_post, g_ffn]
    out_specs = [pl.BlockSpec((tm, d), row), pl.BlockSpec((tm, d), row)]
    out_shape = [jax.ShapeDtypeStruct((n, d), F32), jax.ShapeDtypeStruct((n, d), BF16)]
    if with_router:
        in_specs.append(pl.BlockSpec(w_router.shape, const))
        args.append(w_router)
        out_specs.append(pl.BlockSpec((tm, LANES), row))
        out_shape.append(jax.ShapeDtypeStruct((n, LANES), F32))
    return pl.pallas_call(
        functools.partial(_outproj_kernel, with_router=with_router),
        grid=(n // tm,),
        in_specs=in_specs,
        out_specs=out_specs,
        out_shape=out_shape,
        name="outproj_router" if with_router else "outproj",
    )(*args)


def _ffn_kernel(h_ref, wg_ref, wu_ref, wd_ref, x_ref, mod_ref, gpost_ref, o_ref, acc_ref):
    f = pl.program_id(1)
    h = h_ref[...]
    g = _dot(h, wg_ref[...])
    u = _dot(h, wu_ref[...])
    act = (g * jax.nn.sigmoid(g) * u).astype(BF16)
    part = _dot(act, wd_ref[...])

    @pl.when(f == 0)
    def _():
        acc_ref[...] = part

    @pl.when(f > 0)
    def _():
        acc_ref[...] += part

    @pl.when(f == pl.num_programs(1) - 1)
    def _():
        gt_f = mod_ref[0, 5:6, :]
        o_ref[...] = x_ref[...] + (1.0 + gt_f) * _rms(acc_ref[...], gpost_ref[...])


def _ffn_dense(h2, w_gu, w_down, x2, mod_l, g_post, seq):
    n, d = x2.shape
    tm = TOK_TILE
    tf = FFN_TF
    nf = D_FF_DENSE // tf
    per_b = seq // tm
    return pl.pallas_call(
        _ffn_kernel,
        grid=(n // tm, nf),
        in_specs=[
            pl.BlockSpec((tm, d), lambda i, f: (i, 0)),
            pl.BlockSpec((d, tf), lambda i, f: (0, f)),
            pl.BlockSpec((d, tf), lambda i, f: (0, nf + f)),
            pl.BlockSpec((tf, d), lambda i, f: (f, 0)),
            pl.BlockSpec((tm, d), lambda i, f: (i, 0)),
            pl.BlockSpec((1, N_MOD, d), lambda i, f: (i // per_b, 0, 0)),
            pl.BlockSpec((1, d), lambda i, f: (0, 0)),
        ],
        out_specs=pl.BlockSpec((tm, d), lambda i, f: (i, 0)),
        out_shape=jax.ShapeDtypeStruct((n, d), F32),
        scratch_shapes=[pltpu.VMEM((tm, d), F32)],
        name="ffn_dense",
    )(h2, w_gu, w_gu, w_down, x2, mod_l, g_post)


def _route_kernel(lg_ref, keyc_ref, gate_ref, keyt_ref, ends_ref, base_ref):
    s = pl.program_id(0)
    tm = lg_ref.shape[0]

    @pl.when(s == 0)
    def _():
        base_ref[...] = jnp.zeros_like(base_ref)

    lane = lax.broadcasted_iota(jnp.int32, (tm, LANES), 1)
    lg = jnp.where(lane < N_EXPERTS, lg_ref[...], -jnp.inf)
    m1 = jnp.max(lg, axis=-1, keepdims=True)
    i1 = jnp.min(jnp.where(lg == m1, lane, LANES), axis=-1, keepdims=True)
    sel1 = lane == i1
    lg2 = jnp.where(sel1, -jnp.inf, lg)
    m2 = jnp.max(lg2, axis=-1, keepdims=True)
    i2 = jnp.min(jnp.where(lg2 == m2, lane, LANES), axis=-1, keepdims=True)
    sel2 = lane == i2
    e2 = jnp.exp(m2 - m1)
    w1 = 1.0 / (1.0 + e2)
    w2 = e2 / (1.0 + e2)
    gate_ref[...] = jnp.where(sel1, w1, 0.0) + jnp.where(sel2, w2, 0.0)
    chosen = jnp.logical_or(sel1, sel2)
    mask = jnp.where(chosen, 1.0, 0.0)
    r = lax.broadcasted_iota(jnp.int32, (tm, tm), 0)
    c = lax.broadcasted_iota(jnp.int32, (tm, tm), 1)
    tri = jnp.where(c < r, 1.0, 0.0).astype(BF16)
    base = base_ref[0:1, :]
    rank = _dot(tri, mask.astype(BF16)) + base
    key = jnp.where(chosen, rank, -1.0)
    keyc_ref[...] = key
    keyt_ref[...] = key.T
    new_base = base + jnp.sum(mask, axis=0, keepdims=True)
    base_ref[...] = jnp.broadcast_to(new_base, base_ref.shape)
    ends_ref[0] = jnp.broadcast_to(new_base, ends_ref.shape[1:])


def _route(logits):
    n = logits.shape[0]
    tm = TOK_TILE
    ns = n // tm
    row = lambda s: (s, 0)
    return pl.pallas_call(
        _route_kernel,
        grid=(ns,),
        in_specs=[pl.BlockSpec((tm, LANES), row)],
        out_specs=[
            pl.BlockSpec((tm, LANES), row),
            pl.BlockSpec((tm, LANES), row),
            pl.BlockSpec((LANES, tm), lambda s: (0, s)),
            pl.BlockSpec((1, 8, LANES), lambda s: (s, 0, 0)),
        ],
        out_shape=[
            jax.ShapeDtypeStruct((n, LANES), F32),
            jax.ShapeDtypeStruct((n, LANES), F32),
            jax.ShapeDtypeStruct((LANES, n), F32),
            jax.ShapeDtypeStruct((ns, 8, LANES), F32),
        ],
        scratch_shapes=[pltpu.VMEM((8, LANES), F32)],
        name="moe_route",
    )(logits)


def _work_lists(ends, n_tokens):
    ns = n_tokens // TOK_TILE
    e_n = N_EXPERTS
    sub = EXP_TM // DISP_TD
    max_tiles = (2 * n_tokens) // EXP_TM + e_n
    w_max = max_tiles * sub + e_n * (ns - 1)
    i32 = jnp.int32

    ends_i = ends[:, 0, :e_n].astype(i32)
    base = jnp.concatenate([jnp.zeros((1, e_n), i32), ends_i], axis=0)
    lo = base[:-1].T
    hi = base[1:].T
    cnt = base[-1]
    ntile = (cnt + EXP_TM - 1) // EXP_TM
    tile_end = jnp.cumsum(ntile)
    tile_start = tile_end - ntile
    total_tiles = tile_end[-1]
    nd = ntile * sub
    dstart = tile_start * sub

    nonempty = hi > lo
    m_lo = lo // DISP_TD
    m_hi = (hi - 1) // DISP_TD
    s_idx = jnp.arange(ns, dtype=i32)[None, :]
    last_s = jnp.max(jnp.where(nonempty, s_idx, -1), axis=1, keepdims=True)
    m_hi_pad = jnp.where(s_idx == last_s, nd[:, None] - 1, m_hi)

    def build(n_items, mlo, e_of, s_of):
        flat = n_items.reshape(-1)
        end = jnp.cumsum(flat)
        start = end - flat
        total = end[-1]
        w = jnp.arange(w_max, dtype=i32)
        wc = jnp.minimum(w, total - 1)
        idx = jnp.sum((end[None, :] <= wc[:, None]).astype(i32), axis=1)
        e = e_of.reshape(-1)[idx]
        s = s_of.reshape(-1)[idx]
        m = mlo.reshape(-1)[idx] + (wc - start[idx])
        valid = (w < total).astype(i32)
        return e, s, m, valid, total

    e_grid = jnp.broadcast_to(jnp.arange(e_n, dtype=i32)[:, None], (e_n, ns))
    s_grid = jnp.broadcast_to(s_idx, (e_n, ns))

    n_disp = jnp.where(nonempty, m_hi_pad - m_lo + 1, 0)
    de, ds_, dm, dvalid, _ = build(n_disp, m_lo, e_grid, s_grid)
    ddst = dstart[de] + dm
    dfirst = jnp.concatenate([jnp.ones((1,), i32), (ddst[1:] != ddst[:-1]).astype(i32)])

    n_comb = jnp.where(nonempty, m_hi - m_lo + 1, 0).T
    ce, cs, cm, cvalid, ctotal = build(n_comb, m_lo.T, e_grid.T, s_grid.T)
    csrc = dstart[ce] + cm
    cfirst = jnp.concatenate([jnp.ones((1,), i32), (cs[1:] != cs[:-1]).astype(i32)])
    w = jnp.arange(w_max, dtype=i32)
    clast = jnp.concatenate([(cs[1:] != cs[:-1]).astype(i32), jnp.ones((1,), i32)])
    clast = jnp.where(w == ctotal - 1, 1, clast) * cvalid

    t = jnp.arange(max_tiles, dtype=i32)
    tc = jnp.minimum(t, total_tiles - 1)
    tile_e = jnp.sum((tile_end[None, :] <= tc[:, None]).astype(i32), axis=1)
    tile_valid = (t < total_tiles).astype(i32)
    return dict(
        disp=(ds_, ddst, de, dm, dfirst, dvalid),
        comb=(cs, csrc, ce, cm, cfirst, clast, cvalid),
        tiles=(tc, tile_e, tile_valid),
        max_tiles=max_tiles, w_max=w_max)


def _dispatch_kernel(src_ref, dst_ref, e_ref, m_ref, first_ref, valid_ref, h_ref, keyt_ref, o_ref):
    w = pl.program_id(0)

    @pl.when(jnp.logical_and(valid_ref[w] == 1, first_ref[w] == 1))
    def _():
        o_ref[...] = jnp.zeros_like(o_ref)

    @pl.when(valid_ref[w] == 1)
    def _():
        td, ts = o_ref.shape[0], h_ref.shape[0]
        key = keyt_ref[pl.ds(e_ref[w], 1), :]
        slot = (m_ref[w] * td + lax.broadcasted_iota(jnp.int32, (td, ts), 0)).astype(F32)
        onehot = jnp.where(key == slot, 1.0, 0.0).astype(BF16)
        o_ref[...] = (o_ref[...].astype(F32) + _dot(onehot, h_ref[...])).astype(o_ref.dtype)


def _dispatch(h2, key_t, lists, max_tiles, w_max):
    n, d = h2.shape
    rows = max_tiles * EXP_TM
    grid_spec = pltpu.PrefetchScalarGridSpec(
        num_scalar_prefetch=6,
        grid=(w_max,),
        in_specs=[
            pl.BlockSpec((TOK_TILE, d), lambda w, src, dst, e, m, fi, va: (src[w], 0)),
            pl.BlockSpec((8, TOK_TILE), lambda w, src, dst, e, m, fi, va: (0, src[w])),
        ],
        out_specs=pl.BlockSpec((DISP_TD, d), lambda w, src, dst, e, m, fi, va: (dst[w], 0)),
    )
    return pl.pallas_call(
        _dispatch_kernel,
        grid_spec=grid_spec,
        out_shape=jax.ShapeDtypeStruct((rows, d), BF16),
        name="moe_dispatch",
    )(*lists, h2, key_t)


def _expert_kernel(tidx_ref, te_ref, tv_ref, x_ref, wg_ref, wu_ref, wd_ref, y_ref, acc_ref):
    i = pl.program_id(0)
    f = pl.program_id(1)

    @pl.when(tv_ref[i] == 1)
    def _():
        x = x_ref[...]
        g = _dot(x, wg_ref[0])
        u = _dot(x, wu_ref[0])
        act = (g * jax.nn.sigmoid(g) * u).astype(BF16)
        part = _dot(act, wd_ref[0])

        @pl.when(f == 0)
        def _():
            acc_ref[...] = part

        @pl.when(f > 0)
        def _():
            acc_ref[...] += part

        @pl.when(f == pl.num_programs(1) - 1)
        def _():
            y_ref[...] = acc_ref[...].astype(y_ref.dtype)


def _experts(xs, w_gu, w_down, tiles, max_tiles):
    rows, d = xs.shape
    tm, tf = EXP_TM, EXP_TF
    nf = D_FF_EXPERT // tf

    def fsel(i, f, tv):
        return jnp.where(tv[i] == 1, f, nf - 1)

    grid_spec = pltpu.PrefetchScalarGridSpec(
        num_scalar_prefetch=3,
        grid=(max_tiles, nf),
        in_specs=[
            pl.BlockSpec((tm, d), lambda i, f, ti, te, tv: (ti[i], 0)),
            pl.BlockSpec((1, d, tf), lambda i, f, ti, te, tv: (te[i], 0, fsel(i, f, tv))),
            pl.BlockSpec((1, d, tf), lambda i, f, ti, te, tv: (te[i], 0, nf + fsel(i, f, tv))),
            pl.BlockSpec((1, tf, d), lambda i, f, ti, te, tv: (te[i], fsel(i, f, tv), 0)),
        ],
        out_specs=pl.BlockSpec((tm, d), lambda i, f, ti, te, tv: (ti[i], 0)),
        scratch_shapes=[pltpu.VMEM((tm, d), F32)],
    )
    return pl.pallas_call(
        _expert_kernel,
        grid_spec=grid_spec,
        out_shape=jax.ShapeDtypeStruct((rows, d), BF16),
        name="moe_experts",
    )(*tiles, xs, w_gu, w_gu, w_down)


def _combine_kernel(s_ref, src_ref, e_ref, m_ref, first_ref, last_ref, valid_ref,
                    y_ref, keyc_ref, gate_ref, x_ref, mod_ref, gpost_ref, o_ref, acc_ref):
    w = pl.program_id(0)

    @pl.when(jnp.logical_and(valid_ref[w] == 1, first_ref[w] == 1))
    def _():
        acc_ref[...] = jnp.zeros_like(acc_ref)

    @pl.when(valid_ref[w] == 1)
    def _():
        ts, td = acc_ref.shape[0], y_ref.shape[0]
        lane = lax.broadcasted_iota(jnp.int32, (ts, LANES), 1)
        pick = lane == e_ref[w]
        key = jnp.sum(jnp.where(pick, keyc_ref[...], 0.0), axis=-1, keepdims=True)
        gate = jnp.sum(jnp.where(pick, gate_ref[...], 0.0), axis=-1, keepdims=True)
        slot = (m_ref[w] * td + lax.broadcasted_iota(jnp.int32, (ts, td), 1)).astype(F32)
        onehot = jnp.where(key == slot, 1.0, 0.0).astype(BF16)
        acc_ref[...] += gate * _dot(onehot, y_ref[...])

    @pl.when(last_ref[w] == 1)
    def _():
        gt_f = mod_ref[0, 5:6, :]
        o_ref[...] = x_ref[...] + (1.0 + gt_f) * _rms(acc_ref[...], gpost_ref[...])


def _combine(y, key_c, gates, x2, mod_l, g_post, lists, w_max, seq):
    n, d = x2.shape
    per_b = seq // TOK_TILE
    tok = lambda w, s, *_: (s[w], 0)
    grid_spec = pltpu.PrefetchScalarGridSpec(
        num_scalar_prefetch=7,
        grid=(w_max,),
        in_specs=[
            pl.BlockSpec((DISP_TD, d), lambda w, s, src, *_: (src[w], 0)),
            pl.BlockSpec((TOK_TILE, LANES), tok),
            pl.BlockSpec((TOK_TILE, LANES), tok),
            pl.BlockSpec((TOK_TILE, d), tok),
            pl.BlockSpec((1, N_MOD, d), lambda w, s, *_: (s[w] // per_b, 0, 0)),
            pl.BlockSpec((1, d), lambda w, *_: (0, 0)),
        ],
        out_specs=pl.BlockSpec((TOK_TILE, d), tok),
        scratch_shapes=[pltpu.VMEM((TOK_TILE, d), F32)],
    )
    return pl.pallas_call(
        _combine_kernel,
        grid_spec=grid_spec,
        out_shape=jax.ShapeDtypeStruct((n, d), F32),
        name="moe_combine",
    )(*lists, y, key_c, gates, x2, mod_l, g_post)


def kernel(x, c, w_ada, b_ada, g_pre_mix, g_post_mix, g_pre_ffn, g_post_ffn, w_in, b_forget, g_v,
           w_spatial, b_spatial, g_out_attn, g_out_sgu, w_out, w_gate_up_dense, w_down_dense,
           w_router, w_gate_up_exp, w_down_exp):
    bsz, seq, d = x.shape
    depth = w_ada.shape[0]
    n = bsz * seq
    npair = N_HEADS // 2
    assert d == D_MODEL and seq % TOK_TILE == 0 and seq % ATT_TQ == 0

    mod = _modulation(c, w_ada, b_ada).reshape(depth, bsz, N_MOD, d)
    x2 = x.reshape(n, d)
    o3 = 3 * ATTN_W
    o4 = o3 + N_HEADS
    for l in range(depth):
        mod_l = mod[l]
        w_l = w_in[l]
        w_qkv = w_l[:, :o3].astype(BF16)
        w_f = jnp.pad(w_l[:, o3:o4], ((0, 0), (0, LANES - N_HEADS))).astype(BF16)
        w_uv = w_l[:, o4:].astype(BF16)
        q, k, v, zf, u, sv = _inproj(x2, mod_l, g_pre_mix[l].reshape(1, d), w_qkv, w_f, w_uv,
                                     g_v[l].reshape(1, SGU_W), seq)
        zf_t = zf[:, :N_HEADS].reshape(bsz, seq, N_HEADS).transpose(0, 2, 1)
        lcum = _forget_cumsum(zf_t, b_forget[l])
        l_row = lcum.reshape(bsz, npair, 2, seq)
        l_col = l_row.transpose(1, 0, 3, 2).reshape(npair, n, 2)
        attn = _attention(q, k, v, l_col, l_row, bsz, seq)
        sgun = _sgu(u, sv, w_spatial[l].astype(BF16), b_spatial[l].T,
                    g_out_sgu[l].reshape(1, SGU_W))
        w_o = w_out[l].astype(BF16)
        moe = (l % 2 == 1)
        w_r = None
        if moe:
            w_r = jnp.pad(w_router[l // 2], ((0, 0), (0, LANES - N_EXPERTS)))
        res = _outproj(attn, sgun, g_out_attn[l].reshape(1, ATTN_W), w_o[:ATTN_W], w_o[ATTN_W:],
                       x2, mod_l, g_post_mix[l].reshape(1, d), g_pre_ffn[l].reshape(1, d), w_r, seq)
        g_post = g_post_ffn[l].reshape(1, d)
        if not moe:
            x2, h2 = res
            x2 = _ffn_dense(h2, w_gate_up_dense[l // 2].astype(BF16),
                            w_down_dense[l // 2].astype(BF16), x2, mod_l, g_post, seq)
        else:
            x2, h2, logits = res
            key_c, gates, key_t, ends = _route(logits)
            wl = _work_lists(ends, n)
            xs = _dispatch(h2, key_t, wl["disp"], wl["max_tiles"], wl["w_max"])
            y = _experts(xs, w_gate_up_exp[l // 2].astype(BF16), w_down_exp[l // 2].astype(BF16),
                         wl["tiles"], wl["max_tiles"])
            x2 = _combine(y, key_c, gates, x2, mod_l, g_post, wl["comb"], wl["w_max"], seq)
    return x2.reshape(bsz, seq, d)
```

```python
import functools
import math

import jax
import jax.numpy as jnp
from jax import lax
from jax.experimental import pallas as pl
from jax.experimental.pallas import tpu as pltpu

F32 = jnp.float32
BF16 = jnp.bfloat16

D_MODEL = 1024
HEAD_DIM = 64
N_HEADS = 8
N_GROUPS = 8
ATTN_W = N_HEADS * HEAD_DIM
SGU_W = N_GROUPS * HEAD_DIM
CHUNK = 128
D_FF_DENSE = 2816
N_EXPERTS = 8
D_FF_EXPERT = 3584
N_MOD = 6
EPS = 1e-6

LANES = 128
TOK_TILE = 512
ATT_TQ = 256
FFN_TF = 1408
EXP_TM = 1024
EXP_SUB = 512
EXP_TF = 896
DISP_TD = 256
GELU_C = math.sqrt(2.0 / math.pi)


def _rms(x, g):
    return x * lax.rsqrt(jnp.mean(x * x, axis=-1, keepdims=True) + EPS) * g


def _dot(a, b):
    return jnp.dot(a, b, preferred_element_type=F32)


def _mod_kernel(c_ref, w_ref, b_ref, o_ref):
    c = c_ref[...]
    ca = (c * jax.nn.sigmoid(c)).astype(BF16)
    o_ref[0] = _dot(ca, w_ref[0].astype(BF16)) + b_ref[0]


def _modulation(c, w_ada, b_ada):
    depth, d, m = w_ada.shape
    bsz = c.shape[0]
    nblk = m // d
    return pl.pallas_call(
        _mod_kernel,
        grid=(depth, nblk),
        in_specs=[
            pl.BlockSpec((bsz, d), lambda l, j: (0, 0)),
            pl.BlockSpec((1, d, d), lambda l, j: (l, 0, j)),
            pl.BlockSpec((1, 1, d), lambda l, j: (l, 0, j)),
        ],
        out_specs=pl.BlockSpec((1, bsz, d), lambda l, j: (l, 0, j)),
        out_shape=jax.ShapeDtypeStruct((depth, bsz, m), F32),
        name="adaln_mod",
    )(c, w_ada, b_ada.reshape(depth, 1, m))


def _inproj_kernel(x_ref, mod_ref, gpre_ref, wqkv_ref, wf_ref, wuv_ref, gv_ref,
                   q_ref, k_ref, v_ref, zf_ref, u_ref, sv_ref):
    x = x_ref[...]
    sh = mod_ref[0, 0:1, :]
    sc = mod_ref[0, 1:2, :]
    h = (_rms(x, gpre_ref[...]) * (1.0 + sc) + sh).astype(BF16)
    qkv = _dot(h, wqkv_ref[...])
    q_ref[...] = (qkv[:, :ATTN_W] * (HEAD_DIM ** -0.5)).astype(BF16)
    k_ref[...] = qkv[:, ATTN_W:2 * ATTN_W].astype(BF16)
    v_ref[...] = qkv[:, 2 * ATTN_W:].astype(BF16)
    zf_ref[...] = _dot(h, wf_ref[...])
    uv = _dot(h, wuv_ref[...])
    uv = uv * (0.5 * (1.0 + jnp.tanh(GELU_C * (uv + 0.044715 * (uv * uv * uv)))))
    u_ref[...] = uv[:, :SGU_W].astype(BF16)
    s = uv[:, SGU_W:]
    sc_ = s - jnp.mean(s, axis=-1, keepdims=True)
    sv = sc_ * lax.rsqrt(jnp.mean(sc_ * sc_, axis=-1, keepdims=True) + EPS) * gv_ref[...]
    sv_ref[...] = sv.astype(BF16)


def _inproj(x2, mod_l, g_pre, w_qkv, w_f, w_uv, g_v, seq):
    n, d = x2.shape
    tm = TOK_TILE
    per_b = seq // tm
    row = lambda i: (i, 0)
    const = lambda i: (0, 0)
    outs = [jax.ShapeDtypeStruct((n, ATTN_W), BF16)] * 3 + [
        jax.ShapeDtypeStruct((n, LANES), F32),
        jax.ShapeDtypeStruct((n, SGU_W), BF16),
        jax.ShapeDtypeStruct((n, SGU_W), BF16)]
    return pl.pallas_call(
        _inproj_kernel,
        grid=(n // tm,),
        in_specs=[
            pl.BlockSpec((tm, d), row),
            pl.BlockSpec((1, N_MOD, d), lambda i: (i // per_b, 0, 0)),
            pl.BlockSpec((1, d), const),
            pl.BlockSpec(w_qkv.shape, const),
            pl.BlockSpec(w_f.shape, const),
            pl.BlockSpec(w_uv.shape, const),
            pl.BlockSpec((1, SGU_W), const),
        ],
        out_specs=[pl.BlockSpec((tm, ATTN_W), row)] * 3 + [
            pl.BlockSpec((tm, LANES), row),
            pl.BlockSpec((tm, SGU_W), row),
            pl.BlockSpec((tm, SGU_W), row)],
        out_shape=outs,
        name="inproj",
    )(x2, mod_l, g_pre, w_qkv, w_f, w_uv, g_v)


def _gates_kernel(z_ref, b_ref, o_ref):
    z = z_ref[0] + b_ref[...]
    ls = jnp.minimum(z, 0.0) - jnp.log1p(jnp.exp(-jnp.abs(z)))
    seq = ls.shape[-1]
    pos = lax.broadcasted_iota(jnp.int32, ls.shape, 1)
    shift = 1
    while shift < seq:
        ls = ls + jnp.where(pos >= shift, pltpu.roll(ls, shift, 1), 0.0)
        shift *= 2
    o_ref[0] = ls


def _forget_cumsum(zf_t, b_forget):
    bsz, h, seq = zf_t.shape
    return pl.pallas_call(
        _gates_kernel,
        grid=(bsz,),
        in_specs=[pl.BlockSpec((1, h, seq), lambda b: (b, 0, 0)),
                  pl.BlockSpec((h, 1), lambda b: (0, 0))],
        out_specs=pl.BlockSpec((1, h, seq), lambda b: (b, 0, 0)),
        out_shape=jax.ShapeDtypeStruct((bsz, h, seq), F32),
        name="forget_cumsum",
    )(zf_t, b_forget.reshape(h, 1))


def _attn_case(q_ref, k_ref, v_ref, lq_ref, lk_ref, o_ref, nblk):
    tq = q_ref.shape[0]
    lane = lax.broadcasted_iota(jnp.int32, (tq, LANES), 1)
    left = lane < HEAD_DIM
    q2 = q_ref[...]
    zero = jnp.zeros_like(q2)
    one = jnp.ones_like(q2)
    rows = lax.broadcasted_iota(jnp.int32, (tq, tq), 0)
    cols = lax.broadcasted_iota(jnp.int32, (tq, tq), 1)
    causal = cols <= rows
    outs = []
    for hh in range(2):
        qm = jnp.where(left, q2, zero) if hh == 0 else jnp.where(left, zero, q2)
        lq = lq_ref[0, :, hh:hh + 1]
        s_blocks = []
        m = None
        for j in range(nblk):
            ks = slice(j * tq, (j + 1) * tq)
            s = lax.dot_general(qm, k_ref[ks, :], (((1,), (1,)), ((), ())),
                                preferred_element_type=F32)
            s = s + (lq - lk_ref[0, 0, hh:hh + 1, ks])
            if j == nblk - 1:
                s = jnp.where(causal, s, -jnp.inf)
            s_blocks.append(s)
            mj = jnp.max(s, axis=-1, keepdims=True)
            m = mj if m is None else jnp.maximum(m, mj)
        acc = jnp.zeros((tq, LANES), F32)
        for j in range(nblk):
            ks = slice(j * tq, (j + 1) * tq)
            p = jnp.exp(s_blocks[j] - m).astype(BF16)
            vb = v_ref[ks, :]
            vaug = jnp.where(left, vb, one) if hh == 0 else jnp.where(left, one, vb)
            acc = acc + _dot(p, vaug)
        outs.append(acc / pltpu.roll(acc, HEAD_DIM, 1))
    o_ref[...] = jnp.where(left, outs[0], outs[1]).astype(o_ref.dtype)


def _attn_kernel(q_ref, k_ref, v_ref, lq_ref, lk_ref, o_ref):
    nq = k_ref.shape[0] // q_ref.shape[0]
    qi = pl.program_id(2)
    for case in range(nq):
        @pl.when(qi == case)
        def _(case=case):
            _attn_case(q_ref, k_ref, v_ref, lq_ref, lk_ref, o_ref, case + 1)


def _attention(q, k, v, l_pairs_col, l_pairs_row, bsz, seq):
    n = q.shape[0]
    tq = ATT_TQ
    nq = seq // tq
    npair = N_HEADS // 2
    return pl.pallas_call(
        _attn_kernel,
        grid=(bsz, npair, nq),
        in_specs=[
            pl.BlockSpec((tq, LANES), lambda b, p, i: (b * nq + i, p)),
            pl.BlockSpec((seq, LANES), lambda b, p, i: (b, p)),
            pl.BlockSpec((seq, LANES), lambda b, p, i: (b, p)),
            pl.BlockSpec((1, tq, 2), lambda b, p, i: (p, b * nq + i, 0)),
            pl.BlockSpec((1, 1, 2, seq), lambda b, p, i: (b, p, 0, 0)),
        ],
        out_specs=pl.BlockSpec((tq, LANES), lambda b, p, i: (b * nq + i, p)),
        out_shape=jax.ShapeDtypeStruct((n, ATTN_W), BF16),
        name="fox_attention",
    )(q, k, v, l_pairs_col, l_pairs_row)


def _sgu_kernel(u_ref, sv_ref, w_ref, bt_ref, g_ref, o_ref):
    tm = u_ref.shape[0]
    npair = N_GROUPS // 2
    lane = lax.broadcasted_iota(jnp.int32, (CHUNK, LANES), 1)
    left = lane < HEAD_DIM
    r = lax.broadcasted_iota(jnp.int32, (CHUNK, CHUNK), 0)
    c = lax.broadcasted_iota(jnp.int32, (CHUNK, CHUNK), 1)
    causal = c <= r
    lhs, bias = [], []
    for j in range(npair):
        wa = jnp.where(causal, w_ref[2 * j], jnp.zeros((), BF16))
        wb = jnp.where(causal, w_ref[2 * j + 1], jnp.zeros((), BF16))
        lhs.append(jnp.concatenate([wa, wb], axis=1))
        bias.append(jnp.where(left, bt_ref[:, 2 * j:2 * j + 1], bt_ref[:, 2 * j + 1:2 * j + 2]))
    for ci in range(tm // CHUNK):
        rs = slice(ci * CHUNK, (ci + 1) * CHUNK)
        blks = []
        ssq = jnp.zeros((CHUNK, 1), F32)
        for j in range(npair):
            cs = slice(j * LANES, (j + 1) * LANES)
            svb = sv_ref[rs, cs]
            zero = jnp.zeros_like(svb)
            rhs = jnp.concatenate([jnp.where(left, svb, zero), jnp.where(left, zero, svb)], axis=0)
            mixed = _dot(lhs[j], rhs) + bias[j]
            ob = u_ref[rs, cs].astype(F32) * mixed
            ssq = ssq + jnp.sum(ob * ob, axis=-1, keepdims=True)
            blks.append(ob)
        inv = lax.rsqrt(ssq * (1.0 / SGU_W) + EPS)
        for j in range(npair):
            cs = slice(j * LANES, (j + 1) * LANES)
            o_ref[rs, cs] = (blks[j] * inv * g_ref[:, cs]).astype(o_ref.dtype)


def _sgu(u, sv, w_sp, b_sp_t, g_out):
    n = u.shape[0]
    tm = TOK_TILE
    row = lambda i: (i, 0)
    return pl.pallas_call(
        _sgu_kernel,
        grid=(n // tm,),
        in_specs=[
            pl.BlockSpec((tm, SGU_W), row),
            pl.BlockSpec((tm, SGU_W), row),
            pl.BlockSpec(w_sp.shape, lambda i: (0, 0, 0)),
            pl.BlockSpec(b_sp_t.shape, lambda i: (0, 0)),
            pl.BlockSpec((1, SGU_W), lambda i: (0, 0)),
        ],
        out_specs=pl.BlockSpec((tm, SGU_W), row),
        out_shape=jax.ShapeDtypeStruct((n, SGU_W), BF16),
        name="sgu",
    )(u, sv, w_sp, b_sp_t, g_out)


def _outproj_kernel(a_ref, s_ref, ga_ref, wa_ref, ws_ref, x_ref, mod_ref, gpost_ref, gffn_ref,
                    *rest, with_router):
    if with_router:
        wr_ref, xo_ref, h_ref, lg_ref = rest
    else:
        xo_ref, h_ref = rest
    a = _rms(a_ref[...].astype(F32), ga_ref[...]).astype(BF16)
    y = _dot(a, wa_ref[...]) + _dot(s_ref[...], ws_ref[...])
    gt_m = mod_ref[0, 2:3, :]
    sh_f = mod_ref[0, 3:4, :]
    sc_f = mod_ref[0, 4:5, :]
    xn = x_ref[...] + (1.0 + gt_m) * _rms(y, gpost_ref[...])
    xo_ref[...] = xn
    h = _rms(xn, gffn_ref[...]) * (1.0 + sc_f) + sh_f
    h_ref[...] = h.astype(BF16)
    if with_router:
        wr = wr_ref[...]
        h_hi = h.astype(BF16)
        h_lo = (h - h_hi.astype(F32)).astype(BF16)
        w_hi = wr.astype(BF16)
        w_lo = (wr - w_hi.astype(F32)).astype(BF16)
        lg_ref[...] = _dot(h_hi, w_hi) + (_dot(h_hi, w_lo) + _dot(h_lo, w_hi))


def _outproj(attn, sgun, g_attn, w_a, w_s, x2, mod_l, g_post, g_ffn, w_router, seq):
    n, d = x2.shape
    tm = TOK_TILE
    per_b = seq // tm
    row = lambda i: (i, 0)
    const = lambda i: (0, 0)
    with_router = w_router is not None
    in_specs = [
        pl.BlockSpec((tm, ATTN_W), row),
        pl.BlockSpec((tm, SGU_W), row),
        pl.BlockSpec((1, ATTN_W), const),
        pl.BlockSpec(w_a.shape, const),
        pl.BlockSpec(w_s.shape, const),
        pl.BlockSpec((tm, d), row),
        pl.BlockSpec((1, N_MOD, d), lambda i: (i // per_b, 0, 0)),
        pl.BlockSpec((1, d), const),
        pl.BlockSpec((1, d), const),
    ]
    args = [attn, sgun, g_attn, w_a, w_s, x2, mod_l, g_post, g_ffn]
    out_specs = [pl.BlockSpec((tm, d), row), pl.BlockSpec((tm, d), row)]
    out_shape = [jax.ShapeDtypeStruct((n, d), F32), jax.ShapeDtypeStruct((n, d), BF16)]
    if with_router:
        in_specs.append(pl.BlockSpec(w_router.shape, const))
        args.append(w_router)
        out_specs.append(pl.BlockSpec((tm, LANES), row))
        out_shape.append(jax.ShapeDtypeStruct((n, LANES), F32))
    return pl.pallas_call(
        functools.partial(_outproj_kernel, with_router=with_router),
        grid=(n // tm,),
        in_specs=in_specs,
        out_specs=out_specs,
        out_shape=out_shape,
        name="outproj_router" if with_router else "outproj",
    )(*args)


def _ffn_kernel(h_ref, wg_ref, wu_ref, wd_ref, x_ref, mod_ref, gpost_ref, o_ref, acc_ref):
    f = pl.program_id(1)
    h = h_ref[...]
    g = _dot(h, wg_ref[...])
    u = _dot(h, wu_ref[...])
    act = (g * jax.nn.sigmoid(g) * u).astype(BF16)
    part = _dot(act, wd_ref[...])

    @pl.when(f == 0)
    def _():
        acc_ref[...] = part

    @pl.when(f > 0)
    def _():
        acc_ref[...] += part

    @pl.when(f == pl.num_programs(1) - 1)
    def _():
        gt_f = mod_ref[0, 5:6, :]
        o_ref[...] = x_ref[...] + (1.0 + gt_f) * _rms(acc_ref[...], gpost_ref[...])


def _ffn_dense(h2, w_gu, w_down, x2, mod_l, g_post, seq):
    n, d = x2.shape
    tm = TOK_TILE
    tf = FFN_TF
    nf = D_FF_DENSE // tf
    per_b = seq // tm
    return pl.pallas_call(
        _ffn_kernel,
        grid=(n // tm, nf),
        in_specs=[
            pl.BlockSpec((tm, d), lambda i, f: (i, 0)),
            pl.BlockSpec((d, tf), lambda i, f: (0, f)),
            pl.BlockSpec((d, tf), lambda i, f: (0, nf + f)),
            pl.BlockSpec((tf, d), lambda i, f: (f, 0)),
            pl.BlockSpec((tm, d), lambda i, f: (i, 0)),
            pl.BlockSpec((1, N_MOD, d), lambda i, f: (i // per_b, 0, 0)),
            pl.BlockSpec((1, d), lambda i, f: (0, 0)),
        ],
        out_specs=pl.BlockSpec((tm, d), lambda i, f: (i, 0)),
        out_shape=jax.ShapeDtypeStruct((n, d), F32),
        scratch_shapes=[pltpu.VMEM((tm, d), F32)],
        name="ffn_dense",
    )(h2, w_gu, w_gu, w_down, x2, mod_l, g_post)


def _route_kernel(lg_ref, keyc_ref, gate_ref, keyt_ref, ends_ref, base_ref):
    s = pl.program_id(0)
    tm = lg_ref.shape[0]

    @pl.when(s == 0)
    def _():
        base_ref[...] = jnp.zeros_like(base_ref)

    lane = lax.broadcasted_iota(jnp.int32, (tm, LANES), 1)
    lg = jnp.where(lane < N_EXPERTS, lg_ref[...], -jnp.inf)
    m1 = jnp.max(lg, axis=-1, keepdims=True)
    i1 = jnp.min(jnp.where(lg == m1, lane, LANES), axis=-1, keepdims=True)
    sel1 = lane == i1
    lg2 = jnp.where(sel1, -jnp.inf, lg)
    m2 = jnp.max(lg2, axis=-1, keepdims=True)
    i2 = jnp.min(jnp.where(lg2 == m2, lane, LANES), axis=-1, keepdims=True)
    sel2 = lane == i2
    e2 = jnp.exp(m2 - m1)
    w1 = 1.0 / (1.0 + e2)
    w2 = e2 / (1.0 + e2)
    gate_ref[...] = jnp.where(sel1, w1, 0.0) + jnp.where(sel2, w2, 0.0)
    chosen = jnp.logical_or(sel1, sel2)
    mask = jnp.where(chosen, 1.0, 0.0)
    r = lax.broadcasted_iota(jnp.int32, (tm, tm), 0)
    c = lax.broadcasted_iota(jnp.int32, (tm, tm), 1)
    tri = jnp.where(c < r, 1.0, 0.0).astype(BF16)
    base = base_ref[0:1, :]
    rank = _dot(tri, mask.astype(BF16)) + base
    key = jnp.where(chosen, rank, -1.0)
    keyc_ref[...] = key
    keyt_ref[...] = key.T
    new_base = base + jnp.sum(mask, axis=0, keepdims=True)
    base_ref[...] = jnp.broadcast_to(new_base, base_ref.shape)
    ends_ref[0] = jnp.broadcast_to(new_base, ends_ref.shape[1:])


def _route(logits):
    n = logits.shape[0]
    tm = TOK_TILE
    ns = n // tm
    row = lambda s: (s, 0)
    return pl.pallas_call(
        _route_kernel,
        grid=(ns,),
        in_specs=[pl.BlockSpec((tm, LANES), row)],
        out_specs=[
            pl.BlockSpec((tm, LANES), row),
            pl.BlockSpec((tm, LANES), row),
            pl.BlockSpec((LANES, tm), lambda s: (0, s)),
            pl.BlockSpec((1, 8, LANES), lambda s: (s, 0, 0)),
        ],
        out_shape=[
            jax.ShapeDtypeStruct((n, LANES), F32),
            jax.ShapeDtypeStruct((n, LANES), F32),
            jax.ShapeDtypeStruct((LANES, n), F32),
            jax.ShapeDtypeStruct((ns, 8, LANES), F32),
        ],
        scratch_shapes=[pltpu.VMEM((8, LANES), F32)],
        name="moe_route",
    )(logits)


def _work_lists(ends, n_tokens):
    ns = n_tokens // TOK_TILE
    e_n = N_EXPERTS
    sub = EXP_TM // DISP_TD
    max_tiles = (2 * n_tokens) // EXP_TM + e_n
    w_max = (2 * n_tokens) // DISP_TD + e_n * (EXP_SUB // DISP_TD) + e_n * (ns - 1)
    i32 = jnp.int32

    ends_i = ends[:, 0, :e_n].astype(i32)
    base = jnp.concatenate([jnp.zeros((1, e_n), i32), ends_i], axis=0)
    lo = base[:-1].T
    hi = base[1:].T
    cnt = base[-1]
    ntile = (cnt + EXP_TM - 1) // EXP_TM
    tile_end = jnp.cumsum(ntile)
    tile_start = tile_end - ntile
    total_tiles = tile_end[-1]
    npass = (cnt + EXP_SUB - 1) // EXP_SUB
    nd = npass * (EXP_SUB // DISP_TD)
    dstart = tile_start * sub

    nonempty = hi > lo
    m_lo = lo // DISP_TD
    m_hi = (hi - 1) // DISP_TD
    s_idx = jnp.arange(ns, dtype=i32)[None, :]
    last_s = jnp.max(jnp.where(nonempty, s_idx, -1), axis=1, keepdims=True)
    m_hi_pad = jnp.where(s_idx == last_s, nd[:, None] - 1, m_hi)

    def build(n_items, mlo, e_of, s_of):
        flat = n_items.reshape(-1)
        end = jnp.cumsum(flat)
        start = end - flat
        total = end[-1]
        w = jnp.arange(w_max, dtype=i32)
        wc = jnp.minimum(w, total - 1)
        idx = jnp.sum((end[None, :] <= wc[:, None]).astype(i32), axis=1)
        e = e_of.reshape(-1)[idx]
        s = s_of.reshape(-1)[idx]
        m = mlo.reshape(-1)[idx] + (wc - start[idx])
        valid = (w < total).astype(i32)
        return e, s, m, valid, total

    e_grid = jnp.broadcast_to(jnp.arange(e_n, dtype=i32)[:, None], (e_n, ns))
    s_grid = jnp.broadcast_to(s_idx, (e_n, ns))

    n_disp = jnp.where(nonempty, m_hi_pad - m_lo + 1, 0)
    de, ds_, dm, dvalid, _ = build(n_disp, m_lo, e_grid, s_grid)
    ddst = dstart[de] + dm
    dfirst = jnp.concatenate([jnp.ones((1,), i32), (ddst[1:] != ddst[:-1]).astype(i32)])

    n_comb = jnp.where(nonempty, m_hi - m_lo + 1, 0).T
    ce, cs, cm, cvalid, ctotal = build(n_comb, m_lo.T, e_grid.T, s_grid.T)
    csrc = dstart[ce] + cm
    cfirst = jnp.concatenate([jnp.ones((1,), i32), (cs[1:] != cs[:-1]).astype(i32)])
    w = jnp.arange(w_max, dtype=i32)
    clast = jnp.concatenate([(cs[1:] != cs[:-1]).astype(i32), jnp.ones((1,), i32)])
    clast = jnp.where(w == ctotal - 1, 1, clast) * cvalid

    t = jnp.arange(max_tiles, dtype=i32)
    tc = jnp.minimum(t, total_tiles - 1)
    tile_e = jnp.sum((tile_end[None, :] <= tc[:, None]).astype(i32), axis=1)
    tile_pass = jnp.clip(npass[tile_e] - (tc - tile_start[tile_e]) * (EXP_TM // EXP_SUB),
                         1, EXP_TM // EXP_SUB)
    tile_pass = jnp.where(t < total_tiles, tile_pass, 0)
    return dict(
        disp=(ds_, ddst, de, dm, dfirst, dvalid),
        comb=(cs, csrc, ce, cm, cfirst, clast, cvalid),
        tiles=(tc, tile_e, tile_pass),
        max_tiles=max_tiles, w_max=w_max)


def _dispatch_kernel(src_ref, dst_ref, e_ref, m_ref, first_ref, valid_ref, h_ref, keyt_ref, o_ref):
    w = pl.program_id(0)

    @pl.when(jnp.logical_and(valid_ref[w] == 1, first_ref[w] == 1))
    def _():
        o_ref[...] = jnp.zeros_like(o_ref)

    @pl.when(valid_ref[w] == 1)
    def _():
        td, ts = o_ref.shape[0], h_ref.shape[0]
        key = keyt_ref[pl.ds(e_ref[w], 1), :]
        slot = (m_ref[w] * td + lax.broadcasted_iota(jnp.int32, (td, ts), 0)).astype(F32)
        onehot = jnp.where(key == slot, 1.0, 0.0).astype(BF16)
        o_ref[...] = (o_ref[...].astype(F32) + _dot(onehot, h_ref[...])).astype(o_ref.dtype)


def _dispatch(h2, key_t, lists, max_tiles, w_max):
    n, d = h2.shape
    rows = max_tiles * EXP_TM
    grid_spec = pltpu.PrefetchScalarGridSpec(
        num_scalar_prefetch=6,
        grid=(w_max,),
        in_specs=[
            pl.BlockSpec((TOK_TILE, d), lambda w, src, dst, e, m, fi, va: (src[w], 0)),
            pl.BlockSpec((8, TOK_TILE), lambda w, src, dst, e, m, fi, va: (0, src[w])),
        ],
        out_specs=pl.BlockSpec((DISP_TD, d), lambda w, src, dst, e, m, fi, va: (dst[w], 0)),
    )
    return pl.pallas_call(
        _dispatch_kernel,
        grid_spec=grid_spec,
        out_shape=jax.ShapeDtypeStruct((rows, d), BF16),
        name="moe_dispatch",
    )(*lists, h2, key_t)


def _expert_kernel(tidx_ref, te_ref, tp_ref, x_ref, wg_ref, wu_ref, wd_ref, y_ref, acc_ref):
    i = pl.program_id(0)
    f = pl.program_id(1)
    nf = pl.num_programs(1)

    def one_pass(r, carry):
        rs = pl.ds(pl.multiple_of(r * EXP_SUB, EXP_SUB), EXP_SUB)
        x = x_ref[rs, :]
        g = _dot(x, wg_ref[0])
        u = _dot(x, wu_ref[0])
        act = (g * jax.nn.sigmoid(g) * u).astype(BF16)
        part = _dot(act, wd_ref[0])

        @pl.when(f == 0)
        def _():
            acc_ref[rs, :] = part

        @pl.when(f > 0)
        def _():
            acc_ref[rs, :] += part

        @pl.when(f == nf - 1)
        def _():
            y_ref[rs, :] = acc_ref[rs, :].astype(y_ref.dtype)

        return carry

    lax.fori_loop(0, tp_ref[i], one_pass, 0)


def _experts(xs, w_gu, w_down, tiles, max_tiles):
    rows, d = xs.shape
    tm, tf = EXP_TM, EXP_TF
    nf = D_FF_EXPERT // tf

    def fsel(i, f, tp):
        return jnp.where(tp[i] > 0, f, nf - 1)

    grid_spec = pltpu.PrefetchScalarGridSpec(
        num_scalar_prefetch=3,
        grid=(max_tiles, nf),
        in_specs=[
            pl.BlockSpec((tm, d), lambda i, f, ti, te, tv: (ti[i], 0)),
            pl.BlockSpec((1, d, tf), lambda i, f, ti, te, tv: (te[i], 0, fsel(i, f, tv))),
            pl.BlockSpec((1, d, tf), lambda i, f, ti, te, tv: (te[i], 0, nf + fsel(i, f, tv))),
            pl.BlockSpec((1, tf, d), lambda i, f, ti, te, tv: (te[i], fsel(i, f, tv), 0)),
        ],
        out_specs=pl.BlockSpec((tm, d), lambda i, f, ti, te, tv: (ti[i], 0)),
        scratch_shapes=[pltpu.VMEM((tm, d), F32)],
    )
    return pl.pallas_call(
        _expert_kernel,
        grid_spec=grid_spec,
        out_shape=jax.ShapeDtypeStruct((rows, d), BF16),
        name="moe_experts",
    )(*tiles, xs, w_gu, w_gu, w_down)


def _combine_kernel(s_ref, src_ref, e_ref, m_ref, first_ref, last_ref, valid_ref,
                    y_ref, keyc_ref, gate_ref, x_ref, mod_ref, gpost_ref, o_ref, acc_ref):
    w = pl.program_id(0)

    @pl.when(jnp.logical_and(valid_ref[w] == 1, first_ref[w] == 1))
    def _():
        acc_ref[...] = jnp.zeros_like(acc_ref)

    @pl.when(valid_ref[w] == 1)
    def _():
        ts, td = acc_ref.shape[0], y_ref.shape[0]
        lane = lax.broadcasted_iota(jnp.int32, (ts, LANES), 1)
        pick = lane == e_ref[w]
        key = jnp.sum(jnp.where(pick, keyc_ref[...], 0.0), axis=-1, keepdims=True)
        gate = jnp.sum(jnp.where(pick, gate_ref[...], 0.0), axis=-1, keepdims=True)
        slot = (m_ref[w] * td + lax.broadcasted_iota(jnp.int32, (ts, td), 1)).astype(F32)
        onehot = jnp.where(key == slot, 1.0, 0.0).astype(BF16)
        acc_ref[...] += gate * _dot(onehot, y_ref[...])

    @pl.when(last_ref[w] == 1)
    def _():
        gt_f = mod_ref[0, 5:6, :]
        o_ref[...] = x_ref[...] + (1.0 + gt_f) * _rms(acc_ref[...], gpost_ref[...])


def _combine(y, key_c, gates, x2, mod_l, g_post, lists, w_max, seq):
    n, d = x2.shape
    per_b = seq // TOK_TILE
    tok = lambda w, s, *_: (s[w], 0)
    grid_spec = pltpu.PrefetchScalarGridSpec(
        num_scalar_prefetch=7,
        grid=(w_max,),
        in_specs=[
            pl.BlockSpec((DISP_TD, d), lambda w, s, src, *_: (src[w], 0)),
            pl.BlockSpec((TOK_TILE, LANES), tok),
            pl.BlockSpec((TOK_TILE, LANES), tok),
            pl.BlockSpec((TOK_TILE, d), tok),
            pl.BlockSpec((1, N_MOD, d), lambda w, s, *_: (s[w] // per_b, 0, 0)),
            pl.BlockSpec((1, d), lambda w, *_: (0, 0)),
        ],
        out_specs=pl.BlockSpec((TOK_TILE, d), tok),
        scratch_shapes=[pltpu.VMEM((TOK_TILE, d), F32)],
    )
    return pl.pallas_call(
        _combine_kernel,
        grid_spec=grid_spec,
        out_shape=jax.ShapeDtypeStruct((n, d), F32),
        name="moe_combine",
    )(*lists, y, key_c, gates, x2, mod_l, g_post)


def kernel(x, c, w_ada, b_ada, g_pre_mix, g_post_mix, g_pre_ffn, g_post_ffn, w_in, b_forget, g_v,
           w_spatial, b_spatial, g_out_attn, g_out_sgu, w_out, w_gate_up_dense, w_down_dense,
           w_router, w_gate_up_exp, w_down_exp):
    bsz, seq, d = x.shape
    depth = w_ada.shape[0]
    n = bsz * seq
    npair = N_HEADS // 2
    assert d == D_MODEL and seq % TOK_TILE == 0 and seq % ATT_TQ == 0

    mod = _modulation(c, w_ada, b_ada).reshape(depth, bsz, N_MOD, d)
    x2 = x.reshape(n, d)
    o3 = 3 * ATTN_W
    o4 = o3 + N_HEADS
    for l in range(depth):
        mod_l = mod[l]
        w_l = w_in[l]
        w_qkv = w_l[:, :o3].astype(BF16)
        w_f = jnp.pad(w_l[:, o3:o4], ((0, 0), (0, LANES - N_HEADS))).astype(BF16)
        w_uv = w_l[:, o4:].astype(BF16)
        q, k, v, zf, u, sv = _inproj(x2, mod_l, g_pre_mix[l].reshape(1, d), w_qkv, w_f, w_uv,
                                     g_v[l].reshape(1, SGU_W), seq)
        zf_t = zf[:, :N_HEADS].reshape(bsz, seq, N_HEADS).transpose(0, 2, 1)
        lcum = _forget_cumsum(zf_t, b_forget[l])
        l_row = lcum.reshape(bsz, npair, 2, seq)
        l_col = l_row.transpose(1, 0, 3, 2).reshape(npair, n, 2)
        attn = _attention(q, k, v, l_col, l_row, bsz, seq)
        sgun = _sgu(u, sv, w_spatial[l].astype(BF16), b_spatial[l].T,
                    g_out_sgu[l].reshape(1, SGU_W))
        w_o = w_out[l].astype(BF16)
        moe = (l % 2 == 1)
        w_r = None
        if moe:
            w_r = jnp.pad(w_router[l // 2], ((0, 0), (0, LANES - N_EXPERTS)))
        res = _outproj(attn, sgun, g_out_attn[l].reshape(1, ATTN_W), w_o[:ATTN_W], w_o[ATTN_W:],
                       x2, mod_l, g_post_mix[l].reshape(1, d), g_pre_ffn[l].reshape(1, d), w_r, seq)
        g_post = g_post_ffn[l].reshape(1, d)
        if not moe:
            x2, h2 = res
            x2 = _ffn_dense(h2, w_gate_up_dense[l // 2].astype(BF16),
                            w_down_dense[l // 2].astype(BF16), x2, mod_l, g_post, seq)
        else:
            x2, h2, logits = res
            key_c, gates, key_t, ends = _route(logits)
            wl = _work_lists(ends, n)
            xs = _dispatch(h2, key_t, wl["disp"], wl["max_tiles"], wl["w_max"])
            y = _experts(xs, w_gate_up_exp[l // 2].astype(BF16), w_down_exp[l // 2].astype(BF16),
                         wl["tiles"], wl["max_tiles"])
            x2 = _combine(y, key_c, gates, x2, mod_l, g_post, wl["comb"], wl["w_max"], seq)
    return x2.reshape(bsz, seq, d)
```

```python
import functools
import math

import jax
import jax.numpy as jnp
from jax import lax
from jax.experimental import pallas as pl
from jax.experimental.pallas import tpu as pltpu

F32 = jnp.float32
BF16 = jnp.bfloat16

D_MODEL = 1024
HEAD_DIM = 64
N_HEADS = 8
N_GROUPS = 8
ATTN_W = N_HEADS * HEAD_DIM
SGU_W = N_GROUPS * HEAD_DIM
CHUNK = 128
D_FF_DENSE = 2816
N_EXPERTS = 8
D_FF_EXPERT = 3584
N_MOD = 6
EPS = 1e-6

LANES = 128
TOK_TILE = 512
ATT_TQ = 256
MXU_TILE = 256
FFN_CHUNKS = ((0, 6 * MXU_TILE), (6 * MXU_TILE, D_FF_DENSE))
EXP_TM = 1024
EXP_SUB = 512
EXP_TF = 7 * MXU_TILE
LOG2E = math.log2(math.e)
DISP_TD = 256
GELU_C = math.sqrt(2.0 / math.pi)


def _rms(x, g):
    return x * lax.rsqrt(jnp.mean(x * x, axis=-1, keepdims=True) + EPS) * g


def _dot(a, b):
    return jnp.dot(a, b, preferred_element_type=F32)


def _mod_kernel(c_ref, w_ref, b_ref, o_ref):
    c = c_ref[...]
    ca = (c * jax.nn.sigmoid(c)).astype(BF16)
    o_ref[0] = _dot(ca, w_ref[0].astype(BF16)) + b_ref[0]


def _modulation(c, w_ada, b_ada):
    depth, d, m = w_ada.shape
    bsz = c.shape[0]
    nblk = m // d
    return pl.pallas_call(
        _mod_kernel,
        grid=(depth, nblk),
        in_specs=[
            pl.BlockSpec((bsz, d), lambda l, j: (0, 0)),
            pl.BlockSpec((1, d, d), lambda l, j: (l, 0, j)),
            pl.BlockSpec((1, 1, d), lambda l, j: (l, 0, j)),
        ],
        out_specs=pl.BlockSpec((1, bsz, d), lambda l, j: (l, 0, j)),
        out_shape=jax.ShapeDtypeStruct((depth, bsz, m), F32),
        name="adaln_mod",
    )(c, w_ada, b_ada.reshape(depth, 1, m))


def _inproj_kernel(x_ref, mod_ref, gpre_ref, wqkv_ref, wf_ref, wuv_ref, gv_ref,
                   q_ref, k_ref, v_ref, zf_ref, u_ref, sv_ref):
    x = x_ref[...]
    sh = mod_ref[0, 0:1, :]
    sc = mod_ref[0, 1:2, :]
    h = (_rms(x, gpre_ref[...]) * (1.0 + sc) + sh).astype(BF16)
    qkv = _dot(h, wqkv_ref[...])
    q_ref[...] = (qkv[:, :ATTN_W] * (HEAD_DIM ** -0.5 * LOG2E)).astype(BF16)
    k_ref[...] = qkv[:, ATTN_W:2 * ATTN_W].astype(BF16)
    v_ref[...] = qkv[:, 2 * ATTN_W:].astype(BF16)
    zf_ref[...] = _dot(h, wf_ref[...])
    uv = _dot(h, wuv_ref[...])
    uv = uv * (0.5 * (1.0 + jnp.tanh(GELU_C * (uv + 0.044715 * (uv * uv * uv)))))
    u_ref[...] = uv[:, :SGU_W].astype(BF16)
    s = uv[:, SGU_W:]
    sc_ = s - jnp.mean(s, axis=-1, keepdims=True)
    sv = sc_ * lax.rsqrt(jnp.mean(sc_ * sc_, axis=-1, keepdims=True) + EPS) * gv_ref[...]
    sv_ref[...] = sv.astype(BF16)


def _inproj(x2, mod_l, g_pre, w_qkv, w_f, w_uv, g_v, seq):
    n, d = x2.shape
    tm = TOK_TILE
    per_b = seq // tm
    row = lambda i: (i, 0)
    const = lambda i: (0, 0)
    outs = [jax.ShapeDtypeStruct((n, ATTN_W), BF16)] * 3 + [
        jax.ShapeDtypeStruct((n, LANES), F32),
        jax.ShapeDtypeStruct((n, SGU_W), BF16),
        jax.ShapeDtypeStruct((n, SGU_W), BF16)]
    return pl.pallas_call(
        _inproj_kernel,
        grid=(n // tm,),
        in_specs=[
            pl.BlockSpec((tm, d), row),
            pl.BlockSpec((1, N_MOD, d), lambda i: (i // per_b, 0, 0)),
            pl.BlockSpec((1, d), const),
            pl.BlockSpec(w_qkv.shape, const),
            pl.BlockSpec(w_f.shape, const),
            pl.BlockSpec(w_uv.shape, const),
            pl.BlockSpec((1, SGU_W), const),
        ],
        out_specs=[pl.BlockSpec((tm, ATTN_W), row)] * 3 + [
            pl.BlockSpec((tm, LANES), row),
            pl.BlockSpec((tm, SGU_W), row),
            pl.BlockSpec((tm, SGU_W), row)],
        out_shape=outs,
        name="inproj",
    )(x2, mod_l, g_pre, w_qkv, w_f, w_uv, g_v)


def _gates_kernel(z_ref, b_ref, o_ref):
    z = z_ref[0] + b_ref[...]
    ls = jnp.minimum(z, 0.0) - jnp.log1p(jnp.exp(-jnp.abs(z)))
    seq = ls.shape[-1]
    pos = lax.broadcasted_iota(jnp.int32, ls.shape, 1)
    shift = 1
    while shift < seq:
        ls = ls + jnp.where(pos >= shift, pltpu.roll(ls, shift, 1), 0.0)
        shift *= 2
    o_ref[0] = ls * LOG2E


def _forget_cumsum(zf_t, b_forget):
    bsz, h, seq = zf_t.shape
    return pl.pallas_call(
        _gates_kernel,
        grid=(bsz,),
        in_specs=[pl.BlockSpec((1, h, seq), lambda b: (b, 0, 0)),
                  pl.BlockSpec((h, 1), lambda b: (0, 0))],
        out_specs=pl.BlockSpec((1, h, seq), lambda b: (b, 0, 0)),
        out_shape=jax.ShapeDtypeStruct((bsz, h, seq), F32),
        name="forget_cumsum",
    )(zf_t, b_forget.reshape(h, 1))


def _attn_kernel(q_ref, k_ref, v_ref, lq_ref, lk_ref, o_ref):
    tq = ATT_TQ
    nq = q_ref.shape[0] // tq
    lane = lax.broadcasted_iota(jnp.int32, (tq, LANES), 1)
    left = lane < HEAD_DIM
    rows = lax.broadcasted_iota(jnp.int32, (tq, tq), 0)
    cols = lax.broadcasted_iota(jnp.int32, (tq, tq), 1)
    causal = cols <= rows
    zero = jnp.zeros((tq, LANES), BF16)
    one = jnp.ones((tq, LANES), BF16)
    def scores(qi):
        qs = slice(qi * tq, (qi + 1) * tq)
        q2 = q_ref[qs, :]
        qq = jnp.concatenate([jnp.where(left, q2, zero), jnp.where(left, zero, q2)], axis=0)
        t_blocks = [[], []]
        mt = [None, None]
        for j in range(qi + 1):
            ks = slice(j * tq, (j + 1) * tq)
            t2 = lax.dot_general(qq, k_ref[ks, :], (((1,), (1,)), ((), ())),
                                 preferred_element_type=F32)
            for hh in range(2):
                t = t2[hh * tq:(hh + 1) * tq] - lk_ref[0, 0, hh:hh + 1, ks]
                if j == qi:
                    t = jnp.where(causal, t, -jnp.inf)
                t_blocks[hh].append(t)
                mj = jnp.maximum(t[:, :LANES], t[:, LANES:])
                mt[hh] = mj if mt[hh] is None else jnp.maximum(mt[hh], mj)
        shift = []
        for hh in range(2):
            lq = lq_ref[0, qs, hh:hh + 1]
            row_max = jnp.max(mt[hh], axis=-1, keepdims=True) + lq
            shift.append(row_max - lq)
        return t_blocks, shift

    def values(qi, t_blocks, shift):
        qs = slice(qi * tq, (qi + 1) * tq)
        accs = [None, None]
        for j in range(qi + 1):
            ks = slice(j * tq, (j + 1) * tq)
            pp = jnp.concatenate([jnp.exp2(t_blocks[hh][j] - shift[hh]) for hh in range(2)],
                                 axis=0).astype(BF16)
            vaug = jnp.concatenate([v_ref[ks, :], one], axis=1)
            part = _dot(pp, vaug)
            accs[j % 2] = part if accs[j % 2] is None else accs[j % 2] + part
        acc = accs[0] if accs[1] is None else accs[0] + accs[1]
        out0 = acc[:tq, :LANES] / acc[:tq, LANES:]
        out1 = acc[tq:, :LANES] / acc[tq:, LANES:]
        o_ref[qs, :] = jnp.where(left, out0, out1).astype(o_ref.dtype)

    pending = scores(0)
    for qi in range(nq):
        nxt = scores(qi + 1) if qi + 1 < nq else None
        values(qi, *pending)
        pending = nxt


def _attention(q, k, v, l_pairs_col, l_pairs_row, bsz, seq):
    n = q.shape[0]
    npair = N_HEADS // 2
    blk = pl.BlockSpec((seq, LANES), lambda b, p: (b, p))
    return pl.pallas_call(
        _attn_kernel,
        grid=(bsz, npair),
        in_specs=[
            blk, blk, blk,
            pl.BlockSpec((1, seq, 2), lambda b, p: (p, b, 0)),
            pl.BlockSpec((1, 1, 2, seq), lambda b, p: (b, p, 0, 0)),
        ],
        out_specs=blk,
        out_shape=jax.ShapeDtypeStruct((n, ATTN_W), BF16),
        name="fox_attention",
    )(q, k, v, l_pairs_col, l_pairs_row)


def _sgu_kernel(u_ref, sv_ref, w_ref, bt_ref, g_ref, o_ref):
    tm = u_ref.shape[0]
    npair = N_GROUPS // 2
    lane = lax.broadcasted_iota(jnp.int32, (CHUNK, LANES), 1)
    left = lane < HEAD_DIM
    r = lax.broadcasted_iota(jnp.int32, (CHUNK, CHUNK), 0)
    c = lax.broadcasted_iota(jnp.int32, (CHUNK, CHUNK), 1)
    causal = c <= r
    lhs, bias = [], []
    for j in range(npair):
        wa = jnp.where(causal, w_ref[2 * j], jnp.zeros((), BF16))
        wb = jnp.where(causal, w_ref[2 * j + 1], jnp.zeros((), BF16))
        lhs.append(jnp.concatenate([wa, wb], axis=1))
        bias.append(jnp.where(left, bt_ref[:, 2 * j:2 * j + 1], bt_ref[:, 2 * j + 1:2 * j + 2]))
    for ci in range(tm // CHUNK):
        rs = slice(ci * CHUNK, (ci + 1) * CHUNK)
        blks = []
        ssq = jnp.zeros((CHUNK, 1), F32)
        for j in range(npair):
            cs = slice(j * LANES, (j + 1) * LANES)
            svb = sv_ref[rs, cs]
            zero = jnp.zeros_like(svb)
            rhs = jnp.concatenate([jnp.where(left, svb, zero), jnp.where(left, zero, svb)], axis=0)
            mixed = _dot(lhs[j], rhs) + bias[j]
            ob = u_ref[rs, cs].astype(F32) * mixed
            ssq = ssq + jnp.sum(ob * ob, axis=-1, keepdims=True)
            blks.append(ob)
        inv = lax.rsqrt(ssq * (1.0 / SGU_W) + EPS)
        for j in range(npair):
            cs = slice(j * LANES, (j + 1) * LANES)
            o_ref[rs, cs] = (blks[j] * inv * g_ref[:, cs]).astype(o_ref.dtype)


def _sgu(u, sv, w_sp, b_sp_t, g_out):
    n = u.shape[0]
    tm = TOK_TILE
    row = lambda i: (i, 0)
    return pl.pallas_call(
        _sgu_kernel,
        grid=(n // tm,),
        in_specs=[
            pl.BlockSpec((tm, SGU_W), row),
            pl.BlockSpec((tm, SGU_W), row),
            pl.BlockSpec(w_sp.shape, lambda i: (0, 0, 0)),
            pl.BlockSpec(b_sp_t.shape, lambda i: (0, 0)),
            pl.BlockSpec((1, SGU_W), lambda i: (0, 0)),
        ],
        out_specs=pl.BlockSpec((tm, SGU_W), row),
        out_shape=jax.ShapeDtypeStruct((n, SGU_W), BF16),
        name="sgu",
    )(u, sv, w_sp, b_sp_t, g_out)


def _outproj_kernel(a_ref, s_ref, ga_ref, wa_ref, ws_ref, x_ref, mod_ref, gpost_ref, gffn_ref,
                    *rest, with_router):
    if with_router:
        wr_ref, xo_ref, h_ref, lg_ref = rest
    else:
        xo_ref, h_ref = rest
    a = _rms(a_ref[...].astype(F32), ga_ref[...]).astype(BF16)
    y = _dot(a, wa_ref[...]) + _dot(s_ref[...], ws_ref[...])
    gt_m = mod_ref[0, 2:3, :]
    sh_f = mod_ref[0, 3:4, :]
    sc_f = mod_ref[0, 4:5, :]
    xn = x_ref[...] + (1.0 + gt_m) * _rms(y, gpost_ref[...])
    xo_ref[...] = xn
    h = _rms(xn, gffn_ref[...]) * (1.0 + sc_f) + sh_f
    h_ref[...] = h.astype(BF16)
    if with_router:
        wr = wr_ref[...]
        h_hi = h.astype(BF16)
        h_lo = (h - h_hi.astype(F32)).astype(BF16)
        w_hi = wr.astype(BF16)
        w_lo = (wr - w_hi.astype(F32)).astype(BF16)
        lg_ref[...] = _dot(h_hi, w_hi) + (_dot(h_hi, w_lo) + _dot(h_lo, w_hi))


def _outproj(attn, sgun, g_attn, w_a, w_s, x2, mod_l, g_post, g_ffn, w_router, seq):
    n, d = x2.shape
    tm = TOK_TILE
    per_b = seq // tm
    row = lambda i: (i, 0)
    const = lambda i: (0, 0)
    with_router = w_router is not None
    in_specs = [
        pl.BlockSpec((tm, ATTN_W), row),
        pl.BlockSpec((tm, SGU_W), row),
        pl.BlockSpec((1, ATTN_W), const),
        pl.BlockSpec(w_a.shape, const),
        pl.BlockSpec(w_s.shape, const),
        pl.BlockSpec((tm, d), row),
        pl.BlockSpec((1, N_MOD, d), lambda i: (i // per_b, 0, 0)),
        pl.BlockSpec((1, d), const),
        pl.BlockSpec((1, d), const),
    ]
    args = [attn, sgun, g_attn, w_a, w_s, x2, mod_l, g_post, g_ffn]
    out_specs = [pl.BlockSpec((tm, d), row), pl.BlockSpec((tm, d), row)]
    out_shape = [jax.ShapeDtypeStruct((n, d), F32), jax.ShapeDtypeStruct((n, d), BF16)]
    if with_router:
        in_specs.append(pl.BlockSpec(w_router.shape, const))
        args.append(w_router)
        out_specs.append(pl.BlockSpec((tm, LANES), row))
        out_shape.append(jax.ShapeDtypeStruct((n, LANES), F32))
    return pl.pallas_call(
        functools.partial(_outproj_kernel, with_router=with_router),
        grid=(n // tm,),
        in_specs=in_specs,
        out_specs=out_specs,
        out_shape=out_shape,
        name="outproj_router" if with_router else "outproj",
    )(*args)


def _ffn_kernel(h_ref, wgu_ref, wd_ref, x_ref, mod_ref, gpost_ref, o_ref):
    h = h_ref[...]
    y = None
    for a, b in FFN_CHUNKS:
        g = _dot(h, wgu_ref[:, a:b])
        u = _dot(h, wgu_ref[:, D_FF_DENSE + a:D_FF_DENSE + b])
        act = (g * jax.nn.sigmoid(g) * u).astype(BF16)
        part = _dot(act, wd_ref[a:b, :])
        y = part if y is None else y + part
    gt_f = mod_ref[0, 5:6, :]
    o_ref[...] = x_ref[...] + (1.0 + gt_f) * _rms(y, gpost_ref[...])


def _ffn_dense(h2, w_gu, w_down, x2, mod_l, g_post, seq):
    n, d = x2.shape
    tm = TOK_TILE
    per_b = seq // tm
    resident = dict(pipeline_mode=pl.Buffered(1))
    return pl.pallas_call(
        _ffn_kernel,
        grid=(n // tm,),
        in_specs=[
            pl.BlockSpec((tm, d), lambda i: (i, 0)),
            pl.BlockSpec(w_gu.shape, lambda i: (0, 0), **resident),
            pl.BlockSpec(w_down.shape, lambda i: (0, 0), **resident),
            pl.BlockSpec((tm, d), lambda i: (i, 0)),
            pl.BlockSpec((1, N_MOD, d), lambda i: (i // per_b, 0, 0)),
            pl.BlockSpec((1, d), lambda i: (0, 0)),
        ],
        out_specs=pl.BlockSpec((tm, d), lambda i: (i, 0)),
        out_shape=jax.ShapeDtypeStruct((n, d), F32),
        name="ffn_dense",
    )(h2, w_gu, w_down, x2, mod_l, g_post)


def _route_kernel(lg_ref, keyc_ref, gate_ref, keyt_ref, ends_ref, base_ref):
    s = pl.program_id(0)
    tm = lg_ref.shape[0]

    @pl.when(s == 0)
    def _():
        base_ref[...] = jnp.zeros_like(base_ref)

    lane = lax.broadcasted_iota(jnp.int32, (tm, LANES), 1)
    lg = jnp.where(lane < N_EXPERTS, lg_ref[...], -jnp.inf)
    m1 = jnp.max(lg, axis=-1, keepdims=True)
    i1 = jnp.min(jnp.where(lg == m1, lane, LANES), axis=-1, keepdims=True)
    sel1 = lane == i1
    lg2 = jnp.where(sel1, -jnp.inf, lg)
    m2 = jnp.max(lg2, axis=-1, keepdims=True)
    i2 = jnp.min(jnp.where(lg2 == m2, lane, LANES), axis=-1, keepdims=True)
    sel2 = lane == i2
    e2 = jnp.exp(m2 - m1)
    w1 = 1.0 / (1.0 + e2)
    w2 = e2 / (1.0 + e2)
    gate_ref[...] = jnp.where(sel1, w1, 0.0) + jnp.where(sel2, w2, 0.0)
    chosen = jnp.logical_or(sel1, sel2)
    mask = jnp.where(chosen, 1.0, 0.0)
    r = lax.broadcasted_iota(jnp.int32, (tm, tm), 0)
    c = lax.broadcasted_iota(jnp.int32, (tm, tm), 1)
    tri = jnp.where(c < r, 1.0, 0.0).astype(BF16)
    base = base_ref[0:1, :]
    rank = _dot(tri, mask.astype(BF16)) + base
    key = jnp.where(chosen, rank, -1.0)
    keyc_ref[...] = key
    keyt_ref[...] = key.T
    new_base = base + jnp.sum(mask, axis=0, keepdims=True)
    base_ref[...] = jnp.broadcast_to(new_base, base_ref.shape)
    ends_ref[0] = jnp.broadcast_to(new_base, ends_ref.shape[1:])


def _route(logits):
    n = logits.shape[0]
    tm = TOK_TILE
    ns = n // tm
    row = lambda s: (s, 0)
    return pl.pallas_call(
        _route_kernel,
        grid=(ns,),
        in_specs=[pl.BlockSpec((tm, LANES), row)],
        out_specs=[
            pl.BlockSpec((tm, LANES), row),
            pl.BlockSpec((tm, LANES), row),
            pl.BlockSpec((LANES, tm), lambda s: (0, s)),
            pl.BlockSpec((1, 8, LANES), lambda s: (s, 0, 0)),
        ],
        out_shape=[
            jax.ShapeDtypeStruct((n, LANES), F32),
            jax.ShapeDtypeStruct((n, LANES), F32),
            jax.ShapeDtypeStruct((LANES, n), F32),
            jax.ShapeDtypeStruct((ns, 8, LANES), F32),
        ],
        scratch_shapes=[pltpu.VMEM((8, LANES), F32)],
        name="moe_route",
    )(logits)


def _work_lists(ends, n_tokens):
    ns = n_tokens // TOK_TILE
    e_n = N_EXPERTS
    sub = EXP_TM // DISP_TD
    max_tiles = (2 * n_tokens) // EXP_TM + e_n
    w_max = (2 * n_tokens) // DISP_TD + e_n * (EXP_SUB // DISP_TD) + e_n * (ns - 1)
    i32 = jnp.int32

    ends_i = ends[:, 0, :e_n].astype(i32)
    base = jnp.concatenate([jnp.zeros((1, e_n), i32), ends_i], axis=0)
    lo = base[:-1].T
    hi = base[1:].T
    cnt = base[-1]
    ntile = (cnt + EXP_TM - 1) // EXP_TM
    tile_end = jnp.cumsum(ntile)
    tile_start = tile_end - ntile
    total_tiles = tile_end[-1]
    npass = (cnt + EXP_SUB - 1) // EXP_SUB
    nd = npass * (EXP_SUB // DISP_TD)
    dstart = tile_start * sub

    nonempty = hi > lo
    m_lo = lo // DISP_TD
    m_hi = (hi - 1) // DISP_TD
    s_idx = jnp.arange(ns, dtype=i32)[None, :]
    last_s = jnp.max(jnp.where(nonempty, s_idx, -1), axis=1, keepdims=True)
    m_hi_pad = jnp.where(s_idx == last_s, nd[:, None] - 1, m_hi)

    def build(n_items, mlo, e_of, s_of):
        flat = n_items.reshape(-1)
        end = jnp.cumsum(flat)
        start = end - flat
        total = end[-1]
        w = jnp.arange(w_max, dtype=i32)
        wc = jnp.minimum(w, total - 1)
        idx = jnp.sum((end[None, :] <= wc[:, None]).astype(i32), axis=1)
        e = e_of.reshape(-1)[idx]
        s = s_of.reshape(-1)[idx]
        m = mlo.reshape(-1)[idx] + (wc - start[idx])
        valid = (w < total).astype(i32)
        return e, s, m, valid, total

    e_grid = jnp.broadcast_to(jnp.arange(e_n, dtype=i32)[:, None], (e_n, ns))
    s_grid = jnp.broadcast_to(s_idx, (e_n, ns))

    n_disp = jnp.where(nonempty, m_hi_pad - m_lo + 1, 0)
    de, ds_, dm, dvalid, _ = build(n_disp, m_lo, e_grid, s_grid)
    ddst = dstart[de] + dm
    dfirst = jnp.concatenate([jnp.ones((1,), i32), (ddst[1:] != ddst[:-1]).astype(i32)])

    n_comb = jnp.where(nonempty, m_hi - m_lo + 1, 0).T
    ce, cs, cm, cvalid, ctotal = build(n_comb, m_lo.T, e_grid.T, s_grid.T)
    csrc = dstart[ce] + cm
    cfirst = jnp.concatenate([jnp.ones((1,), i32), (cs[1:] != cs[:-1]).astype(i32)])
    w = jnp.arange(w_max, dtype=i32)
    clast = jnp.concatenate([(cs[1:] != cs[:-1]).astype(i32), jnp.ones((1,), i32)])
    clast = jnp.where(w == ctotal - 1, 1, clast) * cvalid

    t = jnp.arange(max_tiles, dtype=i32)
    tc = jnp.minimum(t, total_tiles - 1)
    tile_e = jnp.sum((tile_end[None, :] <= tc[:, None]).astype(i32), axis=1)
    tile_pass = jnp.clip(npass[tile_e] - (tc - tile_start[tile_e]) * (EXP_TM // EXP_SUB),
                         1, EXP_TM // EXP_SUB)
    tile_pass = jnp.where(t < total_tiles, tile_pass, 0)
    return dict(
        disp=(ds_, ddst, de, dm, dfirst, dvalid),
        comb=(cs, csrc, ce, cm, cfirst, clast, cvalid),
        tiles=(tc, tile_e, tile_pass),
        max_tiles=max_tiles, w_max=w_max)


def _dispatch_kernel(src_ref, dst_ref, e_ref, m_ref, first_ref, valid_ref, h_ref, keyt_ref, o_ref):
    w = pl.program_id(0)

    @pl.when(jnp.logical_and(valid_ref[w] == 1, first_ref[w] == 1))
    def _():
        o_ref[...] = jnp.zeros_like(o_ref)

    @pl.when(valid_ref[w] == 1)
    def _():
        td, ts = o_ref.shape[0], h_ref.shape[0]
        key = keyt_ref[pl.ds(e_ref[w], 1), :]
        slot = (m_ref[w] * td + lax.broadcasted_iota(jnp.int32, (td, ts), 0)).astype(F32)
        onehot = jnp.where(key == slot, 1.0, 0.0).astype(BF16)
        o_ref[...] = (o_ref[...].astype(F32) + _dot(onehot, h_ref[...])).astype(o_ref.dtype)


def _dispatch(h2, key_t, lists, max_tiles, w_max):
    n, d = h2.shape
    rows = max_tiles * EXP_TM
    grid_spec = pltpu.PrefetchScalarGridSpec(
        num_scalar_prefetch=6,
        grid=(w_max,),
        in_specs=[
            pl.BlockSpec((TOK_TILE, d), lambda w, src, dst, e, m, fi, va: (src[w], 0)),
            pl.BlockSpec((8, TOK_TILE), lambda w, src, dst, e, m, fi, va: (0, src[w])),
        ],
        out_specs=pl.BlockSpec((DISP_TD, d), lambda w, src, dst, e, m, fi, va: (dst[w], 0)),
    )
    return pl.pallas_call(
        _dispatch_kernel,
        grid_spec=grid_spec,
        out_shape=jax.ShapeDtypeStruct((rows, d), BF16),
        name="moe_dispatch",
    )(*lists, h2, key_t)


def _expert_kernel(tidx_ref, te_ref, tp_ref, x_ref, wg_ref, wu_ref, wd_ref, y_ref, acc_ref):
    i = pl.program_id(0)
    f = pl.program_id(1)
    nf = pl.num_programs(1)

    def one_pass(r, carry):
        rs = pl.ds(pl.multiple_of(r * EXP_SUB, EXP_SUB), EXP_SUB)
        x = x_ref[rs, :]
        g = _dot(x, wg_ref[0])
        u = _dot(x, wu_ref[0])
        act = (g * jax.nn.sigmoid(g) * u).astype(BF16)
        part = _dot(act, wd_ref[0])

        @pl.when(f == 0)
        def _():
            acc_ref[rs, :] = part

        @pl.when(f > 0)
        def _():
            acc_ref[rs, :] += part

        @pl.when(f == nf - 1)
        def _():
            y_ref[rs, :] = acc_ref[rs, :].astype(y_ref.dtype)

        return carry

    lax.fori_loop(0, tp_ref[i], one_pass, 0)


def _experts(xs, w_gu, w_down, tiles, max_tiles):
    rows, d = xs.shape
    tm, tf = EXP_TM, EXP_TF
    nf = D_FF_EXPERT // tf

    def fsel(i, f, tp):
        return jnp.where(tp[i] > 0, f, nf - 1)

    grid_spec = pltpu.PrefetchScalarGridSpec(
        num_scalar_prefetch=3,
        grid=(max_tiles, nf),
        in_specs=[
            pl.BlockSpec((tm, d), lambda i, f, ti, te, tv: (ti[i], 0)),
            pl.BlockSpec((1, d, tf), lambda i, f, ti, te, tv: (te[i], 0, fsel(i, f, tv))),
            pl.BlockSpec((1, d, tf), lambda i, f, ti, te, tv: (te[i], 0, nf + fsel(i, f, tv))),
            pl.BlockSpec((1, tf, d), lambda i, f, ti, te, tv: (te[i], fsel(i, f, tv), 0)),
        ],
        out_specs=pl.BlockSpec((tm, d), lambda i, f, ti, te, tv: (ti[i], 0)),
        scratch_shapes=[pltpu.VMEM((tm, d), F32)],
    )
    return pl.pallas_call(
        _expert_kernel,
        grid_spec=grid_spec,
        out_shape=jax.ShapeDtypeStruct((rows, d), BF16),
        name="moe_experts",
    )(*tiles, xs, w_gu, w_gu, w_down)


def _combine_kernel(s_ref, src_ref, e_ref, m_ref, first_ref, last_ref, valid_ref,
                    y_ref, keyc_ref, gate_ref, x_ref, mod_ref, gpost_ref, o_ref, acc_ref):
    w = pl.program_id(0)

    @pl.when(jnp.logical_and(valid_ref[w] == 1, first_ref[w] == 1))
    def _():
        acc_ref[...] = jnp.zeros_like(acc_ref)

    @pl.when(valid_ref[w] == 1)
    def _():
        ts, td = acc_ref.shape[0], y_ref.shape[0]
        lane = lax.broadcasted_iota(jnp.int32, (ts, LANES), 1)
        pick = lane == e_ref[w]
        key = jnp.sum(jnp.where(pick, keyc_ref[...], 0.0), axis=-1, keepdims=True)
        gate = jnp.sum(jnp.where(pick, gate_ref[...], 0.0), axis=-1, keepdims=True)
        slot = (m_ref[w] * td + lax.broadcasted_iota(jnp.int32, (ts, td), 1)).astype(F32)
        onehot = jnp.where(key == slot, 1.0, 0.0).astype(BF16)
        acc_ref[...] += gate * _dot(onehot, y_ref[...])

    @pl.when(last_ref[w] == 1)
    def _():
        gt_f = mod_ref[0, 5:6, :]
        o_ref[...] = x_ref[...] + (1.0 + gt_f) * _rms(acc_ref[...], gpost_ref[...])


def _combine(y, key_c, gates, x2, mod_l, g_post, lists, w_max, seq):
    n, d = x2.shape
    per_b = seq // TOK_TILE
    tok = lambda w, s, *_: (s[w], 0)
    grid_spec = pltpu.PrefetchScalarGridSpec(
        num_scalar_prefetch=7,
        grid=(w_max,),
        in_specs=[
            pl.BlockSpec((DISP_TD, d), lambda w, s, src, *_: (src[w], 0)),
            pl.BlockSpec((TOK_TILE, LANES), tok),
            pl.BlockSpec((TOK_TILE, LANES), tok),
            pl.BlockSpec((TOK_TILE, d), tok),
            pl.BlockSpec((1, N_MOD, d), lambda w, s, *_: (s[w] // per_b, 0, 0)),
            pl.BlockSpec((1, d), lambda w, *_: (0, 0)),
        ],
        out_specs=pl.BlockSpec((TOK_TILE, d), tok),
        scratch_shapes=[pltpu.VMEM((TOK_TILE, d), F32)],
    )
    return pl.pallas_call(
        _combine_kernel,
        grid_spec=grid_spec,
        out_shape=jax.ShapeDtypeStruct((n, d), F32),
        name="moe_combine",
    )(*lists, y, key_c, gates, x2, mod_l, g_post)


def kernel(x, c, w_ada, b_ada, g_pre_mix, g_post_mix, g_pre_ffn, g_post_ffn, w_in, b_forget, g_v,
           w_spatial, b_spatial, g_out_attn, g_out_sgu, w_out, w_gate_up_dense, w_down_dense,
           w_router, w_gate_up_exp, w_down_exp):
    bsz, seq, d = x.shape
    depth = w_ada.shape[0]
    n = bsz * seq
    npair = N_HEADS // 2
    assert d == D_MODEL and seq % TOK_TILE == 0 and seq % ATT_TQ == 0

    mod = _modulation(c, w_ada, b_ada).reshape(depth, bsz, N_MOD, d)
    x2 = x.reshape(n, d)
    o3 = 3 * ATTN_W
    o4 = o3 + N_HEADS
    for l in range(depth):
        mod_l = mod[l]
        w_l = w_in[l]
        w_qkv = w_l[:, :o3].astype(BF16)
        w_f = jnp.pad(w_l[:, o3:o4], ((0, 0), (0, LANES - N_HEADS))).astype(BF16)
        w_uv = w_l[:, o4:].astype(BF16)
        q, k, v, zf, u, sv = _inproj(x2, mod_l, g_pre_mix[l].reshape(1, d), w_qkv, w_f, w_uv,
                                     g_v[l].reshape(1, SGU_W), seq)
        zf_t = zf[:, :N_HEADS].reshape(bsz, seq, N_HEADS).transpose(0, 2, 1)
        lcum = _forget_cumsum(zf_t, b_forget[l])
        l_row = lcum.reshape(bsz, npair, 2, seq)
        l_col = l_row.transpose(1, 0, 3, 2).reshape(npair, n, 2)
        attn = _attention(q, k, v, l_col, l_row, bsz, seq)
        sgun = _sgu(u, sv, w_spatial[l].astype(BF16), b_spatial[l].T,
                    g_out_sgu[l].reshape(1, SGU_W))
        w_o = w_out[l].astype(BF16)
        moe = (l % 2 == 1)
        w_r = None
        if moe:
            w_r = jnp.pad(w_router[l // 2], ((0, 0), (0, LANES - N_EXPERTS)))
        res = _outproj(attn, sgun, g_out_attn[l].reshape(1, ATTN_W), w_o[:ATTN_W], w_o[ATTN_W:],
                       x2, mod_l, g_post_mix[l].reshape(1, d), g_pre_ffn[l].reshape(1, d), w_r, seq)
        g_post = g_post_ffn[l].reshape(1, d)
        if not moe:
            x2, h2 = res
            x2 = _ffn_dense(h2, w_gate_up_dense[l // 2].astype(BF16),
                            w_down_dense[l // 2].astype(BF16), x2, mod_l, g_post, seq)
        else:
            x2, h2, logits = res
            key_c, gates, key_t, ends = _route(logits)
            wl = _work_lists(ends, n)
            xs = _dispatch(h2, key_t, wl["disp"], wl["max_tiles"], wl["w_max"])
            y = _experts(xs, w_gate_up_exp[l // 2].astype(BF16), w_down_exp[l // 2].astype(BF16),
                         wl["tiles"], wl["max_tiles"])
            x2 = _combine(y, key_c, gates, x2, mod_l, g_post, wl["comb"], wl["w_max"], seq)
    return x2.reshape(bsz, seq, d)
```

```python
import functools
import math

import jax
import jax.numpy as jnp
from jax import lax
from jax.experimental import pallas as pl
from jax.experimental.pallas import tpu as pltpu

F32 = jnp.float32
BF16 = jnp.bfloat16

D_MODEL = 1024
HEAD_DIM = 64
N_HEADS = 8
N_GROUPS = 8
ATTN_W = N_HEADS * HEAD_DIM
SGU_W = N_GROUPS * HEAD_DIM
CHUNK = 128
D_FF_DENSE = 2816
N_EXPERTS = 8
D_FF_EXPERT = 3584
N_MOD = 6
EPS = 1e-6

LANES = 128
TOK_TILE = 512
ATT_TQ = 256
MXU_TILE = 256
FFN_CHUNKS = ((0, 6 * MXU_TILE), (6 * MXU_TILE, D_FF_DENSE))
EXP_TM = 1024
EXP_SUB = 512
EXP_TF = 7 * MXU_TILE
LOG2E = math.log2(math.e)
DISP_TD = 256
COMB_ALIGN = 16
COMB_WIN = TOK_TILE + COMB_ALIGN
COMB_CHUNKS = ((0, MXU_TILE), (MXU_TILE, COMB_WIN))
GELU_C = math.sqrt(2.0 / math.pi)


def _rms(x, g):
    return x * lax.rsqrt(jnp.mean(x * x, axis=-1, keepdims=True) + EPS) * g


def _dot(a, b):
    return jnp.dot(a, b, preferred_element_type=F32)


def _mod_kernel(c_ref, w_ref, b_ref, o_ref):
    c = c_ref[...]
    ca = (c * jax.nn.sigmoid(c)).astype(BF16)
    o_ref[0] = _dot(ca, w_ref[0].astype(BF16)) + b_ref[0]


def _modulation(c, w_ada, b_ada):
    depth, d, m = w_ada.shape
    bsz = c.shape[0]
    nblk = m // d
    return pl.pallas_call(
        _mod_kernel,
        grid=(depth, nblk),
        in_specs=[
            pl.BlockSpec((bsz, d), lambda l, j: (0, 0)),
            pl.BlockSpec((1, d, d), lambda l, j: (l, 0, j)),
            pl.BlockSpec((1, 1, d), lambda l, j: (l, 0, j)),
        ],
        out_specs=pl.BlockSpec((1, bsz, d), lambda l, j: (l, 0, j)),
        out_shape=jax.ShapeDtypeStruct((depth, bsz, m), F32),
        name="adaln_mod",
    )(c, w_ada, b_ada.reshape(depth, 1, m))


def _inproj_kernel(x_ref, mod_ref, gpre_ref, wqkv_ref, wf_ref, wuv_ref, gv_ref,
                   q_ref, k_ref, v_ref, zf_ref, u_ref, sv_ref):
    x = x_ref[...]
    sh = mod_ref[0, 0:1, :]
    sc = mod_ref[0, 1:2, :]
    h = (_rms(x, gpre_ref[...]) * (1.0 + sc) + sh).astype(BF16)
    qkv = _dot(h, wqkv_ref[...])
    q_ref[...] = (qkv[:, :ATTN_W] * (HEAD_DIM ** -0.5 * LOG2E)).astype(BF16)
    k_ref[...] = qkv[:, ATTN_W:2 * ATTN_W].astype(BF16)
    v_ref[...] = qkv[:, 2 * ATTN_W:].astype(BF16)
    zf_ref[...] = _dot(h, wf_ref[...])
    uv = _dot(h, wuv_ref[...])
    uv = uv * (0.5 * (1.0 + jnp.tanh(GELU_C * (uv + 0.044715 * (uv * uv * uv)))))
    u_ref[...] = uv[:, :SGU_W].astype(BF16)
    s = uv[:, SGU_W:]
    sc_ = s - jnp.mean(s, axis=-1, keepdims=True)
    sv = sc_ * lax.rsqrt(jnp.mean(sc_ * sc_, axis=-1, keepdims=True) + EPS) * gv_ref[...]
    sv_ref[...] = sv.astype(BF16)


def _inproj(x2, mod_l, g_pre, w_qkv, w_f, w_uv, g_v, seq):
    n, d = x2.shape
    tm = TOK_TILE
    per_b = seq // tm
    row = lambda i: (i, 0)
    const = lambda i: (0, 0)
    outs = [jax.ShapeDtypeStruct((n, ATTN_W), BF16)] * 3 + [
        jax.ShapeDtypeStruct((n, LANES), F32),
        jax.ShapeDtypeStruct((n, SGU_W), BF16),
        jax.ShapeDtypeStruct((n, SGU_W), BF16)]
    return pl.pallas_call(
        _inproj_kernel,
        grid=(n // tm,),
        in_specs=[
            pl.BlockSpec((tm, d), row),
            pl.BlockSpec((1, N_MOD, d), lambda i: (i // per_b, 0, 0)),
            pl.BlockSpec((1, d), const),
            pl.BlockSpec(w_qkv.shape, const),
            pl.BlockSpec(w_f.shape, const),
            pl.BlockSpec(w_uv.shape, const),
            pl.BlockSpec((1, SGU_W), const),
        ],
        out_specs=[pl.BlockSpec((tm, ATTN_W), row)] * 3 + [
            pl.BlockSpec((tm, LANES), row),
            pl.BlockSpec((tm, SGU_W), row),
            pl.BlockSpec((tm, SGU_W), row)],
        out_shape=outs,
        name="inproj",
    )(x2, mod_l, g_pre, w_qkv, w_f, w_uv, g_v)


def _gates_kernel(z_ref, b_ref, o_ref):
    z = z_ref[0] + b_ref[...]
    ls = jnp.minimum(z, 0.0) - jnp.log1p(jnp.exp(-jnp.abs(z)))
    seq = ls.shape[-1]
    pos = lax.broadcasted_iota(jnp.int32, ls.shape, 1)
    shift = 1
    while shift < seq:
        ls = ls + jnp.where(pos >= shift, pltpu.roll(ls, shift, 1), 0.0)
        shift *= 2
    o_ref[0] = ls * LOG2E


def _forget_cumsum(zf_t, b_forget):
    bsz, h, seq = zf_t.shape
    return pl.pallas_call(
        _gates_kernel,
        grid=(bsz,),
        in_specs=[pl.BlockSpec((1, h, seq), lambda b: (b, 0, 0)),
                  pl.BlockSpec((h, 1), lambda b: (0, 0))],
        out_specs=pl.BlockSpec((1, h, seq), lambda b: (b, 0, 0)),
        out_shape=jax.ShapeDtypeStruct((bsz, h, seq), F32),
        name="forget_cumsum",
    )(zf_t, b_forget.reshape(h, 1))


def _attn_kernel(q_ref, k_ref, v_ref, lq_ref, lk_ref, o_ref):
    tq = ATT_TQ
    nq = q_ref.shape[0] // tq
    lane = lax.broadcasted_iota(jnp.int32, (tq, LANES), 1)
    left = lane < HEAD_DIM
    rows = lax.broadcasted_iota(jnp.int32, (tq, tq), 0)
    cols = lax.broadcasted_iota(jnp.int32, (tq, tq), 1)
    causal = cols <= rows
    zero = jnp.zeros((tq, LANES), BF16)
    one = jnp.ones((tq, LANES), BF16)
    def scores(qi):
        qs = slice(qi * tq, (qi + 1) * tq)
        q2 = q_ref[qs, :]
        qq = jnp.concatenate([jnp.where(left, q2, zero), jnp.where(left, zero, q2)], axis=0)
        t_blocks = [[], []]
        mt = [None, None]
        for j in range(qi + 1):
            ks = slice(j * tq, (j + 1) * tq)
            t2 = lax.dot_general(qq, k_ref[ks, :], (((1,), (1,)), ((), ())),
                                 preferred_element_type=F32)
            for hh in range(2):
                t = t2[hh * tq:(hh + 1) * tq] - lk_ref[0, 0, hh:hh + 1, ks]
                if j == qi:
                    t = jnp.where(causal, t, -jnp.inf)
                t_blocks[hh].append(t)
                mj = jnp.maximum(t[:, :LANES], t[:, LANES:])
                mt[hh] = mj if mt[hh] is None else jnp.maximum(mt[hh], mj)
        shift = []
        for hh in range(2):
            lq = lq_ref[0, qs, hh:hh + 1]
            row_max = jnp.max(mt[hh], axis=-1, keepdims=True) + lq
            shift.append(row_max - lq)
        return t_blocks, shift

    def values(qi, t_blocks, shift):
        qs = slice(qi * tq, (qi + 1) * tq)
        accs = [None, None]
        for j in range(qi + 1):
            ks = slice(j * tq, (j + 1) * tq)
            pp = jnp.concatenate([jnp.exp2(t_blocks[hh][j] - shift[hh]) for hh in range(2)],
                                 axis=0).astype(BF16)
            vaug = jnp.concatenate([v_ref[ks, :], one], axis=1)
            part = _dot(pp, vaug)
            accs[j % 2] = part if accs[j % 2] is None else accs[j % 2] + part
        acc = accs[0] if accs[1] is None else accs[0] + accs[1]
        out0 = acc[:tq, :LANES] / acc[:tq, LANES:]
        out1 = acc[tq:, :LANES] / acc[tq:, LANES:]
        o_ref[qs, :] = jnp.where(left, out0, out1).astype(o_ref.dtype)

    pending = scores(0)
    for qi in range(nq):
        nxt = scores(qi + 1) if qi + 1 < nq else None
        values(qi, *pending)
        pending = nxt


def _attention(q, k, v, l_pairs_col, l_pairs_row, bsz, seq):
    n = q.shape[0]
    npair = N_HEADS // 2
    blk = pl.BlockSpec((seq, LANES), lambda b, p: (b, p))
    return pl.pallas_call(
        _attn_kernel,
        grid=(bsz, npair),
        in_specs=[
            blk, blk, blk,
            pl.BlockSpec((1, seq, 2), lambda b, p: (p, b, 0)),
            pl.BlockSpec((1, 1, 2, seq), lambda b, p: (b, p, 0, 0)),
        ],
        out_specs=blk,
        out_shape=jax.ShapeDtypeStruct((n, ATTN_W), BF16),
        name="fox_attention",
    )(q, k, v, l_pairs_col, l_pairs_row)


def _sgu_kernel(u_ref, sv_ref, w_ref, bt_ref, g_ref, o_ref):
    tm = u_ref.shape[0]
    npair = N_GROUPS // 2
    lane = lax.broadcasted_iota(jnp.int32, (CHUNK, LANES), 1)
    left = lane < HEAD_DIM
    r = lax.broadcasted_iota(jnp.int32, (CHUNK, CHUNK), 0)
    c = lax.broadcasted_iota(jnp.int32, (CHUNK, CHUNK), 1)
    causal = c <= r
    lhs, bias = [], []
    for j in range(npair):
        wa = jnp.where(causal, w_ref[2 * j], jnp.zeros((), BF16))
        wb = jnp.where(causal, w_ref[2 * j + 1], jnp.zeros((), BF16))
        lhs.append(jnp.concatenate([wa, wb], axis=1))
        bias.append(jnp.where(left, bt_ref[:, 2 * j:2 * j + 1], bt_ref[:, 2 * j + 1:2 * j + 2]))
    for ci in range(tm // CHUNK):
        rs = slice(ci * CHUNK, (ci + 1) * CHUNK)
        blks = []
        ssq = jnp.zeros((CHUNK, 1), F32)
        for j in range(npair):
            cs = slice(j * LANES, (j + 1) * LANES)
            svb = sv_ref[rs, cs]
            zero = jnp.zeros_like(svb)
            rhs = jnp.concatenate([jnp.where(left, svb, zero), jnp.where(left, zero, svb)], axis=0)
            mixed = _dot(lhs[j], rhs) + bias[j]
            ob = u_ref[rs, cs].astype(F32) * mixed
            ssq = ssq + jnp.sum(ob * ob, axis=-1, keepdims=True)
            blks.append(ob)
        inv = lax.rsqrt(ssq * (1.0 / SGU_W) + EPS)
        for j in range(npair):
            cs = slice(j * LANES, (j + 1) * LANES)
            o_ref[rs, cs] = (blks[j] * inv * g_ref[:, cs]).astype(o_ref.dtype)


def _sgu(u, sv, w_sp, b_sp_t, g_out):
    n = u.shape[0]
    tm = TOK_TILE
    row = lambda i: (i, 0)
    return pl.pallas_call(
        _sgu_kernel,
        grid=(n // tm,),
        in_specs=[
            pl.BlockSpec((tm, SGU_W), row),
            pl.BlockSpec((tm, SGU_W), row),
            pl.BlockSpec(w_sp.shape, lambda i: (0, 0, 0)),
            pl.BlockSpec(b_sp_t.shape, lambda i: (0, 0)),
            pl.BlockSpec((1, SGU_W), lambda i: (0, 0)),
        ],
        out_specs=pl.BlockSpec((tm, SGU_W), row),
        out_shape=jax.ShapeDtypeStruct((n, SGU_W), BF16),
        name="sgu",
    )(u, sv, w_sp, b_sp_t, g_out)


def _outproj_kernel(a_ref, s_ref, ga_ref, wa_ref, ws_ref, x_ref, mod_ref, gpost_ref, gffn_ref,
                    *rest, with_router):
    if with_router:
        wr_ref, xo_ref, h_ref, lg_ref = rest
    else:
        xo_ref, h_ref = rest
    a = _rms(a_ref[...].astype(F32), ga_ref[...]).astype(BF16)
    y = _dot(a, wa_ref[...]) + _dot(s_ref[...], ws_ref[...])
    gt_m = mod_ref[0, 2:3, :]
    sh_f = mod_ref[0, 3:4, :]
    sc_f = mod_ref[0, 4:5, :]
    xn = x_ref[...] + (1.0 + gt_m) * _rms(y, gpost_ref[...])
    xo_ref[...] = xn
    h = _rms(xn, gffn_ref[...]) * (1.0 + sc_f) + sh_f
    h_ref[...] = h.astype(BF16)
    if with_router:
        wr = wr_ref[...]
        h_hi = h.astype(BF16)
        h_lo = (h - h_hi.astype(F32)).astype(BF16)
        w_hi = wr.astype(BF16)
        w_lo = (wr - w_hi.astype(F32)).astype(BF16)
        lg_ref[...] = _dot(h_hi, w_hi) + (_dot(h_hi, w_lo) + _dot(h_lo, w_hi))


def _outproj(attn, sgun, g_attn, w_a, w_s, x2, mod_l, g_post, g_ffn, w_router, seq):
    n, d = x2.shape
    tm = TOK_TILE
    per_b = seq // tm
    row = lambda i: (i, 0)
    const = lambda i: (0, 0)
    with_router = w_router is not None
    in_specs = [
        pl.BlockSpec((tm, ATTN_W), row),
        pl.BlockSpec((tm, SGU_W), row),
        pl.BlockSpec((1, ATTN_W), const),
        pl.BlockSpec(w_a.shape, const),
        pl.BlockSpec(w_s.shape, const),
        pl.BlockSpec((tm, d), row),
        pl.BlockSpec((1, N_MOD, d), lambda i: (i // per_b, 0, 0)),
        pl.BlockSpec((1, d), const),
        pl.BlockSpec((1, d), const),
    ]
    args = [attn, sgun, g_attn, w_a, w_s, x2, mod_l, g_post, g_ffn]
    out_specs = [pl.BlockSpec((tm, d), row), pl.BlockSpec((tm, d), row)]
    out_shape = [jax.ShapeDtypeStruct((n, d), F32), jax.ShapeDtypeStruct((n, d), BF16)]
    if with_router:
        in_specs.append(pl.BlockSpec(w_router.shape, const))
        args.append(w_router)
        out_specs.append(pl.BlockSpec((tm, LANES), row))
        out_shape.append(jax.ShapeDtypeStruct((n, LANES), F32))
    return pl.pallas_call(
        functools.partial(_outproj_kernel, with_router=with_router),
        grid=(n // tm,),
        in_specs=in_specs,
        out_specs=out_specs,
        out_shape=out_shape,
        name="outproj_router" if with_router else "outproj",
    )(*args)


def _ffn_kernel(h_ref, wgu_ref, wd_ref, x_ref, mod_ref, gpost_ref, o_ref):
    h = h_ref[...]
    y = None
    for a, b in FFN_CHUNKS:
        g = _dot(h, wgu_ref[:, a:b])
        u = _dot(h, wgu_ref[:, D_FF_DENSE + a:D_FF_DENSE + b])
        act = (g * jax.nn.sigmoid(g) * u).astype(BF16)
        part = _dot(act, wd_ref[a:b, :])
        y = part if y is None else y + part
    gt_f = mod_ref[0, 5:6, :]
    o_ref[...] = x_ref[...] + (1.0 + gt_f) * _rms(y, gpost_ref[...])


def _ffn_dense(h2, w_gu, w_down, x2, mod_l, g_post, seq):
    n, d = x2.shape
    tm = TOK_TILE
    per_b = seq // tm
    resident = dict(pipeline_mode=pl.Buffered(1))
    return pl.pallas_call(
        _ffn_kernel,
        grid=(n // tm,),
        in_specs=[
            pl.BlockSpec((tm, d), lambda i: (i, 0)),
            pl.BlockSpec(w_gu.shape, lambda i: (0, 0), **resident),
            pl.BlockSpec(w_down.shape, lambda i: (0, 0), **resident),
            pl.BlockSpec((tm, d), lambda i: (i, 0)),
            pl.BlockSpec((1, N_MOD, d), lambda i: (i // per_b, 0, 0)),
            pl.BlockSpec((1, d), lambda i: (0, 0)),
        ],
        out_specs=pl.BlockSpec((tm, d), lambda i: (i, 0)),
        out_shape=jax.ShapeDtypeStruct((n, d), F32),
        name="ffn_dense",
    )(h2, w_gu, w_down, x2, mod_l, g_post)


def _route_kernel(lg_ref, keyc_ref, gate_ref, keyt_ref, ends_ref, base_ref):
    s = pl.program_id(0)
    tm = lg_ref.shape[0]

    @pl.when(s == 0)
    def _():
        base_ref[...] = jnp.zeros_like(base_ref)

    lane = lax.broadcasted_iota(jnp.int32, (tm, LANES), 1)
    lg = jnp.where(lane < N_EXPERTS, lg_ref[...], -jnp.inf)
    m1 = jnp.max(lg, axis=-1, keepdims=True)
    i1 = jnp.min(jnp.where(lg == m1, lane, LANES), axis=-1, keepdims=True)
    sel1 = lane == i1
    lg2 = jnp.where(sel1, -jnp.inf, lg)
    m2 = jnp.max(lg2, axis=-1, keepdims=True)
    i2 = jnp.min(jnp.where(lg2 == m2, lane, LANES), axis=-1, keepdims=True)
    sel2 = lane == i2
    e2 = jnp.exp(m2 - m1)
    w1 = 1.0 / (1.0 + e2)
    w2 = e2 / (1.0 + e2)
    gate_ref[...] = jnp.where(sel1, w1, 0.0) + jnp.where(sel2, w2, 0.0)
    chosen = jnp.logical_or(sel1, sel2)
    mask = jnp.where(chosen, 1.0, 0.0)
    r = lax.broadcasted_iota(jnp.int32, (tm, tm), 0)
    c = lax.broadcasted_iota(jnp.int32, (tm, tm), 1)
    tri = jnp.where(c < r, 1.0, 0.0).astype(BF16)
    base = base_ref[0:1, :]
    rank = _dot(tri, mask.astype(BF16)) + base
    key = jnp.where(chosen, rank, -1.0)
    keyc_ref[...] = key
    keyt_ref[...] = key.T
    new_base = base + jnp.sum(mask, axis=0, keepdims=True)
    base_ref[...] = jnp.broadcast_to(new_base, base_ref.shape)
    ends_ref[0] = jnp.broadcast_to(new_base, ends_ref.shape[1:])


def _route(logits):
    n = logits.shape[0]
    tm = TOK_TILE
    ns = n // tm
    row = lambda s: (s, 0)
    return pl.pallas_call(
        _route_kernel,
        grid=(ns,),
        in_specs=[pl.BlockSpec((tm, LANES), row)],
        out_specs=[
            pl.BlockSpec((tm, LANES), row),
            pl.BlockSpec((tm, LANES), row),
            pl.BlockSpec((LANES, tm), lambda s: (0, s)),
            pl.BlockSpec((1, 8, LANES), lambda s: (s, 0, 0)),
        ],
        out_shape=[
            jax.ShapeDtypeStruct((n, LANES), F32),
            jax.ShapeDtypeStruct((n, LANES), F32),
            jax.ShapeDtypeStruct((LANES, n), F32),
            jax.ShapeDtypeStruct((ns, 8, LANES), F32),
        ],
        scratch_shapes=[pltpu.VMEM((8, LANES), F32)],
        name="moe_route",
    )(logits)


def _work_lists(ends, n_tokens):
    e_n = N_EXPERTS
    sub = EXP_TM // DISP_TD
    max_tiles = (2 * n_tokens) // EXP_TM + e_n
    i32 = jnp.int32

    ends_i = ends[:, 0, :e_n].astype(i32)
    base = jnp.concatenate([jnp.zeros((1, e_n), i32), ends_i], axis=0)
    cnt = base[-1]
    ntile = (cnt + EXP_TM - 1) // EXP_TM
    tile_end = jnp.cumsum(ntile)
    tile_start = tile_end - ntile
    total_tiles = tile_end[-1]
    npass = (cnt + EXP_SUB - 1) // EXP_SUB
    off = tile_start * EXP_TM

    t = jnp.arange(max_tiles, dtype=i32)
    tc = jnp.minimum(t, total_tiles - 1)
    tile_e = jnp.sum((tile_end[None, :] <= tc[:, None]).astype(i32), axis=1)
    tile_pass = jnp.clip(npass[tile_e] - (tc - tile_start[tile_e]) * (EXP_TM // EXP_SUB),
                         1, EXP_TM // EXP_SUB)
    tile_pass = jnp.where(t < total_tiles, tile_pass, 0)

    j = jnp.arange(max_tiles * sub, dtype=i32)
    jt = jnp.minimum(j // sub, total_tiles - 1)
    je = tile_e[jt]
    jm = (jt - tile_start[je]) * sub + j % sub
    lo_rank = jm * DISP_TD
    hi_rank = jnp.minimum(lo_rank + DISP_TD, cnt[je])
    base_e = base.T[je]
    s_lo = jnp.sum((base_e[:, 1:] <= lo_rank[:, None]).astype(i32), axis=1)
    s_hi = jnp.sum((base_e[:, :-1] < hi_rank[:, None]).astype(i32), axis=1)
    n_src = jnp.where(j // sub < total_tiles, jnp.maximum(s_hi - s_lo, 0), 0)

    rows = max_tiles * EXP_TM
    start = off[None, :] + base[:-1]
    stop = off[None, :] + base[1:]
    wstart = jnp.minimum(start // COMB_ALIGN * COMB_ALIGN, rows - COMB_WIN)
    need = jnp.where(stop > start, stop - wstart, 0)
    rel = wstart - off[None, :]
    return dict(
        disp=(je, jm, s_lo, n_src),
        comb=((wstart // COMB_ALIGN).reshape(-1), rel.reshape(-1), need.reshape(-1)),
        tiles=(tc, tile_e, tile_pass),
        max_tiles=max_tiles)


def _dispatch_kernel(e_ref, m_ref, slo_ref, n_ref, h_ref, keyt_ref, o_ref, acc_ref):
    j = pl.program_id(0)
    td = o_ref.shape[0]
    acc_ref[...] = jnp.zeros_like(acc_ref)
    e = e_ref[j]
    slot = (m_ref[j] * td + lax.broadcasted_iota(jnp.int32, (td, TOK_TILE), 0)).astype(F32)

    def body(i, carry):
        ts = pl.multiple_of((slo_ref[j] + i) * TOK_TILE, TOK_TILE)
        key = keyt_ref[pl.ds(e, 1), pl.ds(ts, TOK_TILE)]
        onehot = jnp.where(key == slot, 1.0, 0.0).astype(BF16)
        acc_ref[...] += _dot(onehot, h_ref[pl.ds(ts, TOK_TILE), :])
        return carry

    lax.fori_loop(0, n_ref[j], body, 0)
    o_ref[...] = acc_ref[...].astype(o_ref.dtype)


def _dispatch(h2, key_t, lists, max_tiles):
    n, d = h2.shape
    rows = max_tiles * EXP_TM
    resident = dict(pipeline_mode=pl.Buffered(1))
    grid_spec = pltpu.PrefetchScalarGridSpec(
        num_scalar_prefetch=4,
        grid=(rows // DISP_TD,),
        in_specs=[
            pl.BlockSpec((n, d), lambda j, *_: (0, 0), **resident),
            pl.BlockSpec((8, n), lambda j, *_: (0, 0), **resident),
        ],
        out_specs=pl.BlockSpec((DISP_TD, d), lambda j, *_: (j, 0)),
        scratch_shapes=[pltpu.VMEM((DISP_TD, d), F32)],
    )
    return pl.pallas_call(
        _dispatch_kernel,
        grid_spec=grid_spec,
        out_shape=jax.ShapeDtypeStruct((rows, d), BF16),
        name="moe_dispatch",
    )(*lists, h2, key_t)


def _expert_kernel(tidx_ref, te_ref, tp_ref, x_ref, wg_ref, wu_ref, wd_ref, y_ref, acc_ref):
    i = pl.program_id(0)
    f = pl.program_id(1)
    nf = pl.num_programs(1)

    def one_pass(r, carry):
        rs = pl.ds(pl.multiple_of(r * EXP_SUB, EXP_SUB), EXP_SUB)
        x = x_ref[rs, :]
        g = _dot(x, wg_ref[0])
        u = _dot(x, wu_ref[0])
        act = (g * jax.nn.sigmoid(g) * u).astype(BF16)
        part = _dot(act, wd_ref[0])

        @pl.when(f == 0)
        def _():
            acc_ref[rs, :] = part

        @pl.when(f > 0)
        def _():
            acc_ref[rs, :] += part

        @pl.when(f == nf - 1)
        def _():
            y_ref[rs, :] = acc_ref[rs, :].astype(y_ref.dtype)

        return carry

    lax.fori_loop(0, tp_ref[i], one_pass, 0)

    @pl.when(f == nf - 1)
    def _():
        for r in range(EXP_TM // EXP_SUB):
            @pl.when(r >= tp_ref[i])
            def _(r=r):
                y_ref[r * EXP_SUB:(r + 1) * EXP_SUB, :] = jnp.zeros((EXP_SUB, y_ref.shape[1]),
                                                                    y_ref.dtype)


def _experts(xs, w_gu, w_down, tiles, max_tiles):
    rows, d = xs.shape
    tm, tf = EXP_TM, EXP_TF
    nf = D_FF_EXPERT // tf

    def fsel(i, f, tp):
        return jnp.where(tp[i] > 0, f, nf - 1)

    grid_spec = pltpu.PrefetchScalarGridSpec(
        num_scalar_prefetch=3,
        grid=(max_tiles, nf),
        in_specs=[
            pl.BlockSpec((tm, d), lambda i, f, ti, te, tv: (ti[i], 0)),
            pl.BlockSpec((1, d, tf), lambda i, f, ti, te, tv: (te[i], 0, fsel(i, f, tv))),
            pl.BlockSpec((1, d, tf), lambda i, f, ti, te, tv: (te[i], 0, nf + fsel(i, f, tv))),
            pl.BlockSpec((1, tf, d), lambda i, f, ti, te, tv: (te[i], fsel(i, f, tv), 0)),
        ],
        out_specs=pl.BlockSpec((tm, d), lambda i, f, ti, te, tv: (i, 0)),
        scratch_shapes=[pltpu.VMEM((tm, d), F32)],
    )
    return pl.pallas_call(
        _expert_kernel,
        grid_spec=grid_spec,
        out_shape=jax.ShapeDtypeStruct((rows, d), BF16),
        name="moe_experts",
    )(*tiles, xs, w_gu, w_gu, w_down)


def _combine_kernel(ws_ref, rel_ref, need_ref, *refs):
    y_refs = refs[:N_EXPERTS]
    keyc_ref, gate_ref, x_ref, mod_ref, gpost_ref, o_ref, acc_ref = refs[N_EXPERTS:]
    s = pl.program_id(0)
    ts = acc_ref.shape[0]
    acc_ref[...] = jnp.zeros_like(acc_ref)
    for e in range(N_EXPERTS):
        key = keyc_ref[:, e:e + 1]
        gate = gate_ref[:, e:e + 1]
        rel = rel_ref[s * N_EXPERTS + e]
        need = need_ref[s * N_EXPERTS + e]
        for a, b in COMB_CHUNKS:
            @pl.when(need > a)
            def _(e=e, a=a, b=b, key=key, gate=gate, rel=rel):
                slot = (rel + a + lax.broadcasted_iota(jnp.int32, (ts, b - a), 1)).astype(F32)
                onehot = jnp.where(key == slot, 1.0, 0.0).astype(BF16)
                acc_ref[...] += gate * _dot(onehot, y_refs[e][a:b, :])
    gt_f = mod_ref[0, 5:6, :]
    o_ref[...] = x_ref[...] + (1.0 + gt_f) * _rms(acc_ref[...], gpost_ref[...])


def _combine(y, key_c, gates, x2, mod_l, g_post, lists, seq):
    n, d = x2.shape
    per_b = seq // TOK_TILE
    tok = lambda s, *_: (s, 0)

    def window(e):
        return pl.BlockSpec((pl.Element(COMB_WIN), pl.Element(d)),
                            lambda s, ws, rel, need: (ws[s * N_EXPERTS + e] * COMB_ALIGN, 0))

    grid_spec = pltpu.PrefetchScalarGridSpec(
        num_scalar_prefetch=3,
        grid=(n // TOK_TILE,),
        in_specs=[window(e) for e in range(N_EXPERTS)] + [
            pl.BlockSpec((TOK_TILE, LANES), tok),
            pl.BlockSpec((TOK_TILE, LANES), tok),
            pl.BlockSpec((TOK_TILE, d), tok),
            pl.BlockSpec((1, N_MOD, d), lambda s, *_: (s // per_b, 0, 0)),
            pl.BlockSpec((1, d), lambda s, *_: (0, 0)),
        ],
        out_specs=pl.BlockSpec((TOK_TILE, d), tok),
        scratch_shapes=[pltpu.VMEM((TOK_TILE, d), F32)],
    )
    return pl.pallas_call(
        _combine_kernel,
        grid_spec=grid_spec,
        out_shape=jax.ShapeDtypeStruct((n, d), F32),
        name="moe_combine",
    )(*lists, *([y] * N_EXPERTS), key_c, gates, x2, mod_l, g_post)


def kernel(x, c, w_ada, b_ada, g_pre_mix, g_post_mix, g_pre_ffn, g_post_ffn, w_in, b_forget, g_v,
           w_spatial, b_spatial, g_out_attn, g_out_sgu, w_out, w_gate_up_dense, w_down_dense,
           w_router, w_gate_up_exp, w_down_exp):
    bsz, seq, d = x.shape
    depth = w_ada.shape[0]
    n = bsz * seq
    npair = N_HEADS // 2
    assert d == D_MODEL and seq % TOK_TILE == 0 and seq % ATT_TQ == 0

    mod = _modulation(c, w_ada, b_ada).reshape(depth, bsz, N_MOD, d)
    x2 = x.reshape(n, d)
    o3 = 3 * ATTN_W
    o4 = o3 + N_HEADS
    for l in range(depth):
        mod_l = mod[l]
        w_l = w_in[l]
        w_qkv = w_l[:, :o3].astype(BF16)
        w_f = jnp.pad(w_l[:, o3:o4], ((0, 0), (0, LANES - N_HEADS))).astype(BF16)
        w_uv = w_l[:, o4:].astype(BF16)
        q, k, v, zf, u, sv = _inproj(x2, mod_l, g_pre_mix[l].reshape(1, d), w_qkv, w_f, w_uv,
                                     g_v[l].reshape(1, SGU_W), seq)
        zf_t = zf[:, :N_HEADS].reshape(bsz, seq, N_HEADS).transpose(0, 2, 1)
        lcum = _forget_cumsum(zf_t, b_forget[l])
        l_row = lcum.reshape(bsz, npair, 2, seq)
        l_col = l_row.transpose(1, 0, 3, 2).reshape(npair, n, 2)
        attn = _attention(q, k, v, l_col, l_row, bsz, seq)
        sgun = _sgu(u, sv, w_spatial[l].astype(BF16), b_spatial[l].T,
                    g_out_sgu[l].reshape(1, SGU_W))
        w_o = w_out[l].astype(BF16)
        moe = (l % 2 == 1)
        w_r = None
        if moe:
            w_r = jnp.pad(w_router[l // 2], ((0, 0), (0, LANES - N_EXPERTS)))
        res = _outproj(attn, sgun, g_out_attn[l].reshape(1, ATTN_W), w_o[:ATTN_W], w_o[ATTN_W:],
                       x2, mod_l, g_post_mix[l].reshape(1, d), g_pre_ffn[l].reshape(1, d), w_r, seq)
        g_post = g_post_ffn[l].reshape(1, d)
        if not moe:
            x2, h2 = res
            x2 = _ffn_dense(h2, w_gate_up_dense[l // 2].astype(BF16),
                            w_down_dense[l // 2].astype(BF16), x2, mod_l, g_post, seq)
        else:
            x2, h2, logits = res
            key_c, gates, key_t, ends = _route(logits)
            wl = _work_lists(ends, n)
            xs = _dispatch(h2, key_t, wl["disp"], wl["max_tiles"])
            y = _experts(xs, w_gate_up_exp[l // 2].astype(BF16), w_down_exp[l // 2].astype(BF16),
                         wl["tiles"], wl["max_tiles"])
            x2 = _combine(y, key_c, gates, x2, mod_l, g_post, wl["comb"], seq)
    return x2.reshape(bsz, seq, d)
```

```python
import functools
import math

import jax
import jax.numpy as jnp
from jax import lax
from jax.experimental import pallas as pl
from jax.experimental.pallas import tpu as pltpu

F32 = jnp.float32
BF16 = jnp.bfloat16

D_MODEL = 1024
HEAD_DIM = 64
N_HEADS = 8
N_GROUPS = 8
ATTN_W = N_HEADS * HEAD_DIM
SGU_W = N_GROUPS * HEAD_DIM
CHUNK = 128
D_FF_DENSE = 2816
N_EXPERTS = 8
D_FF_EXPERT = 3584
N_MOD = 6
EPS = 1e-6

LANES = 128
TOK_TILE = 512
ATT_TQ = 256
MXU_TILE = 256
FFN_CHUNKS = ((0, 6 * MXU_TILE), (6 * MXU_TILE, D_FF_DENSE))
EXP_TM = 1024
EXP_SUB = 512
EXP_TF = 7 * MXU_TILE
LOG2E = math.log2(math.e)
DISP_TD = 256
COMB_ALIGN = 16
COMB_WIN = TOK_TILE + COMB_ALIGN
COMB_CHUNKS = ((0, MXU_TILE), (MXU_TILE, COMB_WIN))
UNROUTED = -float(2 ** 30)
GELU_C = math.sqrt(2.0 / math.pi)


def _rms(x, g):
    return x * lax.rsqrt(jnp.mean(x * x, axis=-1, keepdims=True) + EPS) * g


def _dot(a, b):
    return jnp.dot(a, b, preferred_element_type=F32)


def _mod_kernel(c_ref, w_ref, b_ref, o_ref):
    c = c_ref[...]
    ca = (c * jax.nn.sigmoid(c)).astype(BF16)
    o_ref[0] = _dot(ca, w_ref[0].astype(BF16)) + b_ref[0]


def _modulation(c, w_ada, b_ada):
    depth, d, m = w_ada.shape
    bsz = c.shape[0]
    nblk = m // d
    return pl.pallas_call(
        _mod_kernel,
        grid=(depth, nblk),
        in_specs=[
            pl.BlockSpec((bsz, d), lambda l, j: (0, 0)),
            pl.BlockSpec((1, d, d), lambda l, j: (l, 0, j)),
            pl.BlockSpec((1, 1, d), lambda l, j: (l, 0, j)),
        ],
        out_specs=pl.BlockSpec((1, bsz, d), lambda l, j: (l, 0, j)),
        out_shape=jax.ShapeDtypeStruct((depth, bsz, m), F32),
        name="adaln_mod",
    )(c, w_ada, b_ada.reshape(depth, 1, m))


def _inproj_kernel(x_ref, mod_ref, gpre_ref, wqkv_ref, wf_ref, wuv_ref, gv_ref,
                   q_ref, k_ref, v_ref, zf_ref, u_ref, sv_ref):
    x = x_ref[...]
    sh = mod_ref[0, 0:1, :]
    sc = mod_ref[0, 1:2, :]
    h = (_rms(x, gpre_ref[...]) * (1.0 + sc) + sh).astype(BF16)
    qkv = _dot(h, wqkv_ref[...])
    q_ref[...] = (qkv[:, :ATTN_W] * (HEAD_DIM ** -0.5 * LOG2E)).astype(BF16)
    k_ref[...] = qkv[:, ATTN_W:2 * ATTN_W].astype(BF16)
    v_ref[...] = qkv[:, 2 * ATTN_W:].astype(BF16)
    zf_ref[...] = _dot(h, wf_ref[...]).T[:N_HEADS, :]
    uv = _dot(h, wuv_ref[...])
    uv = uv * (0.5 * (1.0 + jnp.tanh(GELU_C * (uv + 0.044715 * (uv * uv * uv)))))
    u_ref[...] = uv[:, :SGU_W].astype(BF16)
    s = uv[:, SGU_W:]
    sc_ = s - jnp.mean(s, axis=-1, keepdims=True)
    sv = sc_ * lax.rsqrt(jnp.mean(sc_ * sc_, axis=-1, keepdims=True) + EPS) * gv_ref[...]
    sv_ref[...] = sv.astype(BF16)


def _inproj(x2, mod_l, g_pre, w_qkv, w_f, w_uv, g_v, seq):
    n, d = x2.shape
    tm = TOK_TILE
    per_b = seq // tm
    row = lambda i: (i, 0)
    const = lambda i: (0, 0)
    outs = [jax.ShapeDtypeStruct((n, ATTN_W), BF16)] * 3 + [
        jax.ShapeDtypeStruct((N_HEADS, n), F32),
        jax.ShapeDtypeStruct((n, SGU_W), BF16),
        jax.ShapeDtypeStruct((n, SGU_W), BF16)]
    return pl.pallas_call(
        _inproj_kernel,
        grid=(n // tm,),
        in_specs=[
            pl.BlockSpec((tm, d), row),
            pl.BlockSpec((1, N_MOD, d), lambda i: (i // per_b, 0, 0)),
            pl.BlockSpec((1, d), const),
            pl.BlockSpec(w_qkv.shape, const),
            pl.BlockSpec(w_f.shape, const),
            pl.BlockSpec(w_uv.shape, const),
            pl.BlockSpec((1, SGU_W), const),
        ],
        out_specs=[pl.BlockSpec((tm, ATTN_W), row)] * 3 + [
            pl.BlockSpec((N_HEADS, tm), lambda i: (0, i)),
            pl.BlockSpec((tm, SGU_W), row),
            pl.BlockSpec((tm, SGU_W), row)],
        out_shape=outs,
        name="inproj",
    )(x2, mod_l, g_pre, w_qkv, w_f, w_uv, g_v)


def _gates_kernel(z_ref, b_ref, o_ref):
    z = z_ref[...] + b_ref[...]
    ls = jnp.minimum(z, 0.0) - jnp.log1p(jnp.exp(-jnp.abs(z)))
    seq = ls.shape[-1]
    pos = lax.broadcasted_iota(jnp.int32, ls.shape, 1)
    shift = 1
    while shift < seq:
        ls = ls + jnp.where(pos >= shift, pltpu.roll(ls, shift, 1), 0.0)
        shift *= 2
    o_ref[...] = ls * LOG2E


def _forget_cumsum(zf_t, b_forget, seq):
    h, n = zf_t.shape
    return pl.pallas_call(
        _gates_kernel,
        grid=(n // seq,),
        in_specs=[pl.BlockSpec((h, seq), lambda b: (0, b)),
                  pl.BlockSpec((h, 1), lambda b: (0, 0))],
        out_specs=pl.BlockSpec((h, seq), lambda b: (0, b)),
        out_shape=jax.ShapeDtypeStruct((h, n), F32),
        name="forget_cumsum",
    )(zf_t, b_forget.reshape(h, 1))


def _attn_kernel(q_ref, k_ref, v_ref, lk_ref, *rest):
    if len(rest) == 3:
        wsrc_ref, o_ref, wdst_ref = rest
        wdst_ref[...] = wsrc_ref[...].astype(wdst_ref.dtype)
    else:
        (o_ref,) = rest
    tq = ATT_TQ
    seq = q_ref.shape[0]
    nq = seq // tq
    l_rows = jnp.concatenate([lk_ref[0], jnp.zeros((LANES - 2, seq), F32)], axis=0)
    l_cols = l_rows.T
    lane = lax.broadcasted_iota(jnp.int32, (tq, LANES), 1)
    left = lane < HEAD_DIM
    rows = lax.broadcasted_iota(jnp.int32, (tq, tq), 0)
    cols = lax.broadcasted_iota(jnp.int32, (tq, tq), 1)
    causal = cols <= rows
    zero = jnp.zeros((tq, LANES), BF16)
    one = jnp.ones((tq, LANES), BF16)
    def scores(qi):
        qs = slice(qi * tq, (qi + 1) * tq)
        q2 = q_ref[qs, :]
        qq = jnp.concatenate([jnp.where(left, q2, zero), jnp.where(left, zero, q2)], axis=0)
        t_blocks = [[], []]
        mt = [None, None]
        for j in range(qi + 1):
            ks = slice(j * tq, (j + 1) * tq)
            t2 = lax.dot_general(qq, k_ref[ks, :], (((1,), (1,)), ((), ())),
                                 preferred_element_type=F32)
            for hh in range(2):
                t = t2[hh * tq:(hh + 1) * tq] - lk_ref[0, hh:hh + 1, ks]
                if j == qi:
                    t = jnp.where(causal, t, -jnp.inf)
                t_blocks[hh].append(t)
                mj = jnp.maximum(t[:, :LANES], t[:, LANES:])
                mt[hh] = mj if mt[hh] is None else jnp.maximum(mt[hh], mj)
        shift = []
        for hh in range(2):
            lq = l_cols[qs, hh:hh + 1]
            row_max = jnp.max(mt[hh], axis=-1, keepdims=True) + lq
            shift.append(row_max - lq)
        return t_blocks, shift

    def values(qi, t_blocks, shift):
        qs = slice(qi * tq, (qi + 1) * tq)
        accs = [None, None]
        for j in range(qi + 1):
            ks = slice(j * tq, (j + 1) * tq)
            pp = jnp.concatenate([jnp.exp2(t_blocks[hh][j] - shift[hh]) for hh in range(2)],
                                 axis=0).astype(BF16)
            vaug = jnp.concatenate([v_ref[ks, :], one], axis=1)
            part = _dot(pp, vaug)
            accs[j % 2] = part if accs[j % 2] is None else accs[j % 2] + part
        acc = accs[0] if accs[1] is None else accs[0] + accs[1]
        out0 = acc[:tq, :LANES] / acc[:tq, LANES:]
        out1 = acc[tq:, :LANES] / acc[tq:, LANES:]
        o_ref[qs, :] = jnp.where(left, out0, out1).astype(o_ref.dtype)

    pending = scores(0)
    for qi in range(nq):
        nxt = scores(qi + 1) if qi + 1 < nq else None
        values(qi, *pending)
        pending = nxt


def _attention(q, k, v, l_pairs, bsz, seq, w_f32=None):
    n = q.shape[0]
    npair = N_HEADS // 2
    blk = pl.BlockSpec((seq, LANES), lambda b, p: (b, p))
    in_specs = [blk, blk, blk, pl.BlockSpec((1, 2, seq), lambda b, p: (p, 0, b))]
    out_specs = [blk]
    out_shape = [jax.ShapeDtypeStruct((n, ATTN_W), BF16)]
    args = [q, k, v, l_pairs]
    if w_f32 is not None:
        rows, cols = w_f32.shape
        rb = rows // (bsz * npair)
        assert rb * bsz * npair == rows and rb % COMB_ALIGN == 0
        wblk = pl.BlockSpec((rb, cols), lambda b, p: (b * npair + p, 0))
        in_specs.append(wblk)
        out_specs.append(wblk)
        out_shape.append(jax.ShapeDtypeStruct((rows, cols), BF16))
        args.append(w_f32)
    res = pl.pallas_call(
        _attn_kernel,
        grid=(bsz, npair),
        in_specs=in_specs,
        out_specs=out_specs,
        out_shape=out_shape,
        name="fox_attention",
    )(*args)
    return res if w_f32 is not None else (res[0], None)


def _sgu_kernel(u_ref, sv_ref, w_ref, bt_ref, g_ref, o_ref):
    tm = u_ref.shape[0]
    npair = N_GROUPS // 2
    lane = lax.broadcasted_iota(jnp.int32, (CHUNK, LANES), 1)
    left = lane < HEAD_DIM
    r = lax.broadcasted_iota(jnp.int32, (CHUNK, CHUNK), 0)
    c = lax.broadcasted_iota(jnp.int32, (CHUNK, CHUNK), 1)
    causal = c <= r
    lhs, bias = [], []
    for j in range(npair):
        wa = jnp.where(causal, w_ref[2 * j], jnp.zeros((), BF16))
        wb = jnp.where(causal, w_ref[2 * j + 1], jnp.zeros((), BF16))
        lhs.append(jnp.concatenate([wa, wb], axis=1))
        bias.append(jnp.where(left, bt_ref[:, 2 * j:2 * j + 1], bt_ref[:, 2 * j + 1:2 * j + 2]))
    for ci in range(tm // CHUNK):
        rs = slice(ci * CHUNK, (ci + 1) * CHUNK)
        blks = []
        ssq = jnp.zeros((CHUNK, 1), F32)
        for j in range(npair):
            cs = slice(j * LANES, (j + 1) * LANES)
            svb = sv_ref[rs, cs]
            zero = jnp.zeros_like(svb)
            rhs = jnp.concatenate([jnp.where(left, svb, zero), jnp.where(left, zero, svb)], axis=0)
            mixed = _dot(lhs[j], rhs) + bias[j]
            ob = u_ref[rs, cs].astype(F32) * mixed
            ssq = ssq + jnp.sum(ob * ob, axis=-1, keepdims=True)
            blks.append(ob)
        inv = lax.rsqrt(ssq * (1.0 / SGU_W) + EPS)
        for j in range(npair):
            cs = slice(j * LANES, (j + 1) * LANES)
            o_ref[rs, cs] = (blks[j] * inv * g_ref[:, cs]).astype(o_ref.dtype)


def _sgu(u, sv, w_sp, b_sp_t, g_out):
    n = u.shape[0]
    tm = TOK_TILE
    row = lambda i: (i, 0)
    return pl.pallas_call(
        _sgu_kernel,
        grid=(n // tm,),
        in_specs=[
            pl.BlockSpec((tm, SGU_W), row),
            pl.BlockSpec((tm, SGU_W), row),
            pl.BlockSpec(w_sp.shape, lambda i: (0, 0, 0)),
            pl.BlockSpec(b_sp_t.shape, lambda i: (0, 0)),
            pl.BlockSpec((1, SGU_W), lambda i: (0, 0)),
        ],
        out_specs=pl.BlockSpec((tm, SGU_W), row),
        out_shape=jax.ShapeDtypeStruct((n, SGU_W), BF16),
        name="sgu",
    )(u, sv, w_sp, b_sp_t, g_out)


def _outproj_kernel(a_ref, s_ref, ga_ref, wa_ref, ws_ref, x_ref, mod_ref, gpost_ref, gffn_ref,
                    *rest, with_router):
    if with_router:
        wr_ref, xo_ref, h_ref, lg_ref = rest
    else:
        xo_ref, h_ref = rest
    a = _rms(a_ref[...].astype(F32), ga_ref[...]).astype(BF16)
    y = _dot(a, wa_ref[...]) + _dot(s_ref[...], ws_ref[...])
    gt_m = mod_ref[0, 2:3, :]
    sh_f = mod_ref[0, 3:4, :]
    sc_f = mod_ref[0, 4:5, :]
    xn = x_ref[...] + (1.0 + gt_m) * _rms(y, gpost_ref[...])
    xo_ref[...] = xn
    h = _rms(xn, gffn_ref[...]) * (1.0 + sc_f) + sh_f
    h_ref[...] = h.astype(BF16)
    if with_router:
        wr = wr_ref[...]
        h_hi = h.astype(BF16)
        h_lo = (h - h_hi.astype(F32)).astype(BF16)
        w_hi = wr.astype(BF16)
        w_lo = (wr - w_hi.astype(F32)).astype(BF16)
        lg_ref[...] = _dot(h_hi, w_hi) + (_dot(h_hi, w_lo) + _dot(h_lo, w_hi))


def _outproj(attn, sgun, g_attn, w_a, w_s, x2, mod_l, g_post, g_ffn, w_router, seq):
    n, d = x2.shape
    tm = TOK_TILE
    per_b = seq // tm
    row = lambda i: (i, 0)
    const = lambda i: (0, 0)
    with_router = w_router is not None
    in_specs = [
        pl.BlockSpec((tm, ATTN_W), row),
        pl.BlockSpec((tm, SGU_W), row),
        pl.BlockSpec((1, ATTN_W), const),
        pl.BlockSpec(w_a.shape, const),
        pl.BlockSpec(w_s.shape, const),
        pl.BlockSpec((tm, d), row),
        pl.BlockSpec((1, N_MOD, d), lambda i: (i // per_b, 0, 0)),
        pl.BlockSpec((1, d), const),
        pl.BlockSpec((1, d), const),
    ]
    args = [attn, sgun, g_attn, w_a, w_s, x2, mod_l, g_post, g_ffn]
    out_specs = [pl.BlockSpec((tm, d), row), pl.BlockSpec((tm, d), row)]
    out_shape = [jax.ShapeDtypeStruct((n, d), F32), jax.ShapeDtypeStruct((n, d), BF16)]
    if with_router:
        in_specs.append(pl.BlockSpec(w_router.shape, const))
        args.append(w_router)
        out_specs.append(pl.BlockSpec((tm, LANES), row))
        out_shape.append(jax.ShapeDtypeStruct((n, LANES), F32))
    return pl.pallas_call(
        functools.partial(_outproj_kernel, with_router=with_router),
        grid=(n // tm,),
        in_specs=in_specs,
        out_specs=out_specs,
        out_shape=out_shape,
        name="outproj_router" if with_router else "outproj",
    )(*args)


def _ffn_kernel(h_ref, wgu_ref, wd_ref, x_ref, mod_ref, gpost_ref, o_ref):
    h = h_ref[...]
    y = None
    for a, b in FFN_CHUNKS:
        g = _dot(h, wgu_ref[:, a:b])
        u = _dot(h, wgu_ref[:, D_FF_DENSE + a:D_FF_DENSE + b])
        act = (g * jax.nn.sigmoid(g) * u).astype(BF16)
        part = _dot(act, wd_ref[a:b, :])
        y = part if y is None else y + part
    gt_f = mod_ref[0, 5:6, :]
    o_ref[...] = x_ref[...] + (1.0 + gt_f) * _rms(y, gpost_ref[...])


def _ffn_dense(h2, w_gu, w_down, x2, mod_l, g_post, seq):
    n, d = x2.shape
    tm = TOK_TILE
    per_b = seq // tm
    resident = dict(pipeline_mode=pl.Buffered(1))
    return pl.pallas_call(
        _ffn_kernel,
        grid=(n // tm,),
        in_specs=[
            pl.BlockSpec((tm, d), lambda i: (i, 0)),
            pl.BlockSpec(w_gu.shape, lambda i: (0, 0), **resident),
            pl.BlockSpec(w_down.shape, lambda i: (0, 0), **resident),
            pl.BlockSpec((tm, d), lambda i: (i, 0)),
            pl.BlockSpec((1, N_MOD, d), lambda i: (i // per_b, 0, 0)),
            pl.BlockSpec((1, d), lambda i: (0, 0)),
        ],
        out_specs=pl.BlockSpec((tm, d), lambda i: (i, 0)),
        out_shape=jax.ShapeDtypeStruct((n, d), F32),
        name="ffn_dense",
    )(h2, w_gu, w_down, x2, mod_l, g_post)


def _route_kernel(lg_ref, keyc_ref, gate_ref, keyt_ref, ends_ref, base_ref):
    s = pl.program_id(0)
    tm = lg_ref.shape[0]

    @pl.when(s == 0)
    def _():
        base_ref[...] = jnp.zeros_like(base_ref)

    lane = lax.broadcasted_iota(jnp.int32, (tm, LANES), 1)
    lg = jnp.where(lane < N_EXPERTS, lg_ref[...], -jnp.inf)
    m1 = jnp.max(lg, axis=-1, keepdims=True)
    i1 = jnp.min(jnp.where(lg == m1, lane, LANES), axis=-1, keepdims=True)
    sel1 = lane == i1
    lg2 = jnp.where(sel1, -jnp.inf, lg)
    m2 = jnp.max(lg2, axis=-1, keepdims=True)
    i2 = jnp.min(jnp.where(lg2 == m2, lane, LANES), axis=-1, keepdims=True)
    sel2 = lane == i2
    e2 = jnp.exp(m2 - m1)
    w1 = 1.0 / (1.0 + e2)
    w2 = e2 / (1.0 + e2)
    gate_ref[...] = jnp.where(sel1, w1, 0.0) + jnp.where(sel2, w2, 0.0)
    chosen = jnp.logical_or(sel1, sel2)
    mask = jnp.where(chosen, 1.0, 0.0)
    r = lax.broadcasted_iota(jnp.int32, (tm, tm), 0)
    c = lax.broadcasted_iota(jnp.int32, (tm, tm), 1)
    tri = jnp.where(c < r, 1.0, 0.0).astype(BF16)
    base = base_ref[0:1, :]
    rank = _dot(tri, mask.astype(BF16)) + base
    key = jnp.where(chosen, rank, UNROUTED)
    keyc_ref[...] = key
    keyt_ref[...] = key.T
    new_base = base + jnp.sum(mask, axis=0, keepdims=True)
    base_ref[...] = jnp.broadcast_to(new_base, base_ref.shape)
    ends_ref[0] = jnp.broadcast_to(new_base, ends_ref.shape[1:])


def _route(logits):
    n = logits.shape[0]
    tm = TOK_TILE
    ns = n // tm
    row = lambda s: (s, 0)
    return pl.pallas_call(
        _route_kernel,
        grid=(ns,),
        in_specs=[pl.BlockSpec((tm, LANES), row)],
        out_specs=[
            pl.BlockSpec((tm, LANES), row),
            pl.BlockSpec((tm, LANES), row),
            pl.BlockSpec((LANES, tm), lambda s: (0, s)),
            pl.BlockSpec((1, 8, LANES), lambda s: (s, 0, 0)),
        ],
        out_shape=[
            jax.ShapeDtypeStruct((n, LANES), F32),
            jax.ShapeDtypeStruct((n, LANES), F32),
            jax.ShapeDtypeStruct((LANES, n), F32),
            jax.ShapeDtypeStruct((ns, 8, LANES), F32),
        ],
        scratch_shapes=[pltpu.VMEM((8, LANES), F32)],
        name="moe_route",
    )(logits)


def _work_lists(ends, n_tokens):
    e_n = N_EXPERTS
    sub = EXP_TM // DISP_TD
    max_tiles = (2 * n_tokens) // EXP_TM + e_n
    i32 = jnp.int32

    ends_i = ends[:, 0, :e_n].astype(i32)
    base = jnp.concatenate([jnp.zeros((1, e_n), i32), ends_i], axis=0)
    cnt = base[-1]
    ntile = (cnt + EXP_TM - 1) // EXP_TM
    tile_end = jnp.cumsum(ntile)
    tile_start = tile_end - ntile
    total_tiles = tile_end[-1]
    npass = (cnt + EXP_SUB - 1) // EXP_SUB
    off = tile_start * EXP_TM

    t = jnp.arange(max_tiles, dtype=i32)
    tc = jnp.minimum(t, total_tiles - 1)
    tile_e = jnp.sum((tile_end[None, :] <= tc[:, None]).astype(i32), axis=1)
    tile_pass = jnp.clip(npass[tile_e] - (tc - tile_start[tile_e]) * (EXP_TM // EXP_SUB),
                         1, EXP_TM // EXP_SUB)
    tile_pass = jnp.where(t < total_tiles, tile_pass, 0)

    j = jnp.arange(max_tiles * sub, dtype=i32)
    jt = jnp.minimum(j // sub, total_tiles - 1)
    je = tile_e[jt]
    jm = (jt - tile_start[je]) * sub + j % sub
    lo_rank = jm * DISP_TD
    hi_rank = jnp.minimum(lo_rank + DISP_TD, cnt[je])
    base_e = base.T[je]
    s_lo = jnp.sum((base_e[:, 1:] <= lo_rank[:, None]).astype(i32), axis=1)
    s_hi = jnp.sum((base_e[:, :-1] < hi_rank[:, None]).astype(i32), axis=1)
    n_src = jnp.where(j // sub < total_tiles, jnp.maximum(s_hi - s_lo, 0), 0)

    rows = max_tiles * EXP_TM
    start = off[None, :] + base[:-1]
    stop = off[None, :] + base[1:]
    wstart = jnp.minimum(start // COMB_ALIGN * COMB_ALIGN, rows - COMB_WIN)
    need = jnp.where(stop > start, stop - wstart, 0)
    rel = wstart - off[None, :]
    return dict(
        disp=(je, jm, s_lo, n_src),
        comb=((wstart // COMB_ALIGN).reshape(-1), rel.reshape(-1), need.reshape(-1)),
        tiles=(tc, tile_e, tile_pass),
        max_tiles=max_tiles)


def _dispatch_kernel(e_ref, m_ref, slo_ref, n_ref, h_ref, keyt_ref, o_ref, acc_ref):
    j = pl.program_id(0)
    td = o_ref.shape[0]
    acc_ref[...] = jnp.zeros_like(acc_ref)
    e = e_ref[j]
    slot = (m_ref[j] * td + lax.broadcasted_iota(jnp.int32, (td, TOK_TILE), 0)).astype(F32)

    def body(i, carry):
        ts = pl.multiple_of((slo_ref[j] + i) * TOK_TILE, TOK_TILE)
        key = keyt_ref[pl.ds(e, 1), pl.ds(ts, TOK_TILE)]
        onehot = jnp.where(key == slot, 1.0, 0.0).astype(BF16)
        acc_ref[...] += _dot(onehot, h_ref[pl.ds(ts, TOK_TILE), :])
        return carry

    lax.fori_loop(0, n_ref[j], body, 0)
    o_ref[...] = acc_ref[...].astype(o_ref.dtype)


def _dispatch(h2, key_t, lists, max_tiles):
    n, d = h2.shape
    rows = max_tiles * EXP_TM
    resident = dict(pipeline_mode=pl.Buffered(1))
    grid_spec = pltpu.PrefetchScalarGridSpec(
        num_scalar_prefetch=4,
        grid=(rows // DISP_TD,),
        in_specs=[
            pl.BlockSpec((n, d), lambda j, *_: (0, 0), **resident),
            pl.BlockSpec((8, n), lambda j, *_: (0, 0), **resident),
        ],
        out_specs=pl.BlockSpec((DISP_TD, d), lambda j, *_: (j, 0)),
        scratch_shapes=[pltpu.VMEM((DISP_TD, d), F32)],
    )
    return pl.pallas_call(
        _dispatch_kernel,
        grid_spec=grid_spec,
        out_shape=jax.ShapeDtypeStruct((rows, d), BF16),
        name="moe_dispatch",
    )(*lists, h2, key_t)


def _expert_kernel(tidx_ref, te_ref, tp_ref, x_ref, wg_ref, wu_ref, wd_ref, y_ref, acc_ref):
    i = pl.program_id(0)
    f = pl.program_id(1)
    nf = pl.num_programs(1)

    def one_pass(r, carry):
        rs = pl.ds(pl.multiple_of(r * EXP_SUB, EXP_SUB), EXP_SUB)
        x = x_ref[rs, :]
        g = _dot(x, wg_ref[0])
        u = _dot(x, wu_ref[0])
        act = (g * jax.nn.sigmoid(g) * u).astype(BF16)
        part = _dot(act, wd_ref[0])

        @pl.when(f == 0)
        def _():
            acc_ref[rs, :] = part

        @pl.when(f > 0)
        def _():
            acc_ref[rs, :] += part

        @pl.when(f == nf - 1)
        def _():
            y_ref[rs, :] = acc_ref[rs, :].astype(y_ref.dtype)

        return carry

    lax.fori_loop(0, tp_ref[i], one_pass, 0)

    @pl.when(f == nf - 1)
    def _():
        for r in range(EXP_TM // EXP_SUB):
            @pl.when(r >= tp_ref[i])
            def _(r=r):
                y_ref[r * EXP_SUB:(r + 1) * EXP_SUB, :] = jnp.zeros((EXP_SUB, y_ref.shape[1]),
                                                                    y_ref.dtype)


def _experts(xs, w_gu, w_down, tiles, max_tiles):
    rows, d = xs.shape
    tm, tf = EXP_TM, EXP_TF
    nf = D_FF_EXPERT // tf

    def fsel(i, f, tp):
        return jnp.where(tp[i] > 0, f, nf - 1)

    grid_spec = pltpu.PrefetchScalarGridSpec(
        num_scalar_prefetch=3,
        grid=(max_tiles, nf),
        in_specs=[
            pl.BlockSpec((tm, d), lambda i, f, ti, te, tv: (ti[i], 0)),
            pl.BlockSpec((1, d, tf), lambda i, f, ti, te, tv: (te[i], 0, fsel(i, f, tv))),
            pl.BlockSpec((1, d, tf), lambda i, f, ti, te, tv: (te[i], 0, nf + fsel(i, f, tv))),
            pl.BlockSpec((1, tf, d), lambda i, f, ti, te, tv: (te[i], fsel(i, f, tv), 0)),
        ],
        out_specs=pl.BlockSpec((tm, d), lambda i, f, ti, te, tv: (i, 0)),
        scratch_shapes=[pltpu.VMEM((tm, d), F32)],
    )
    return pl.pallas_call(
        _expert_kernel,
        grid_spec=grid_spec,
        out_shape=jax.ShapeDtypeStruct((rows, d), BF16),
        name="moe_experts",
    )(*tiles, xs, w_gu, w_gu, w_down)


def _combine_kernel(ws_ref, rel_ref, need_ref, *refs):
    y_refs = refs[:N_EXPERTS]
    keyc_ref, gate_ref, x_ref, mod_ref, gpost_ref, o_ref, acc_ref = refs[N_EXPERTS:]
    s = pl.program_id(0)
    ts = acc_ref.shape[0]

    def term(e, a, b):
        key = keyc_ref[:, e:e + 1]
        rel = rel_ref[s * N_EXPERTS + e]
        slot = (rel + a + lax.broadcasted_iota(jnp.int32, (ts, b - a), 1)).astype(F32)
        onehot = jnp.where(key == slot, 1.0, 0.0).astype(BF16)
        return gate_ref[:, e:e + 1] * _dot(onehot, y_refs[e][a:b, :])

    a0, b0 = COMB_CHUNKS[0]
    acc = term(0, a0, b0)
    for e in range(1, N_EXPERTS):
        acc = acc + term(e, a0, b0)
    acc_ref[...] = acc
    for e in range(N_EXPERTS):
        for a, b in COMB_CHUNKS[1:]:
            @pl.when(need_ref[s * N_EXPERTS + e] > a)
            def _(e=e, a=a, b=b):
                acc_ref[...] += term(e, a, b)
    gt_f = mod_ref[0, 5:6, :]
    o_ref[...] = x_ref[...] + (1.0 + gt_f) * _rms(acc_ref[...], gpost_ref[...])


def _combine(y, key_c, gates, x2, mod_l, g_post, lists, seq):
    n, d = x2.shape
    per_b = seq // TOK_TILE
    tok = lambda s, *_: (s, 0)

    def window(e):
        return pl.BlockSpec((pl.Element(COMB_WIN), pl.Element(d)),
                            lambda s, ws, rel, need: (ws[s * N_EXPERTS + e] * COMB_ALIGN, 0))

    grid_spec = pltpu.PrefetchScalarGridSpec(
        num_scalar_prefetch=3,
        grid=(n // TOK_TILE,),
        in_specs=[window(e) for e in range(N_EXPERTS)] + [
            pl.BlockSpec((TOK_TILE, LANES), tok),
            pl.BlockSpec((TOK_TILE, LANES), tok),
            pl.BlockSpec((TOK_TILE, d), tok),
            pl.BlockSpec((1, N_MOD, d), lambda s, *_: (s // per_b, 0, 0)),
            pl.BlockSpec((1, d), lambda s, *_: (0, 0)),
        ],
        out_specs=pl.BlockSpec((TOK_TILE, d), tok),
        scratch_shapes=[pltpu.VMEM((TOK_TILE, d), F32)],
    )
    return pl.pallas_call(
        _combine_kernel,
        grid_spec=grid_spec,
        out_shape=jax.ShapeDtypeStruct((n, d), F32),
        name="moe_combine",
    )(*lists, *([y] * N_EXPERTS), key_c, gates, x2, mod_l, g_post)


def kernel(x, c, w_ada, b_ada, g_pre_mix, g_post_mix, g_pre_ffn, g_post_ffn, w_in, b_forget, g_v,
           w_spatial, b_spatial, g_out_attn, g_out_sgu, w_out, w_gate_up_dense, w_down_dense,
           w_router, w_gate_up_exp, w_down_exp):
    bsz, seq, d = x.shape
    depth = w_ada.shape[0]
    n = bsz * seq
    npair = N_HEADS // 2
    assert d == D_MODEL and seq % TOK_TILE == 0 and seq % ATT_TQ == 0

    mod = _modulation(c, w_ada, b_ada).reshape(depth, bsz, N_MOD, d)
    x2 = x.reshape(n, d)
    o3 = 3 * ATTN_W
    o4 = o3 + N_HEADS
    for l in range(depth):
        mod_l = mod[l]
        w_l = w_in[l]
        w_qkv = w_l[:, :o3].astype(BF16)
        w_f = jnp.pad(w_l[:, o3:o4], ((0, 0), (0, LANES - N_HEADS))).astype(BF16)
        w_uv = w_l[:, o4:].astype(BF16)
        q, k, v, zf_t, u, sv = _inproj(x2, mod_l, g_pre_mix[l].reshape(1, d), w_qkv, w_f, w_uv,
                                       g_v[l].reshape(1, SGU_W), seq)
        lcum = _forget_cumsum(zf_t, b_forget[l], seq)
        if l % 2 == 1:
            w_slab = w_gate_up_exp[l // 2].reshape(N_EXPERTS * d, 2 * D_FF_EXPERT)
        elif l + 1 < depth:
            w_slab = w_down_exp[l // 2].reshape(N_EXPERTS * D_FF_EXPERT, d)
        else:
            w_slab = None
        attn, w_cast = _attention(q, k, v, lcum.reshape(npair, 2, n), bsz, seq, w_slab)
        if l % 2 == 1:
            w_gu_bf16 = w_cast.reshape(N_EXPERTS, d, 2 * D_FF_EXPERT)
        elif w_cast is not None:
            w_down_bf16 = w_cast.reshape(N_EXPERTS, D_FF_EXPERT, d)
        sgun = _sgu(u, sv, w_spatial[l].astype(BF16), b_spatial[l].T,
                    g_out_sgu[l].reshape(1, SGU_W))
        w_o = w_out[l].astype(BF16)
        moe = (l % 2 == 1)
        w_r = None
        if moe:
            w_r = jnp.pad(w_router[l // 2], ((0, 0), (0, LANES - N_EXPERTS)))
        res = _outproj(attn, sgun, g_out_attn[l].reshape(1, ATTN_W), w_o[:ATTN_W], w_o[ATTN_W:],
                       x2, mod_l, g_post_mix[l].reshape(1, d), g_pre_ffn[l].reshape(1, d), w_r, seq)
        g_post = g_post_ffn[l].reshape(1, d)
        if not moe:
            x2, h2 = res
            x2 = _ffn_dense(h2, w_gate_up_dense[l // 2].astype(BF16),
                            w_down_dense[l // 2].astype(BF16), x2, mod_l, g_post, seq)
        else:
            x2, h2, logits = res
            key_c, gates, key_t, ends = _route(logits)
            wl = _work_lists(ends, n)
            xs = _dispatch(h2, key_t, wl["disp"], wl["max_tiles"])
            y = _experts(xs, w_gu_bf16, w_down_bf16, wl["tiles"], wl["max_tiles"])
            x2 = _combine(y, key_c, gates, x2, mod_l, g_post, wl["comb"], seq)
    return x2.reshape(bsz, seq, d)
```

```python
import functools
import math

import jax
import jax.numpy as jnp
from jax import lax
from jax.experimental import pallas as pl
from jax.experimental.pallas import tpu as pltpu

F32 = jnp.float32
BF16 = jnp.bfloat16

D_MODEL = 1024
HEAD_DIM = 64
N_HEADS = 8
N_GROUPS = 8
ATTN_W = N_HEADS * HEAD_DIM
SGU_W = N_GROUPS * HEAD_DIM
CHUNK = 128
D_FF_DENSE = 2816
N_EXPERTS = 8
D_FF_EXPERT = 3584
N_MOD = 6
EPS = 1e-6

LANES = 128
TOK_TILE = 512
ROW_GROUP = 256
ATT_TQ = 256
MXU_TILE = 256
FFN_CHUNKS = ((0, 6 * MXU_TILE), (6 * MXU_TILE, D_FF_DENSE))
EXP_TM = 1024
EXP_SUB = 512
EXP_TF = 7 * MXU_TILE
LOG2E = math.log2(math.e)
DISP_TD = 256
COMB_ALIGN = 16
COMB_WIN = TOK_TILE + COMB_ALIGN
COMB_CHUNKS = ((0, MXU_TILE), (MXU_TILE, COMB_WIN))
UNROUTED = -float(2 ** 30)
GELU_C = math.sqrt(2.0 / math.pi)


def _rms(x, g):
    return x * lax.rsqrt(jnp.mean(x * x, axis=-1, keepdims=True) + EPS) * g


def _dot(a, b):
    return jnp.dot(a, b, preferred_element_type=F32)


def _row_groups(rows):
    return [slice(r, r + ROW_GROUP) for r in range(0, rows, ROW_GROUP)]


def _mod_kernel(c_ref, w_ref, b_ref, o_ref):
    c = c_ref[...]
    ca = (c * jax.nn.sigmoid(c)).astype(BF16)
    o_ref[0] = _dot(ca, w_ref[0].astype(BF16)) + b_ref[0]


def _modulation(c, w_ada, b_ada):
    depth, d, m = w_ada.shape
    bsz = c.shape[0]
    nblk = m // d
    return pl.pallas_call(
        _mod_kernel,
        grid=(depth, nblk),
        in_specs=[
            pl.BlockSpec((bsz, d), lambda l, j: (0, 0)),
            pl.BlockSpec((1, d, d), lambda l, j: (l, 0, j)),
            pl.BlockSpec((1, 1, d), lambda l, j: (l, 0, j)),
        ],
        out_specs=pl.BlockSpec((1, bsz, d), lambda l, j: (l, 0, j)),
        out_shape=jax.ShapeDtypeStruct((depth, bsz, m), F32),
        name="adaln_mod",
    )(c, w_ada, b_ada.reshape(depth, 1, m))


def _inproj_kernel(x_ref, mod_ref, gpre_ref, wqkv_ref, wf_ref, wuv_ref, gv_ref,
                   q_ref, k_ref, v_ref, zf_ref, u_ref, sv_ref):
    sh = mod_ref[0, 0:1, :]
    sc = mod_ref[0, 1:2, :]
    for rs in _row_groups(x_ref.shape[0]):
        h = (_rms(x_ref[rs, :], gpre_ref[...]) * (1.0 + sc) + sh).astype(BF16)
        qkv = _dot(h, wqkv_ref[...])
        q_ref[rs, :] = (qkv[:, :ATTN_W] * (HEAD_DIM ** -0.5 * LOG2E)).astype(BF16)
        k_ref[rs, :] = qkv[:, ATTN_W:2 * ATTN_W].astype(BF16)
        v_ref[rs, :] = qkv[:, 2 * ATTN_W:].astype(BF16)
        zf_ref[:, rs] = _dot(h, wf_ref[...]).T[:N_HEADS, :]
        uv = _dot(h, wuv_ref[...])
        uv = uv * (0.5 * (1.0 + jnp.tanh(GELU_C * (uv + 0.044715 * (uv * uv * uv)))))
        u_ref[rs, :] = uv[:, :SGU_W].astype(BF16)
        s = uv[:, SGU_W:]
        sc_ = s - jnp.mean(s, axis=-1, keepdims=True)
        sv = sc_ * lax.rsqrt(jnp.mean(sc_ * sc_, axis=-1, keepdims=True) + EPS) * gv_ref[...]
        sv_ref[rs, :] = sv.astype(BF16)


def _inproj(x2, mod_l, g_pre, w_qkv, w_f, w_uv, g_v, seq):
    n, d = x2.shape
    tm = TOK_TILE
    per_b = seq // tm
    row = lambda i: (i, 0)
    const = lambda i: (0, 0)
    outs = [jax.ShapeDtypeStruct((n, ATTN_W), BF16)] * 3 + [
        jax.ShapeDtypeStruct((N_HEADS, n), F32),
        jax.ShapeDtypeStruct((n, SGU_W), BF16),
        jax.ShapeDtypeStruct((n, SGU_W), BF16)]
    return pl.pallas_call(
        _inproj_kernel,
        grid=(n // tm,),
        in_specs=[
            pl.BlockSpec((tm, d), row),
            pl.BlockSpec((1, N_MOD, d), lambda i: (i // per_b, 0, 0)),
            pl.BlockSpec((1, d), const),
            pl.BlockSpec(w_qkv.shape, const),
            pl.BlockSpec(w_f.shape, const),
            pl.BlockSpec(w_uv.shape, const),
            pl.BlockSpec((1, SGU_W), const),
        ],
        out_specs=[pl.BlockSpec((tm, ATTN_W), row)] * 3 + [
            pl.BlockSpec((N_HEADS, tm), lambda i: (0, i)),
            pl.BlockSpec((tm, SGU_W), row),
            pl.BlockSpec((tm, SGU_W), row)],
        out_shape=outs,
        name="inproj",
    )(x2, mod_l, g_pre, w_qkv, w_f, w_uv, g_v)


def _gates_kernel(z_ref, b_ref, o_ref):
    z = z_ref[...] + b_ref[...]
    ls = jnp.minimum(z, 0.0) - jnp.log1p(jnp.exp(-jnp.abs(z)))
    seq = ls.shape[-1]
    pos = lax.broadcasted_iota(jnp.int32, ls.shape, 1)
    shift = 1
    while shift < seq:
        ls = ls + jnp.where(pos >= shift, pltpu.roll(ls, shift, 1), 0.0)
        shift *= 2
    o_ref[...] = ls * LOG2E


def _forget_cumsum(zf_t, b_forget, seq):
    h, n = zf_t.shape
    return pl.pallas_call(
        _gates_kernel,
        grid=(n // seq,),
        in_specs=[pl.BlockSpec((h, seq), lambda b: (0, b)),
                  pl.BlockSpec((h, 1), lambda b: (0, 0))],
        out_specs=pl.BlockSpec((h, seq), lambda b: (0, b)),
        out_shape=jax.ShapeDtypeStruct((h, n), F32),
        name="forget_cumsum",
    )(zf_t, b_forget.reshape(h, 1))


def _attn_kernel(q_ref, k_ref, v_ref, lk_ref, *rest):
    if len(rest) == 3:
        wsrc_ref, o_ref, wdst_ref = rest
        wdst_ref[...] = wsrc_ref[...].astype(wdst_ref.dtype)
    else:
        (o_ref,) = rest
    tq = ATT_TQ
    seq = q_ref.shape[0]
    nq = seq // tq
    l_rows = jnp.concatenate([lk_ref[0], jnp.zeros((LANES - 2, seq), F32)], axis=0)
    l_cols = l_rows.T
    lane = lax.broadcasted_iota(jnp.int32, (tq, LANES), 1)
    left = lane < HEAD_DIM
    rows = lax.broadcasted_iota(jnp.int32, (tq, tq), 0)
    cols = lax.broadcasted_iota(jnp.int32, (tq, tq), 1)
    causal = cols <= rows
    zero = jnp.zeros((tq, LANES), BF16)
    one = jnp.ones((tq, LANES), BF16)
    def scores(qi):
        qs = slice(qi * tq, (qi + 1) * tq)
        q2 = q_ref[qs, :]
        qq = jnp.concatenate([jnp.where(left, q2, zero), jnp.where(left, zero, q2)], axis=0)
        t_blocks = [[], []]
        mt = [None, None]
        for j in range(qi + 1):
            ks = slice(j * tq, (j + 1) * tq)
            t2 = lax.dot_general(qq, k_ref[ks, :], (((1,), (1,)), ((), ())),
                                 preferred_element_type=F32)
            for hh in range(2):
                t = t2[hh * tq:(hh + 1) * tq] - lk_ref[0, hh:hh + 1, ks]
                if j == qi:
                    t = jnp.where(causal, t, -jnp.inf)
                t_blocks[hh].append(t)
                mj = jnp.maximum(t[:, :LANES], t[:, LANES:])
                mt[hh] = mj if mt[hh] is None else jnp.maximum(mt[hh], mj)
        shift = []
        for hh in range(2):
            lq = l_cols[qs, hh:hh + 1]
            row_max = jnp.max(mt[hh], axis=-1, keepdims=True) + lq
            shift.append(row_max - lq)
        return t_blocks, shift

    def values(qi, t_blocks, shift):
        qs = slice(qi * tq, (qi + 1) * tq)
        accs = [None, None]
        for j in range(qi + 1):
            ks = slice(j * tq, (j + 1) * tq)
            pp = jnp.concatenate([jnp.exp2(t_blocks[hh][j] - shift[hh]) for hh in range(2)],
                                 axis=0).astype(BF16)
            vaug = jnp.concatenate([v_ref[ks, :], one], axis=1)
            part = _dot(pp, vaug)
            accs[j % 2] = part if accs[j % 2] is None else accs[j % 2] + part
        acc = accs[0] if accs[1] is None else accs[0] + accs[1]
        out0 = acc[:tq, :LANES] / acc[:tq, LANES:]
        out1 = acc[tq:, :LANES] / acc[tq:, LANES:]
        o_ref[qs, :] = jnp.where(left, out0, out1).astype(o_ref.dtype)

    pending = scores(0)
    for qi in range(nq):
        nxt = scores(qi + 1) if qi + 1 < nq else None
        values(qi, *pending)
        pending = nxt


def _attention(q, k, v, l_pairs, bsz, seq, w_f32=None):
    n = q.shape[0]
    npair = N_HEADS // 2
    blk = pl.BlockSpec((seq, LANES), lambda b, p: (b, p))
    in_specs = [blk, blk, blk, pl.BlockSpec((1, 2, seq), lambda b, p: (p, 0, b))]
    out_specs = [blk]
    out_shape = [jax.ShapeDtypeStruct((n, ATTN_W), BF16)]
    args = [q, k, v, l_pairs]
    if w_f32 is not None:
        rows, cols = w_f32.shape
        rb = rows // (bsz * npair)
        assert rb * bsz * npair == rows and rb % COMB_ALIGN == 0
        wblk = pl.BlockSpec((rb, cols), lambda b, p: (b * npair + p, 0))
        in_specs.append(wblk)
        out_specs.append(wblk)
        out_shape.append(jax.ShapeDtypeStruct((rows, cols), BF16))
        args.append(w_f32)
    res = pl.pallas_call(
        _attn_kernel,
        grid=(bsz, npair),
        in_specs=in_specs,
        out_specs=out_specs,
        out_shape=out_shape,
        name="fox_attention",
    )(*args)
    return res if w_f32 is not None else (res[0], None)


def _sgu_kernel(u_ref, sv_ref, w_ref, bt_ref, g_ref, o_ref):
    tm = u_ref.shape[0]
    npair = N_GROUPS // 2
    lane = lax.broadcasted_iota(jnp.int32, (CHUNK, LANES), 1)
    left = lane < HEAD_DIM
    r = lax.broadcasted_iota(jnp.int32, (CHUNK, CHUNK), 0)
    c = lax.broadcasted_iota(jnp.int32, (CHUNK, CHUNK), 1)
    causal = c <= r
    lhs, bias = [], []
    for j in range(npair):
        wa = jnp.where(causal, w_ref[2 * j], jnp.zeros((), BF16))
        wb = jnp.where(causal, w_ref[2 * j + 1], jnp.zeros((), BF16))
        lhs.append(jnp.concatenate([wa, wb], axis=1))
        bias.append(jnp.where(left, bt_ref[:, 2 * j:2 * j + 1], bt_ref[:, 2 * j + 1:2 * j + 2]))
    for ci in range(tm // CHUNK):
        rs = slice(ci * CHUNK, (ci + 1) * CHUNK)
        blks = []
        ssq = jnp.zeros((CHUNK, 1), F32)
        for j in range(npair):
            cs = slice(j * LANES, (j + 1) * LANES)
            svb = sv_ref[rs, cs]
            zero = jnp.zeros_like(svb)
            rhs = jnp.concatenate([jnp.where(left, svb, zero), jnp.where(left, zero, svb)], axis=0)
            mixed = _dot(lhs[j], rhs) + bias[j]
            ob = u_ref[rs, cs].astype(F32) * mixed
            ssq = ssq + jnp.sum(ob * ob, axis=-1, keepdims=True)
            blks.append(ob)
        inv = lax.rsqrt(ssq * (1.0 / SGU_W) + EPS)
        for j in range(npair):
            cs = slice(j * LANES, (j + 1) * LANES)
            o_ref[rs, cs] = (blks[j] * inv * g_ref[:, cs]).astype(o_ref.dtype)


def _sgu(u, sv, w_sp, b_sp_t, g_out):
    n = u.shape[0]
    tm = TOK_TILE
    row = lambda i: (i, 0)
    return pl.pallas_call(
        _sgu_kernel,
        grid=(n // tm,),
        in_specs=[
            pl.BlockSpec((tm, SGU_W), row),
            pl.BlockSpec((tm, SGU_W), row),
            pl.BlockSpec(w_sp.shape, lambda i: (0, 0, 0)),
            pl.BlockSpec(b_sp_t.shape, lambda i: (0, 0)),
            pl.BlockSpec((1, SGU_W), lambda i: (0, 0)),
        ],
        out_specs=pl.BlockSpec((tm, SGU_W), row),
        out_shape=jax.ShapeDtypeStruct((n, SGU_W), BF16),
        name="sgu",
    )(u, sv, w_sp, b_sp_t, g_out)


def _outproj_kernel(a_ref, s_ref, ga_ref, wa_ref, ws_ref, x_ref, mod_ref, gpost_ref, gffn_ref,
                    *rest, with_router):
    if with_router:
        wr_ref, xo_ref, h_ref, lg_ref = rest
    else:
        xo_ref, h_ref = rest
    gt_m = mod_ref[0, 2:3, :]
    sh_f = mod_ref[0, 3:4, :]
    sc_f = mod_ref[0, 4:5, :]
    if with_router:
        wr = wr_ref[...]
        w_hi = wr.astype(BF16)
        w_lo = (wr - w_hi.astype(F32)).astype(BF16)
    for rs in _row_groups(x_ref.shape[0]):
        a = _rms(a_ref[rs, :].astype(F32), ga_ref[...]).astype(BF16)
        y = _dot(a, wa_ref[...]) + _dot(s_ref[rs, :], ws_ref[...])
        xn = x_ref[rs, :] + (1.0 + gt_m) * _rms(y, gpost_ref[...])
        xo_ref[rs, :] = xn
        h = _rms(xn, gffn_ref[...]) * (1.0 + sc_f) + sh_f
        h_hi = h.astype(BF16)
        h_ref[rs, :] = h_hi
        if with_router:
            h_lo = (h - h_hi.astype(F32)).astype(BF16)
            lg_ref[rs, :] = _dot(h_hi, w_hi) + (_dot(h_hi, w_lo) + _dot(h_lo, w_hi))


def _outproj(attn, sgun, g_attn, w_a, w_s, x2, mod_l, g_post, g_ffn, w_router, seq):
    n, d = x2.shape
    tm = TOK_TILE
    per_b = seq // tm
    row = lambda i: (i, 0)
    const = lambda i: (0, 0)
    with_router = w_router is not None
    in_specs = [
        pl.BlockSpec((tm, ATTN_W), row),
        pl.BlockSpec((tm, SGU_W), row),
        pl.BlockSpec((1, ATTN_W), const),
        pl.BlockSpec(w_a.shape, const),
        pl.BlockSpec(w_s.shape, const),
        pl.BlockSpec((tm, d), row),
        pl.BlockSpec((1, N_MOD, d), lambda i: (i // per_b, 0, 0)),
        pl.BlockSpec((1, d), const),
        pl.BlockSpec((1, d), const),
    ]
    args = [attn, sgun, g_attn, w_a, w_s, x2, mod_l, g_post, g_ffn]
    out_specs = [pl.BlockSpec((tm, d), row), pl.BlockSpec((tm, d), row)]
    out_shape = [jax.ShapeDtypeStruct((n, d), F32), jax.ShapeDtypeStruct((n, d), BF16)]
    if with_router:
        in_specs.append(pl.BlockSpec(w_router.shape, const))
        args.append(w_router)
        out_specs.append(pl.BlockSpec((tm, LANES), row))
        out_shape.append(jax.ShapeDtypeStruct((n, LANES), F32))
    return pl.pallas_call(
        functools.partial(_outproj_kernel, with_router=with_router),
        grid=(n // tm,),
        in_specs=in_specs,
        out_specs=out_specs,
        out_shape=out_shape,
        name="outproj_router" if with_router else "outproj",
    )(*args)


def _ffn_kernel(h_ref, wgu_ref, wd_ref, x_ref, mod_ref, gpost_ref, o_ref):
    h = h_ref[...]
    y = None
    for a, b in FFN_CHUNKS:
        g = _dot(h, wgu_ref[:, a:b])
        u = _dot(h, wgu_ref[:, D_FF_DENSE + a:D_FF_DENSE + b])
        act = (g * jax.nn.sigmoid(g) * u).astype(BF16)
        part = _dot(act, wd_ref[a:b, :])
        y = part if y is None else y + part
    gt_f = mod_ref[0, 5:6, :]
    o_ref[...] = x_ref[...] + (1.0 + gt_f) * _rms(y, gpost_ref[...])


def _ffn_dense(h2, w_gu, w_down, x2, mod_l, g_post, seq):
    n, d = x2.shape
    tm = TOK_TILE
    per_b = seq // tm
    resident = dict(pipeline_mode=pl.Buffered(1))
    return pl.pallas_call(
        _ffn_kernel,
        grid=(n // tm,),
        in_specs=[
            pl.BlockSpec((tm, d), lambda i: (i, 0)),
            pl.BlockSpec(w_gu.shape, lambda i: (0, 0), **resident),
            pl.BlockSpec(w_down.shape, lambda i: (0, 0), **resident),
            pl.BlockSpec((tm, d), lambda i: (i, 0)),
            pl.BlockSpec((1, N_MOD, d), lambda i: (i // per_b, 0, 0)),
            pl.BlockSpec((1, d), lambda i: (0, 0)),
        ],
        out_specs=pl.BlockSpec((tm, d), lambda i: (i, 0)),
        out_shape=jax.ShapeDtypeStruct((n, d), F32),
        name="ffn_dense",
    )(h2, w_gu, w_down, x2, mod_l, g_post)


def _route_kernel(lg_ref, keyc_ref, gate_ref, keyt_ref, ends_ref, base_ref):
    s = pl.program_id(0)
    tm = lg_ref.shape[0]

    @pl.when(s == 0)
    def _():
        base_ref[...] = jnp.zeros_like(base_ref)

    lane = lax.broadcasted_iota(jnp.int32, (tm, LANES), 1)
    lg = jnp.where(lane < N_EXPERTS, lg_ref[...], -jnp.inf)
    m1 = jnp.max(lg, axis=-1, keepdims=True)
    i1 = jnp.min(jnp.where(lg == m1, lane, LANES), axis=-1, keepdims=True)
    sel1 = lane == i1
    lg2 = jnp.where(sel1, -jnp.inf, lg)
    m2 = jnp.max(lg2, axis=-1, keepdims=True)
    i2 = jnp.min(jnp.where(lg2 == m2, lane, LANES), axis=-1, keepdims=True)
    sel2 = lane == i2
    e2 = jnp.exp(m2 - m1)
    w1 = 1.0 / (1.0 + e2)
    w2 = e2 / (1.0 + e2)
    gate_ref[...] = jnp.where(sel1, w1, 0.0) + jnp.where(sel2, w2, 0.0)
    chosen = jnp.logical_or(sel1, sel2)
    mask = jnp.where(chosen, 1.0, 0.0)
    r = lax.broadcasted_iota(jnp.int32, (tm, tm), 0)
    c = lax.broadcasted_iota(jnp.int32, (tm, tm), 1)
    tri = jnp.where(c < r, 1.0, 0.0).astype(BF16)
    base = base_ref[0:1, :]
    rank = _dot(tri, mask.astype(BF16)) + base
    key = jnp.where(chosen, rank, UNROUTED)
    keyc_ref[...] = key
    keyt_ref[...] = key.T
    new_base = base + jnp.sum(mask, axis=0, keepdims=True)
    base_ref[...] = jnp.broadcast_to(new_base, base_ref.shape)
    ends_ref[0] = jnp.broadcast_to(new_base, ends_ref.shape[1:])


def _route(logits):
    n = logits.shape[0]
    tm = TOK_TILE
    ns = n // tm
    row = lambda s: (s, 0)
    return pl.pallas_call(
        _route_kernel,
        grid=(ns,),
        in_specs=[pl.BlockSpec((tm, LANES), row)],
        out_specs=[
            pl.BlockSpec((tm, LANES), row),
            pl.BlockSpec((tm, LANES), row),
            pl.BlockSpec((LANES, tm), lambda s: (0, s)),
            pl.BlockSpec((1, 8, LANES), lambda s: (s, 0, 0)),
        ],
        out_shape=[
            jax.ShapeDtypeStruct((n, LANES), F32),
            jax.ShapeDtypeStruct((n, LANES), F32),
            jax.ShapeDtypeStruct((LANES, n), F32),
            jax.ShapeDtypeStruct((ns, 8, LANES), F32),
        ],
        scratch_shapes=[pltpu.VMEM((8, LANES), F32)],
        name="moe_route",
    )(logits)


def _work_lists(ends, n_tokens):
    e_n = N_EXPERTS
    sub = EXP_TM // DISP_TD
    max_tiles = (2 * n_tokens) // EXP_TM + e_n
    i32 = jnp.int32

    ends_i = ends[:, 0, :e_n].astype(i32)
    base = jnp.concatenate([jnp.zeros((1, e_n), i32), ends_i], axis=0)
    cnt = base[-1]
    ntile = (cnt + EXP_TM - 1) // EXP_TM
    tile_end = jnp.cumsum(ntile)
    tile_start = tile_end - ntile
    total_tiles = tile_end[-1]
    npass = (cnt + EXP_SUB - 1) // EXP_SUB
    off = tile_start * EXP_TM

    e_ids = jnp.arange(e_n, dtype=i32)

    def expert_of(tile):
        e = jnp.sum((tile_end[None, :] <= tile[:, None]).astype(i32), axis=1)
        return e, (e[:, None] == e_ids[None, :]).astype(i32)

    def pick(onehot, per_expert):
        return jnp.sum(onehot * per_expert[None, :], axis=1)

    t = jnp.arange(max_tiles, dtype=i32)
    tc = jnp.minimum(t, total_tiles - 1)
    tile_e, oh_t = expert_of(tc)
    tile_pass = jnp.clip(pick(oh_t, npass) - (tc - pick(oh_t, tile_start)) * (EXP_TM // EXP_SUB),
                         1, EXP_TM // EXP_SUB)
    tile_pass = jnp.where(t < total_tiles, tile_pass, 0)

    j = jnp.arange(max_tiles * sub, dtype=i32)
    jt = jnp.minimum(j // sub, total_tiles - 1)
    je, oh_j = expert_of(jt)
    jm = (jt - pick(oh_j, tile_start)) * sub + j % sub
    lo_rank = jm * DISP_TD
    hi_rank = jnp.minimum(lo_rank + DISP_TD, pick(oh_j, cnt))
    base_e = jnp.sum(oh_j[:, :, None] * base.T[None, :, :], axis=1)
    s_lo = jnp.sum((base_e[:, 1:] <= lo_rank[:, None]).astype(i32), axis=1)
    s_hi = jnp.sum((base_e[:, :-1] < hi_rank[:, None]).astype(i32), axis=1)
    n_src = jnp.where(j // sub < total_tiles, jnp.maximum(s_hi - s_lo, 0), 0)

    rows = max_tiles * EXP_TM
    start = off[None, :] + base[:-1]
    stop = off[None, :] + base[1:]
    wstart = jnp.minimum(start // COMB_ALIGN * COMB_ALIGN, rows - COMB_WIN)
    need = jnp.where(stop > start, stop - wstart, 0)
    rel = wstart - off[None, :]
    return dict(
        disp=(je, jm, s_lo, n_src),
        comb=((wstart // COMB_ALIGN).reshape(-1), rel.reshape(-1), need.reshape(-1)),
        tiles=(tc, tile_e, tile_pass),
        max_tiles=max_tiles)


def _dispatch_kernel(e_ref, m_ref, slo_ref, n_ref, h_ref, keyt_ref, o_ref, acc_ref):
    j = pl.program_id(0)
    td = o_ref.shape[0]
    acc_ref[...] = jnp.zeros_like(acc_ref)
    e = e_ref[j]
    slot = (m_ref[j] * td + lax.broadcasted_iota(jnp.int32, (td, TOK_TILE), 0)).astype(F32)

    def body(i, carry):
        ts = pl.multiple_of((slo_ref[j] + i) * TOK_TILE, TOK_TILE)
        key = keyt_ref[pl.ds(e, 1), pl.ds(ts, TOK_TILE)]
        onehot = jnp.where(key == slot, 1.0, 0.0).astype(BF16)
        acc_ref[...] += _dot(onehot, h_ref[pl.ds(ts, TOK_TILE), :])
        return carry

    lax.fori_loop(0, n_ref[j], body, 0)
    o_ref[...] = acc_ref[...].astype(o_ref.dtype)


def _dispatch(h2, key_t, lists, max_tiles):
    n, d = h2.shape
    rows = max_tiles * EXP_TM
    resident = dict(pipeline_mode=pl.Buffered(1))
    grid_spec = pltpu.PrefetchScalarGridSpec(
        num_scalar_prefetch=4,
        grid=(rows // DISP_TD,),
        in_specs=[
            pl.BlockSpec((n, d), lambda j, *_: (0, 0), **resident),
            pl.BlockSpec((8, n), lambda j, *_: (0, 0), **resident),
        ],
        out_specs=pl.BlockSpec((DISP_TD, d), lambda j, *_: (j, 0)),
        scratch_shapes=[pltpu.VMEM((DISP_TD, d), F32)],
    )
    return pl.pallas_call(
        _dispatch_kernel,
        grid_spec=grid_spec,
        out_shape=jax.ShapeDtypeStruct((rows, d), BF16),
        name="moe_dispatch",
    )(*lists, h2, key_t)


def _expert_kernel(tidx_ref, te_ref, tp_ref, x_ref, wg_ref, wu_ref, wd_ref, y_ref, acc_ref):
    i = pl.program_id(0)
    f = pl.program_id(1)
    nf = pl.num_programs(1)

    max_pass = EXP_TM // EXP_SUB
    n_steps = D_FF_EXPERT // EXP_TF

    def run(npass, first, last):
        for r in range(max_pass):
            rs = slice(r * EXP_SUB, (r + 1) * EXP_SUB)
            if r >= npass:
                if last:
                    y_ref[rs, :] = jnp.zeros((EXP_SUB, y_ref.shape[1]), y_ref.dtype)
                continue
            x = x_ref[rs, :]
            g = _dot(x, wg_ref[0])
            u = _dot(x, wu_ref[0])
            act = (g * jax.nn.sigmoid(g) * u).astype(BF16)
            part = _dot(act, wd_ref[0])
            total = part if first else acc_ref[rs, :] + part
            if last:
                y_ref[rs, :] = total.astype(y_ref.dtype)
            else:
                acc_ref[rs, :] = total

    kinds = [(f == 0, True, n_steps == 1), (f == nf - 1, n_steps == 1, True)]
    if n_steps > 2:
        kinds.append((jnp.logical_and(f > 0, f < nf - 1), False, False))
    for npass in range(max_pass + 1):
        for cond, first, last in kinds[:1] if n_steps == 1 else kinds:
            @pl.when(jnp.logical_and(tp_ref[i] == npass, cond))
            def _(npass=npass, first=first, last=last):
                run(npass, first, last)


def _experts(xs, w_gu, w_down, tiles, max_tiles):
    rows, d = xs.shape
    tm, tf = EXP_TM, EXP_TF
    nf = D_FF_EXPERT // tf

    def fsel(i, f, tp):
        return jnp.where(tp[i] > 0, f, nf - 1)

    grid_spec = pltpu.PrefetchScalarGridSpec(
        num_scalar_prefetch=3,
        grid=(max_tiles, nf),
        in_specs=[
            pl.BlockSpec((tm, d), lambda i, f, ti, te, tv: (ti[i], 0)),
            pl.BlockSpec((1, d, tf), lambda i, f, ti, te, tv: (te[i], 0, fsel(i, f, tv))),
            pl.BlockSpec((1, d, tf), lambda i, f, ti, te, tv: (te[i], 0, nf + fsel(i, f, tv))),
            pl.BlockSpec((1, tf, d), lambda i, f, ti, te, tv: (te[i], fsel(i, f, tv), 0)),
        ],
        out_specs=pl.BlockSpec((tm, d), lambda i, f, ti, te, tv: (i, 0)),
        scratch_shapes=[pltpu.VMEM((tm, d), F32)],
    )
    return pl.pallas_call(
        _expert_kernel,
        grid_spec=grid_spec,
        out_shape=jax.ShapeDtypeStruct((rows, d), BF16),
        name="moe_experts",
    )(*tiles, xs, w_gu, w_gu, w_down)


def _combine_kernel(ws_ref, rel_ref, need_ref, *refs):
    y_refs = refs[:N_EXPERTS]
    keyc_ref, gate_ref, x_ref, mod_ref, gpost_ref, o_ref, acc_ref = refs[N_EXPERTS:]
    s = pl.program_id(0)
    ts = acc_ref.shape[0]

    def term(e, a, b):
        key = keyc_ref[:, e:e + 1]
        rel = rel_ref[s * N_EXPERTS + e]
        slot = (rel + a + lax.broadcasted_iota(jnp.int32, (ts, b - a), 1)).astype(F32)
        onehot = jnp.where(key == slot, 1.0, 0.0).astype(BF16)
        return gate_ref[:, e:e + 1] * _dot(onehot, y_refs[e][a:b, :])

    a0, b0 = COMB_CHUNKS[0]
    acc = term(0, a0, b0)
    for e in range(1, N_EXPERTS):
        acc = acc + term(e, a0, b0)
    acc_ref[...] = acc
    for e in range(N_EXPERTS):
        for a, b in COMB_CHUNKS[1:]:
            @pl.when(need_ref[s * N_EXPERTS + e] > a)
            def _(e=e, a=a, b=b):
                acc_ref[...] += term(e, a, b)
    gt_f = mod_ref[0, 5:6, :]
    o_ref[...] = x_ref[...] + (1.0 + gt_f) * _rms(acc_ref[...], gpost_ref[...])


def _combine(y, key_c, gates, x2, mod_l, g_post, lists, seq):
    n, d = x2.shape
    per_b = seq // TOK_TILE
    tok = lambda s, *_: (s, 0)

    def window(e):
        return pl.BlockSpec((pl.Element(COMB_WIN), pl.Element(d)),
                            lambda s, ws, rel, need: (ws[s * N_EXPERTS + e] * COMB_ALIGN, 0))

    grid_spec = pltpu.PrefetchScalarGridSpec(
        num_scalar_prefetch=3,
        grid=(n // TOK_TILE,),
        in_specs=[window(e) for e in range(N_EXPERTS)] + [
            pl.BlockSpec((TOK_TILE, LANES), tok),
            pl.BlockSpec((TOK_TILE, LANES), tok),
            pl.BlockSpec((TOK_TILE, d), tok),
            pl.BlockSpec((1, N_MOD, d), lambda s, *_: (s // per_b, 0, 0)),
            pl.BlockSpec((1, d), lambda s, *_: (0, 0)),
        ],
        out_specs=pl.BlockSpec((TOK_TILE, d), tok),
        scratch_shapes=[pltpu.VMEM((TOK_TILE, d), F32)],
    )
    return pl.pallas_call(
        _combine_kernel,
        grid_spec=grid_spec,
        out_shape=jax.ShapeDtypeStruct((n, d), F32),
        name="moe_combine",
    )(*lists, *([y] * N_EXPERTS), key_c, gates, x2, mod_l, g_post)


def kernel(x, c, w_ada, b_ada, g_pre_mix, g_post_mix, g_pre_ffn, g_post_ffn, w_in, b_forget, g_v,
           w_spatial, b_spatial, g_out_attn, g_out_sgu, w_out, w_gate_up_dense, w_down_dense,
           w_router, w_gate_up_exp, w_down_exp):
    bsz, seq, d = x.shape
    depth = w_ada.shape[0]
    n = bsz * seq
    npair = N_HEADS // 2
    assert d == D_MODEL and seq % TOK_TILE == 0 and seq % ATT_TQ == 0

    mod = _modulation(c, w_ada, b_ada).reshape(depth, bsz, N_MOD, d)
    x2 = x.reshape(n, d)
    o3 = 3 * ATTN_W
    o4 = o3 + N_HEADS
    for l in range(depth):
        mod_l = mod[l]
        w_l = w_in[l]
        w_qkv = w_l[:, :o3].astype(BF16)
        w_f = jnp.pad(w_l[:, o3:o4], ((0, 0), (0, LANES - N_HEADS))).astype(BF16)
        w_uv = w_l[:, o4:].astype(BF16)
        q, k, v, zf_t, u, sv = _inproj(x2, mod_l, g_pre_mix[l].reshape(1, d), w_qkv, w_f, w_uv,
                                       g_v[l].reshape(1, SGU_W), seq)
        lcum = _forget_cumsum(zf_t, b_forget[l], seq)
        if l % 2 == 1:
            w_slab = w_gate_up_exp[l // 2].reshape(N_EXPERTS * d, 2 * D_FF_EXPERT)
        elif l + 1 < depth:
            w_slab = w_down_exp[l // 2].reshape(N_EXPERTS * D_FF_EXPERT, d)
        else:
            w_slab = None
        attn, w_cast = _attention(q, k, v, lcum.reshape(npair, 2, n), bsz, seq, w_slab)
        if l % 2 == 1:
            w_gu_bf16 = w_cast.reshape(N_EXPERTS, d, 2 * D_FF_EXPERT)
        elif w_cast is not None:
            w_down_bf16 = w_cast.reshape(N_EXPERTS, D_FF_EXPERT, d)
        sgun = _sgu(u, sv, w_spatial[l].astype(BF16), b_spatial[l].T,
                    g_out_sgu[l].reshape(1, SGU_W))
        w_o = w_out[l].astype(BF16)
        moe = (l % 2 == 1)
        w_r = None
        if moe:
            w_r = jnp.pad(w_router[l // 2], ((0, 0), (0, LANES - N_EXPERTS)))
        res = _outproj(attn, sgun, g_out_attn[l].reshape(1, ATTN_W), w_o[:ATTN_W], w_o[ATTN_W:],
                       x2, mod_l, g_post_mix[l].reshape(1, d), g_pre_ffn[l].reshape(1, d), w_r, seq)
        g_post = g_post_ffn[l].reshape(1, d)
        if not moe:
            x2, h2 = res
            x2 = _ffn_dense(h2, w_gate_up_dense[l // 2].astype(BF16),
                            w_down_dense[l // 2].astype(BF16), x2, mod_l, g_post, seq)
        else:
            x2, h2, logits = res
            key_c, gates, key_t, ends = _route(logits)
            wl = _work_lists(ends, n)
            xs = _dispatch(h2, key_t, wl["disp"], wl["max_tiles"])
            y = _experts(xs, w_gu_bf16, w_down_bf16, wl["tiles"], wl["max_tiles"])
            x2 = _combine(y, key_c, gates, x2, mod_l, g_post, wl["comb"], seq)
    return x2.reshape(bsz, seq, d)
```

```python
import functools
import math

import jax
import jax.numpy as jnp
from jax import lax
from jax.experimental import pallas as pl
from jax.experimental.pallas import tpu as pltpu

F32 = jnp.float32
BF16 = jnp.bfloat16

D_MODEL = 1024
HEAD_DIM = 64
N_HEADS = 8
N_GROUPS = 8
ATTN_W = N_HEADS * HEAD_DIM
SGU_W = N_GROUPS * HEAD_DIM
CHUNK = 128
D_FF_DENSE = 2816
N_EXPERTS = 8
D_FF_EXPERT = 3584
N_MOD = 6
EPS = 1e-6

LANES = 128
TOK_TILE = 512
PROJ_TILE = 1024
ROW_GROUP = 256
ATT_TQ = 256
MXU_TILE = 256
FFN_CHUNKS = ((0, 6 * MXU_TILE), (6 * MXU_TILE, D_FF_DENSE))
EXP_TM = 1024
EXP_SUB = 512
EXP_TF = 7 * MXU_TILE
LOG2E = math.log2(math.e)
DISP_TD = 256
COMB_ALIGN = 16
COMB_WIN = TOK_TILE + COMB_ALIGN
COMB_CHUNKS = ((0, MXU_TILE), (MXU_TILE, COMB_WIN))
UNROUTED = -float(2 ** 30)
GELU_C = math.sqrt(2.0 / math.pi)


def _rms(x, g):
    return x * lax.rsqrt(jnp.mean(x * x, axis=-1, keepdims=True) + EPS) * g


def _dot(a, b):
    return jnp.dot(a, b, preferred_element_type=F32)


def _row_groups(rows):
    return [slice(r, r + ROW_GROUP) for r in range(0, rows, ROW_GROUP)]


def _mod_kernel(c_ref, w_ref, b_ref, o_ref):
    c = c_ref[...]
    ca = (c * jax.nn.sigmoid(c)).astype(BF16)
    o_ref[0] = _dot(ca, w_ref[0].astype(BF16)) + b_ref[0]


def _modulation(c, w_ada, b_ada):
    depth, d, m = w_ada.shape
    bsz = c.shape[0]
    nblk = m // d
    return pl.pallas_call(
        _mod_kernel,
        grid=(depth, nblk),
        in_specs=[
            pl.BlockSpec((bsz, d), lambda l, j: (0, 0)),
            pl.BlockSpec((1, d, d), lambda l, j: (l, 0, j)),
            pl.BlockSpec((1, 1, d), lambda l, j: (l, 0, j)),
        ],
        out_specs=pl.BlockSpec((1, bsz, d), lambda l, j: (l, 0, j)),
        out_shape=jax.ShapeDtypeStruct((depth, bsz, m), F32),
        name="adaln_mod",
    )(c, w_ada, b_ada.reshape(depth, 1, m))


def _inproj_kernel(x_ref, mod_ref, gpre_ref, wqkv_ref, wf_ref, wuv_ref, gv_ref,
                   q_ref, k_ref, v_ref, zf_ref, u_ref, sv_ref):
    sh = mod_ref[0, 0:1, :]
    sc = mod_ref[0, 1:2, :]
    for rs in _row_groups(x_ref.shape[0]):
        h = (_rms(x_ref[rs, :], gpre_ref[...]) * (1.0 + sc) + sh).astype(BF16)
        qkv = _dot(h, wqkv_ref[...])
        q_ref[rs, :] = (qkv[:, :ATTN_W] * (HEAD_DIM ** -0.5 * LOG2E)).astype(BF16)
        k_ref[rs, :] = qkv[:, ATTN_W:2 * ATTN_W].astype(BF16)
        v_ref[rs, :] = qkv[:, 2 * ATTN_W:].astype(BF16)
        zf_ref[:, rs] = _dot(h, wf_ref[...]).T[:N_HEADS, :]
        uv = _dot(h, wuv_ref[...])
        uv = uv * (0.5 * (1.0 + jnp.tanh(GELU_C * (uv + 0.044715 * (uv * uv * uv)))))
        u_ref[rs, :] = uv[:, :SGU_W].astype(BF16)
        s = uv[:, SGU_W:]
        sc_ = s - jnp.mean(s, axis=-1, keepdims=True)
        sv = sc_ * lax.rsqrt(jnp.mean(sc_ * sc_, axis=-1, keepdims=True) + EPS) * gv_ref[...]
        sv_ref[rs, :] = sv.astype(BF16)


def _inproj(x2, mod_l, g_pre, w_qkv, w_f, w_uv, g_v, seq):
    n, d = x2.shape
    tm = PROJ_TILE
    per_b = seq // tm
    row = lambda i: (i, 0)
    const = lambda i: (0, 0)
    outs = [jax.ShapeDtypeStruct((n, ATTN_W), BF16)] * 3 + [
        jax.ShapeDtypeStruct((N_HEADS, n), F32),
        jax.ShapeDtypeStruct((n, SGU_W), BF16),
        jax.ShapeDtypeStruct((n, SGU_W), BF16)]
    return pl.pallas_call(
        _inproj_kernel,
        grid=(n // tm,),
        in_specs=[
            pl.BlockSpec((tm, d), row),
            pl.BlockSpec((1, N_MOD, d), lambda i: (i // per_b, 0, 0)),
            pl.BlockSpec((1, d), const),
            pl.BlockSpec(w_qkv.shape, const),
            pl.BlockSpec(w_f.shape, const),
            pl.BlockSpec(w_uv.shape, const),
            pl.BlockSpec((1, SGU_W), const),
        ],
        out_specs=[pl.BlockSpec((tm, ATTN_W), row)] * 3 + [
            pl.BlockSpec((N_HEADS, tm), lambda i: (0, i)),
            pl.BlockSpec((tm, SGU_W), row),
            pl.BlockSpec((tm, SGU_W), row)],
        out_shape=outs,
        name="inproj",
    )(x2, mod_l, g_pre, w_qkv, w_f, w_uv, g_v)


def _gates_kernel(z_ref, b_ref, o_ref):
    z = z_ref[...] + b_ref[...]
    ls = jnp.minimum(z, 0.0) - jnp.log1p(jnp.exp(-jnp.abs(z)))
    seq = ls.shape[-1]
    pos = lax.broadcasted_iota(jnp.int32, ls.shape, 1)
    shift = 1
    while shift < seq:
        ls = ls + jnp.where(pos >= shift, pltpu.roll(ls, shift, 1), 0.0)
        shift *= 2
    o_ref[...] = ls * LOG2E


def _forget_cumsum(zf_t, b_forget, seq):
    h, n = zf_t.shape
    return pl.pallas_call(
        _gates_kernel,
        grid=(n // seq,),
        in_specs=[pl.BlockSpec((h, seq), lambda b: (0, b)),
                  pl.BlockSpec((h, 1), lambda b: (0, 0))],
        out_specs=pl.BlockSpec((h, seq), lambda b: (0, b)),
        out_shape=jax.ShapeDtypeStruct((h, n), F32),
        name="forget_cumsum",
    )(zf_t, b_forget.reshape(h, 1))


def _attn_kernel(q_ref, k_ref, v_ref, lk_ref, *rest):
    if len(rest) == 3:
        wsrc_ref, o_ref, wdst_ref = rest
        wdst_ref[...] = wsrc_ref[...].astype(wdst_ref.dtype)
    else:
        (o_ref,) = rest
    tq = ATT_TQ
    seq = q_ref.shape[0]
    nq = seq // tq
    l_rows = jnp.concatenate([lk_ref[0], jnp.zeros((LANES - 2, seq), F32)], axis=0)
    l_cols = l_rows.T
    lane = lax.broadcasted_iota(jnp.int32, (tq, LANES), 1)
    left = lane < HEAD_DIM
    rows = lax.broadcasted_iota(jnp.int32, (tq, tq), 0)
    cols = lax.broadcasted_iota(jnp.int32, (tq, tq), 1)
    causal = cols <= rows
    zero = jnp.zeros((tq, LANES), BF16)
    one = jnp.ones((tq, LANES), BF16)
    def scores(qi):
        qs = slice(qi * tq, (qi + 1) * tq)
        q2 = q_ref[qs, :]
        qq = jnp.concatenate([jnp.where(left, q2, zero), jnp.where(left, zero, q2)], axis=0)
        t_blocks = [[], []]
        mt = [None, None]
        for j in range(qi + 1):
            ks = slice(j * tq, (j + 1) * tq)
            t2 = lax.dot_general(qq, k_ref[ks, :], (((1,), (1,)), ((), ())),
                                 preferred_element_type=F32)
            for hh in range(2):
                t = t2[hh * tq:(hh + 1) * tq] - lk_ref[0, hh:hh + 1, ks]
                if j == qi:
                    t = jnp.where(causal, t, -jnp.inf)
                t_blocks[hh].append(t)
                mj = jnp.maximum(t[:, :LANES], t[:, LANES:])
                mt[hh] = mj if mt[hh] is None else jnp.maximum(mt[hh], mj)
        shift = []
        for hh in range(2):
            lq = l_cols[qs, hh:hh + 1]
            row_max = jnp.max(mt[hh], axis=-1, keepdims=True) + lq
            shift.append(row_max - lq)
        return t_blocks, shift

    def values(qi, t_blocks, shift):
        qs = slice(qi * tq, (qi + 1) * tq)
        accs = [None, None]
        for j in range(qi + 1):
            ks = slice(j * tq, (j + 1) * tq)
            pp = jnp.concatenate([jnp.exp2(t_blocks[hh][j] - shift[hh]) for hh in range(2)],
                                 axis=0).astype(BF16)
            vaug = jnp.concatenate([v_ref[ks, :], one], axis=1)
            part = _dot(pp, vaug)
            accs[j % 2] = part if accs[j % 2] is None else accs[j % 2] + part
        acc = accs[0] if accs[1] is None else accs[0] + accs[1]
        out0 = acc[:tq, :LANES] / acc[:tq, LANES:]
        out1 = acc[tq:, :LANES] / acc[tq:, LANES:]
        o_ref[qs, :] = jnp.where(left, out0, out1).astype(o_ref.dtype)

    order = list(range(nq - 1, -1, -1))
    pending = scores(order[0])
    for pos, qi in enumerate(order):
        nxt = scores(order[pos + 1]) if pos + 1 < nq else None
        values(qi, *pending)
        pending = nxt


def _attention(q, k, v, l_pairs, bsz, seq, w_f32=None):
    n = q.shape[0]
    npair = N_HEADS // 2
    blk = pl.BlockSpec((seq, LANES), lambda b, p: (b, p))
    in_specs = [blk, blk, blk, pl.BlockSpec((1, 2, seq), lambda b, p: (p, 0, b))]
    out_specs = [blk]
    out_shape = [jax.ShapeDtypeStruct((n, ATTN_W), BF16)]
    args = [q, k, v, l_pairs]
    if w_f32 is not None:
        rows, cols = w_f32.shape
        rb = rows // (bsz * npair)
        assert rb * bsz * npair == rows and rb % COMB_ALIGN == 0
        wblk = pl.BlockSpec((rb, cols), lambda b, p: (b * npair + p, 0))
        in_specs.append(wblk)
        out_specs.append(wblk)
        out_shape.append(jax.ShapeDtypeStruct((rows, cols), BF16))
        args.append(w_f32)
    res = pl.pallas_call(
        _attn_kernel,
        grid=(bsz, npair),
        in_specs=in_specs,
        out_specs=out_specs,
        out_shape=out_shape,
        name="fox_attention",
    )(*args)
    return res if w_f32 is not None else (res[0], None)


def _sgu_kernel(u_ref, sv_ref, w_ref, bt_ref, g_ref, o_ref):
    tm = u_ref.shape[0]
    npair = N_GROUPS // 2
    lane = lax.broadcasted_iota(jnp.int32, (CHUNK, LANES), 1)
    left = lane < HEAD_DIM
    r = lax.broadcasted_iota(jnp.int32, (CHUNK, CHUNK), 0)
    c = lax.broadcasted_iota(jnp.int32, (CHUNK, CHUNK), 1)
    causal = c <= r
    lhs, bias = [], []
    for j in range(npair):
        wa = jnp.where(causal, w_ref[2 * j], jnp.zeros((), BF16))
        wb = jnp.where(causal, w_ref[2 * j + 1], jnp.zeros((), BF16))
        lhs.append(jnp.concatenate([wa, wb], axis=1))
        bias.append(jnp.where(left, bt_ref[:, 2 * j:2 * j + 1], bt_ref[:, 2 * j + 1:2 * j + 2]))
    for ci in range(tm // CHUNK):
        rs = slice(ci * CHUNK, (ci + 1) * CHUNK)
        blks = []
        ssq = jnp.zeros((CHUNK, 1), F32)
        for j in range(npair):
            cs = slice(j * LANES, (j + 1) * LANES)
            svb = sv_ref[rs, cs]
            zero = jnp.zeros_like(svb)
            rhs = jnp.concatenate([jnp.where(left, svb, zero), jnp.where(left, zero, svb)], axis=0)
            mixed = _dot(lhs[j], rhs) + bias[j]
            ob = u_ref[rs, cs].astype(F32) * mixed
            ssq = ssq + jnp.sum(ob * ob, axis=-1, keepdims=True)
            blks.append(ob)
        inv = lax.rsqrt(ssq * (1.0 / SGU_W) + EPS)
        for j in range(npair):
            cs = slice(j * LANES, (j + 1) * LANES)
            o_ref[rs, cs] = (blks[j] * inv * g_ref[:, cs]).astype(o_ref.dtype)


def _sgu(u, sv, w_sp, b_sp_t, g_out):
    n = u.shape[0]
    tm = TOK_TILE
    row = lambda i: (i, 0)
    return pl.pallas_call(
        _sgu_kernel,
        grid=(n // tm,),
        in_specs=[
            pl.BlockSpec((tm, SGU_W), row),
            pl.BlockSpec((tm, SGU_W), row),
            pl.BlockSpec(w_sp.shape, lambda i: (0, 0, 0)),
            pl.BlockSpec(b_sp_t.shape, lambda i: (0, 0)),
            pl.BlockSpec((1, SGU_W), lambda i: (0, 0)),
        ],
        out_specs=pl.BlockSpec((tm, SGU_W), row),
        out_shape=jax.ShapeDtypeStruct((n, SGU_W), BF16),
        name="sgu",
    )(u, sv, w_sp, b_sp_t, g_out)


def _outproj_kernel(a_ref, s_ref, ga_ref, wa_ref, ws_ref, x_ref, mod_ref, gpost_ref, gffn_ref,
                    *rest, with_router):
    if with_router:
        wr_ref, xo_ref, h_ref, lg_ref = rest
    else:
        xo_ref, h_ref = rest
    gt_m = mod_ref[0, 2:3, :]
    sh_f = mod_ref[0, 3:4, :]
    sc_f = mod_ref[0, 4:5, :]
    if with_router:
        wr = wr_ref[...]
        w_hi = wr.astype(BF16)
        w_lo = (wr - w_hi.astype(F32)).astype(BF16)
    for rs in _row_groups(x_ref.shape[0]):
        a = _rms(a_ref[rs, :].astype(F32), ga_ref[...]).astype(BF16)
        y = _dot(a, wa_ref[...]) + _dot(s_ref[rs, :], ws_ref[...])
        xn = x_ref[rs, :] + (1.0 + gt_m) * _rms(y, gpost_ref[...])
        xo_ref[rs, :] = xn
        h = _rms(xn, gffn_ref[...]) * (1.0 + sc_f) + sh_f
        h_hi = h.astype(BF16)
        h_ref[rs, :] = h_hi
        if with_router:
            h_lo = (h - h_hi.astype(F32)).astype(BF16)
            lg_ref[rs, :] = _dot(h_hi, w_hi) + (_dot(h_hi, w_lo) + _dot(h_lo, w_hi))


def _outproj(attn, sgun, g_attn, w_a, w_s, x2, mod_l, g_post, g_ffn, w_router, seq):
    n, d = x2.shape
    tm = PROJ_TILE
    per_b = seq // tm
    row = lambda i: (i, 0)
    const = lambda i: (0, 0)
    with_router = w_router is not None
    in_specs = [
        pl.BlockSpec((tm, ATTN_W), row),
        pl.BlockSpec((tm, SGU_W), row),
        pl.BlockSpec((1, ATTN_W), const),
        pl.BlockSpec(w_a.shape, const),
        pl.BlockSpec(w_s.shape, const),
        pl.BlockSpec((tm, d), row),
        pl.BlockSpec((1, N_MOD, d), lambda i: (i // per_b, 0, 0)),
        pl.BlockSpec((1, d), const),
        pl.BlockSpec((1, d), const),
    ]
    args = [attn, sgun, g_attn, w_a, w_s, x2, mod_l, g_post, g_ffn]
    out_specs = [pl.BlockSpec((tm, d), row), pl.BlockSpec((tm, d), row)]
    out_shape = [jax.ShapeDtypeStruct((n, d), F32), jax.ShapeDtypeStruct((n, d), BF16)]
    if with_router:
        in_specs.append(pl.BlockSpec(w_router.shape, const))
        args.append(w_router)
        out_specs.append(pl.BlockSpec((tm, LANES), row))
        out_shape.append(jax.ShapeDtypeStruct((n, LANES), F32))
    return pl.pallas_call(
        functools.partial(_outproj_kernel, with_router=with_router),
        grid=(n // tm,),
        in_specs=in_specs,
        out_specs=out_specs,
        out_shape=out_shape,
        name="outproj_router" if with_router else "outproj",
    )(*args)


def _ffn_kernel(h_ref, wgu_ref, wd_ref, x_ref, mod_ref, gpost_ref, o_ref):
    h = h_ref[...]
    y = None
    for a, b in FFN_CHUNKS:
        g = _dot(h, wgu_ref[:, a:b])
        u = _dot(h, wgu_ref[:, D_FF_DENSE + a:D_FF_DENSE + b])
        act = (g * jax.nn.sigmoid(g) * u).astype(BF16)
        part = _dot(act, wd_ref[a:b, :])
        y = part if y is None else y + part
    gt_f = mod_ref[0, 5:6, :]
    o_ref[...] = x_ref[...] + (1.0 + gt_f) * _rms(y, gpost_ref[...])


def _ffn_dense(h2, w_gu, w_down, x2, mod_l, g_post, seq):
    n, d = x2.shape
    tm = TOK_TILE
    per_b = seq // tm
    resident = dict(pipeline_mode=pl.Buffered(1))
    return pl.pallas_call(
        _ffn_kernel,
        grid=(n // tm,),
        in_specs=[
            pl.BlockSpec((tm, d), lambda i: (i, 0)),
            pl.BlockSpec(w_gu.shape, lambda i: (0, 0), **resident),
            pl.BlockSpec(w_down.shape, lambda i: (0, 0), **resident),
            pl.BlockSpec((tm, d), lambda i: (i, 0)),
            pl.BlockSpec((1, N_MOD, d), lambda i: (i // per_b, 0, 0)),
            pl.BlockSpec((1, d), lambda i: (0, 0)),
        ],
        out_specs=pl.BlockSpec((tm, d), lambda i: (i, 0)),
        out_shape=jax.ShapeDtypeStruct((n, d), F32),
        name="ffn_dense",
    )(h2, w_gu, w_down, x2, mod_l, g_post)


def _route_kernel(lg_ref, keyc_ref, gate_ref, keyt_ref, ends_ref, base_ref):
    s = pl.program_id(0)
    tm = lg_ref.shape[0]

    @pl.when(s == 0)
    def _():
        base_ref[...] = jnp.zeros_like(base_ref)

    lane = lax.broadcasted_iota(jnp.int32, (tm, LANES), 1)
    lg = jnp.where(lane < N_EXPERTS, lg_ref[...], -jnp.inf)
    m1 = jnp.max(lg, axis=-1, keepdims=True)
    i1 = jnp.min(jnp.where(lg == m1, lane, LANES), axis=-1, keepdims=True)
    sel1 = lane == i1
    lg2 = jnp.where(sel1, -jnp.inf, lg)
    m2 = jnp.max(lg2, axis=-1, keepdims=True)
    i2 = jnp.min(jnp.where(lg2 == m2, lane, LANES), axis=-1, keepdims=True)
    sel2 = lane == i2
    e2 = jnp.exp(m2 - m1)
    w1 = 1.0 / (1.0 + e2)
    w2 = e2 / (1.0 + e2)
    gate_ref[...] = jnp.where(sel1, w1, 0.0) + jnp.where(sel2, w2, 0.0)
    chosen = jnp.logical_or(sel1, sel2)
    mask = jnp.where(chosen, 1.0, 0.0)
    r = lax.broadcasted_iota(jnp.int32, (tm, tm), 0)
    c = lax.broadcasted_iota(jnp.int32, (tm, tm), 1)
    tri = jnp.where(c < r, 1.0, 0.0).astype(BF16)
    base = base_ref[0:1, :]
    rank = _dot(tri, mask.astype(BF16)) + base
    key = jnp.where(chosen, rank, UNROUTED)
    keyc_ref[...] = key
    keyt_ref[...] = key.T
    new_base = base + jnp.sum(mask, axis=0, keepdims=True)
    base_ref[...] = jnp.broadcast_to(new_base, base_ref.shape)
    ends_ref[0] = jnp.broadcast_to(new_base, ends_ref.shape[1:])


def _route(logits):
    n = logits.shape[0]
    tm = TOK_TILE
    ns = n // tm
    row = lambda s: (s, 0)
    return pl.pallas_call(
        _route_kernel,
        grid=(ns,),
        in_specs=[pl.BlockSpec((tm, LANES), row)],
        out_specs=[
            pl.BlockSpec((tm, LANES), row),
            pl.BlockSpec((tm, LANES), row),
            pl.BlockSpec((LANES, tm), lambda s: (0, s)),
            pl.BlockSpec((1, 8, LANES), lambda s: (s, 0, 0)),
        ],
        out_shape=[
            jax.ShapeDtypeStruct((n, LANES), F32),
            jax.ShapeDtypeStruct((n, LANES), F32),
            jax.ShapeDtypeStruct((LANES, n), F32),
            jax.ShapeDtypeStruct((ns, 8, LANES), F32),
        ],
        scratch_shapes=[pltpu.VMEM((8, LANES), F32)],
        name="moe_route",
    )(logits)


def _work_lists(ends, n_tokens):
    e_n = N_EXPERTS
    sub = EXP_TM // DISP_TD
    max_tiles = (2 * n_tokens) // EXP_TM + e_n
    i32 = jnp.int32

    ends_i = ends[:, 0, :e_n].astype(i32)
    base = jnp.concatenate([jnp.zeros((1, e_n), i32), ends_i], axis=0)
    cnt = base[-1]
    ntile = (cnt + EXP_TM - 1) // EXP_TM
    tile_end = jnp.cumsum(ntile)
    tile_start = tile_end - ntile
    total_tiles = tile_end[-1]
    npass = (cnt + EXP_SUB - 1) // EXP_SUB
    off = tile_start * EXP_TM

    e_ids = jnp.arange(e_n, dtype=i32)

    def expert_of(tile):
        e = jnp.sum((tile_end[None, :] <= tile[:, None]).astype(i32), axis=1)
        return e, (e[:, None] == e_ids[None, :]).astype(i32)

    def pick(onehot, per_expert):
        return jnp.sum(onehot * per_expert[None, :], axis=1)

    t = jnp.arange(max_tiles, dtype=i32)
    tc = jnp.minimum(t, total_tiles - 1)
    tile_e, oh_t = expert_of(tc)
    tile_pass = jnp.clip(pick(oh_t, npass) - (tc - pick(oh_t, tile_start)) * (EXP_TM // EXP_SUB),
                         1, EXP_TM // EXP_SUB)
    tile_pass = jnp.where(t < total_tiles, tile_pass, 0)

    j = jnp.arange(max_tiles * sub, dtype=i32)
    jt = jnp.minimum(j // sub, total_tiles - 1)
    je, oh_j = expert_of(jt)
    jm = (jt - pick(oh_j, tile_start)) * sub + j % sub
    lo_rank = jm * DISP_TD
    hi_rank = jnp.minimum(lo_rank + DISP_TD, pick(oh_j, cnt))
    base_e = jnp.sum(oh_j[:, :, None] * base.T[None, :, :], axis=1)
    s_lo = jnp.sum((base_e[:, 1:] <= lo_rank[:, None]).astype(i32), axis=1)
    s_hi = jnp.sum((base_e[:, :-1] < hi_rank[:, None]).astype(i32), axis=1)
    n_src = jnp.where(j // sub < total_tiles, jnp.maximum(s_hi - s_lo, 0), 0)

    rows = max_tiles * EXP_TM
    start = off[None, :] + base[:-1]
    stop = off[None, :] + base[1:]
    wstart = jnp.minimum(start // COMB_ALIGN * COMB_ALIGN, rows - COMB_WIN)
    need = jnp.where(stop > start, stop - wstart, 0)
    rel = wstart - off[None, :]
    return dict(
        disp=(je, jm, s_lo, n_src),
        comb=((wstart // COMB_ALIGN).reshape(-1), rel.reshape(-1), need.reshape(-1)),
        tiles=(tc, tile_e, tile_pass),
        max_tiles=max_tiles)


def _dispatch_kernel(e_ref, m_ref, slo_ref, n_ref, h_ref, keyt_ref, o_ref, acc_ref):
    j = pl.program_id(0)
    td = o_ref.shape[0]
    acc_ref[...] = jnp.zeros_like(acc_ref)
    e = e_ref[j]
    slot = (m_ref[j] * td + lax.broadcasted_iota(jnp.int32, (td, TOK_TILE), 0)).astype(F32)

    def body(i, carry):
        ts = pl.multiple_of((slo_ref[j] + i) * TOK_TILE, TOK_TILE)
        key = keyt_ref[pl.ds(e, 1), pl.ds(ts, TOK_TILE)]
        onehot = jnp.where(key == slot, 1.0, 0.0).astype(BF16)
        acc_ref[...] += _dot(onehot, h_ref[pl.ds(ts, TOK_TILE), :])
        return carry

    lax.fori_loop(0, n_ref[j], body, 0)
    o_ref[...] = acc_ref[...].astype(o_ref.dtype)


def _dispatch(h2, key_t, lists, max_tiles):
    n, d = h2.shape
    rows = max_tiles * EXP_TM
    resident = dict(pipeline_mode=pl.Buffered(1))
    grid_spec = pltpu.PrefetchScalarGridSpec(
        num_scalar_prefetch=4,
        grid=(rows // DISP_TD,),
        in_specs=[
            pl.BlockSpec((n, d), lambda j, *_: (0, 0), **resident),
            pl.BlockSpec((8, n), lambda j, *_: (0, 0), **resident),
        ],
        out_specs=pl.BlockSpec((DISP_TD, d), lambda j, *_: (j, 0)),
        scratch_shapes=[pltpu.VMEM((DISP_TD, d), F32)],
    )
    return pl.pallas_call(
        _dispatch_kernel,
        grid_spec=grid_spec,
        out_shape=jax.ShapeDtypeStruct((rows, d), BF16),
        name="moe_dispatch",
    )(*lists, h2, key_t)


def _expert_kernel(tidx_ref, te_ref, tp_ref, x_ref, wg_ref, wu_ref, wd_ref, y_ref, acc_ref):
    i = pl.program_id(0)
    f = pl.program_id(1)
    nf = pl.num_programs(1)

    max_pass = EXP_TM // EXP_SUB
    n_steps = D_FF_EXPERT // EXP_TF

    def run(npass, first, last):
        for r in range(max_pass):
            rs = slice(r * EXP_SUB, (r + 1) * EXP_SUB)
            if r >= npass:
                if last:
                    y_ref[rs, :] = jnp.zeros((EXP_SUB, y_ref.shape[1]), y_ref.dtype)
                continue
            x = x_ref[rs, :]
            g = _dot(x, wg_ref[0])
            u = _dot(x, wu_ref[0])
            act = (g * jax.nn.sigmoid(g) * u).astype(BF16)
            part = _dot(act, wd_ref[0])
            total = part if first else acc_ref[rs, :] + part
            if last:
                y_ref[rs, :] = total.astype(y_ref.dtype)
            else:
                acc_ref[rs, :] = total

    kinds = [(f == 0, True, n_steps == 1), (f == nf - 1, n_steps == 1, True)]
    if n_steps > 2:
        kinds.append((jnp.logical_and(f > 0, f < nf - 1), False, False))
    for npass in range(max_pass + 1):
        for cond, first, last in kinds[:1] if n_steps == 1 else kinds:
            @pl.when(jnp.logical_and(tp_ref[i] == npass, cond))
            def _(npass=npass, first=first, last=last):
                run(npass, first, last)


def _experts(xs, w_gu, w_down, tiles, max_tiles):
    rows, d = xs.shape
    tm, tf = EXP_TM, EXP_TF
    nf = D_FF_EXPERT // tf

    def fsel(i, f, tp):
        return jnp.where(tp[i] > 0, f, nf - 1)

    grid_spec = pltpu.PrefetchScalarGridSpec(
        num_scalar_prefetch=3,
        grid=(max_tiles, nf),
        in_specs=[
            pl.BlockSpec((tm, d), lambda i, f, ti, te, tv: (ti[i], 0)),
            pl.BlockSpec((1, d, tf), lambda i, f, ti, te, tv: (te[i], 0, fsel(i, f, tv))),
            pl.BlockSpec((1, d, tf), lambda i, f, ti, te, tv: (te[i], 0, nf + fsel(i, f, tv))),
            pl.BlockSpec((1, tf, d), lambda i, f, ti, te, tv: (te[i], fsel(i, f, tv), 0)),
        ],
        out_specs=pl.BlockSpec((tm, d), lambda i, f, ti, te, tv: (i, 0)),
        scratch_shapes=[pltpu.VMEM((tm, d), F32)],
    )
    return pl.pallas_call(
        _expert_kernel,
        grid_spec=grid_spec,
        out_shape=jax.ShapeDtypeStruct((rows, d), BF16),
        name="moe_experts",
    )(*tiles, xs, w_gu, w_gu, w_down)


def _combine_kernel(ws_ref, rel_ref, need_ref, *refs):
    y_refs = refs[:N_EXPERTS]
    keyc_ref, gate_ref, x_ref, mod_ref, gpost_ref, o_ref, acc_ref = refs[N_EXPERTS:]
    s = pl.program_id(0)
    ts = acc_ref.shape[0]

    def term(e, a, b):
        key = keyc_ref[:, e:e + 1]
        rel = rel_ref[s * N_EXPERTS + e]
        slot = (rel + a + lax.broadcasted_iota(jnp.int32, (ts, b - a), 1)).astype(F32)
        onehot = jnp.where(key == slot, 1.0, 0.0).astype(BF16)
        return gate_ref[:, e:e + 1] * _dot(onehot, y_refs[e][a:b, :])

    a0, b0 = COMB_CHUNKS[0]
    acc = term(0, a0, b0)
    for e in range(1, N_EXPERTS):
        acc = acc + term(e, a0, b0)
    acc_ref[...] = acc
    for e in range(N_EXPERTS):
        for a, b in COMB_CHUNKS[1:]:
            @pl.when(need_ref[s * N_EXPERTS + e] > a)
            def _(e=e, a=a, b=b):
                acc_ref[...] += term(e, a, b)
    gt_f = mod_ref[0, 5:6, :]
    o_ref[...] = x_ref[...] + (1.0 + gt_f) * _rms(acc_ref[...], gpost_ref[...])


def _combine(y, key_c, gates, x2, mod_l, g_post, lists, seq):
    n, d = x2.shape
    per_b = seq // TOK_TILE
    tok = lambda s, *_: (s, 0)

    def window(e):
        return pl.BlockSpec((pl.Element(COMB_WIN), pl.Element(d)),
                            lambda s, ws, rel, need: (ws[s * N_EXPERTS + e] * COMB_ALIGN, 0))

    grid_spec = pltpu.PrefetchScalarGridSpec(
        num_scalar_prefetch=3,
        grid=(n // TOK_TILE,),
        in_specs=[window(e) for e in range(N_EXPERTS)] + [
            pl.BlockSpec((TOK_TILE, LANES), tok),
            pl.BlockSpec((TOK_TILE, LANES), tok),
            pl.BlockSpec((TOK_TILE, d), tok),
            pl.BlockSpec((1, N_MOD, d), lambda s, *_: (s // per_b, 0, 0)),
            pl.BlockSpec((1, d), lambda s, *_: (0, 0)),
        ],
        out_specs=pl.BlockSpec((TOK_TILE, d), tok),
        scratch_shapes=[pltpu.VMEM((TOK_TILE, d), F32)],
    )
    return pl.pallas_call(
        _combine_kernel,
        grid_spec=grid_spec,
        out_shape=jax.ShapeDtypeStruct((n, d), F32),
        name="moe_combine",
    )(*lists, *([y] * N_EXPERTS), key_c, gates, x2, mod_l, g_post)


def kernel(x, c, w_ada, b_ada, g_pre_mix, g_post_mix, g_pre_ffn, g_post_ffn, w_in, b_forget, g_v,
           w_spatial, b_spatial, g_out_attn, g_out_sgu, w_out, w_gate_up_dense, w_down_dense,
           w_router, w_gate_up_exp, w_down_exp):
    bsz, seq, d = x.shape
    depth = w_ada.shape[0]
    n = bsz * seq
    npair = N_HEADS // 2
    assert d == D_MODEL and seq % PROJ_TILE == 0 and seq % TOK_TILE == 0 and seq % ATT_TQ == 0

    mod = _modulation(c, w_ada, b_ada).reshape(depth, bsz, N_MOD, d)
    x2 = x.reshape(n, d)
    o3 = 3 * ATTN_W
    o4 = o3 + N_HEADS
    for l in range(depth):
        mod_l = mod[l]
        w_l = w_in[l]
        w_qkv = w_l[:, :o3].astype(BF16)
        w_f = jnp.pad(w_l[:, o3:o4], ((0, 0), (0, LANES - N_HEADS))).astype(BF16)
        w_uv = w_l[:, o4:].astype(BF16)
        q, k, v, zf_t, u, sv = _inproj(x2, mod_l, g_pre_mix[l].reshape(1, d), w_qkv, w_f, w_uv,
                                       g_v[l].reshape(1, SGU_W), seq)
        lcum = _forget_cumsum(zf_t, b_forget[l], seq)
        if l % 2 == 1:
            w_slab = w_gate_up_exp[l // 2].reshape(N_EXPERTS * d, 2 * D_FF_EXPERT)
        elif l + 1 < depth:
            w_slab = w_down_exp[l // 2].reshape(N_EXPERTS * D_FF_EXPERT, d)
        else:
            w_slab = None
        attn, w_cast = _attention(q, k, v, lcum.reshape(npair, 2, n), bsz, seq, w_slab)
        if l % 2 == 1:
            w_gu_bf16 = w_cast.reshape(N_EXPERTS, d, 2 * D_FF_EXPERT)
        elif w_cast is not None:
            w_down_bf16 = w_cast.reshape(N_EXPERTS, D_FF_EXPERT, d)
        sgun = _sgu(u, sv, w_spatial[l].astype(BF16), b_spatial[l].T,
                    g_out_sgu[l].reshape(1, SGU_W))
        w_o = w_out[l].astype(BF16)
        moe = (l % 2 == 1)
        w_r = None
        if moe:
            w_r = jnp.pad(w_router[l // 2], ((0, 0), (0, LANES - N_EXPERTS)))
        res = _outproj(attn, sgun, g_out_attn[l].reshape(1, ATTN_W), w_o[:ATTN_W], w_o[ATTN_W:],
                       x2, mod_l, g_post_mix[l].reshape(1, d), g_pre_ffn[l].reshape(1, d), w_r, seq)
        g_post = g_post_ffn[l].reshape(1, d)
        if not moe:
            x2, h2 = res
            x2 = _ffn_dense(h2, w_gate_up_dense[l // 2].astype(BF16),
                            w_down_dense[l // 2].astype(BF16), x2, mod_l, g_post, seq)
        else:
            x2, h2, logits = res
            key_c, gates, key_t, ends = _route(logits)
            wl = _work_lists(ends, n)
            xs = _dispatch(h2, key_t, wl["disp"], wl["max_tiles"])
            y = _experts(xs, w_gu_bf16, w_down_bf16, wl["tiles"], wl["max_tiles"])
            x2 = _combine(y, key_c, gates, x2, mod_l, g_post, wl["comb"], seq)
    return x2.reshape(bsz, seq, d)
```

```python
import functools
import math

import jax
import jax.numpy as jnp
from jax import lax
from jax.experimental import pallas as pl
from jax.experimental.pallas import tpu as pltpu

F32 = jnp.float32
BF16 = jnp.bfloat16

D_MODEL = 1024
HEAD_DIM = 64
N_HEADS = 8
N_GROUPS = 8
ATTN_W = N_HEADS * HEAD_DIM
SGU_W = N_GROUPS * HEAD_DIM
CHUNK = 128
D_FF_DENSE = 2816
N_EXPERTS = 8
D_FF_EXPERT = 3584
N_MOD = 6
EPS = 1e-6

LANES = 128
TOK_TILE = 512
PROJ_TILE = 1024
ROW_GROUP = 256
ATT_TQ = 256
MXU_TILE = 256
FFN_CHUNKS = ((0, 6 * MXU_TILE), (6 * MXU_TILE, D_FF_DENSE))
EXP_TM = 1024
EXP_SUB = 512
EXP_TF = 7 * MXU_TILE
LOG2E = math.log2(math.e)
DISP_TD = 256
COMB_ALIGN = 16
COMB_WIN = TOK_TILE + COMB_ALIGN
COMB_CHUNKS = ((0, MXU_TILE), (MXU_TILE, COMB_WIN))
UNROUTED = -float(2 ** 30)
GELU_C = math.sqrt(2.0 / math.pi)


def _rms(x, g):
    return x * lax.rsqrt(jnp.mean(x * x, axis=-1, keepdims=True) + EPS) * g


def _dot(a, b):
    return jnp.dot(a, b, preferred_element_type=F32)


def _row_groups(rows):
    return [slice(r, r + ROW_GROUP) for r in range(0, rows, ROW_GROUP)]


def _mod_kernel(c_ref, w_ref, b_ref, o_ref):
    c = c_ref[...]
    ca = (c * jax.nn.sigmoid(c)).astype(BF16)
    o_ref[0] = _dot(ca, w_ref[0].astype(BF16)) + b_ref[0]


def _modulation(c, w_ada, b_ada):
    depth, d, m = w_ada.shape
    bsz = c.shape[0]
    nblk = m // d
    return pl.pallas_call(
        _mod_kernel,
        grid=(depth, nblk),
        in_specs=[
            pl.BlockSpec((bsz, d), lambda l, j: (0, 0)),
            pl.BlockSpec((1, d, d), lambda l, j: (l, 0, j)),
            pl.BlockSpec((1, 1, d), lambda l, j: (l, 0, j)),
        ],
        out_specs=pl.BlockSpec((1, bsz, d), lambda l, j: (l, 0, j)),
        out_shape=jax.ShapeDtypeStruct((depth, bsz, m), F32),
        name="adaln_mod",
    )(c, w_ada, b_ada.reshape(depth, 1, m))


def _inproj_kernel(x_ref, mod_ref, gpre_ref, wqkv_ref, wf_ref, wuv_ref, gv_ref,
                   q_ref, k_ref, v_ref, zf_ref, u_ref, sv_ref):
    sh = mod_ref[0, 0:1, :]
    sc = mod_ref[0, 1:2, :]
    for rs in _row_groups(x_ref.shape[0]):
        h = (_rms(x_ref[rs, :], gpre_ref[...]) * (1.0 + sc) + sh).astype(BF16)
        qkv = _dot(h, wqkv_ref[...])
        q_ref[rs, :] = (qkv[:, :ATTN_W] * (HEAD_DIM ** -0.5 * LOG2E)).astype(BF16)
        k_ref[rs, :] = qkv[:, ATTN_W:2 * ATTN_W].astype(BF16)
        v_ref[rs, :] = qkv[:, 2 * ATTN_W:].astype(BF16)
        zf_ref[:, rs] = _dot(h, wf_ref[...]).T[:N_HEADS, :]
        uv = _dot(h, wuv_ref[...])
        uv = uv * (0.5 * (1.0 + jnp.tanh(GELU_C * (uv + 0.044715 * (uv * uv * uv)))))
        u_ref[rs, :] = uv[:, :SGU_W].astype(BF16)
        s = uv[:, SGU_W:]
        sc_ = s - jnp.mean(s, axis=-1, keepdims=True)
        sv = sc_ * lax.rsqrt(jnp.mean(sc_ * sc_, axis=-1, keepdims=True) + EPS) * gv_ref[...]
        sv_ref[rs, :] = sv.astype(BF16)


def _inproj(x2, mod_l, g_pre, w_qkv, w_f, w_uv, g_v, seq):
    n, d = x2.shape
    tm = PROJ_TILE
    per_b = seq // tm
    row = lambda i: (i, 0)
    const = lambda i: (0, 0)
    outs = [jax.ShapeDtypeStruct((n, ATTN_W), BF16)] * 3 + [
        jax.ShapeDtypeStruct((N_HEADS, n), F32),
        jax.ShapeDtypeStruct((n, SGU_W), BF16),
        jax.ShapeDtypeStruct((n, SGU_W), BF16)]
    return pl.pallas_call(
        _inproj_kernel,
        grid=(n // tm,),
        in_specs=[
            pl.BlockSpec((tm, d), row),
            pl.BlockSpec((1, N_MOD, d), lambda i: (i // per_b, 0, 0)),
            pl.BlockSpec((1, d), const),
            pl.BlockSpec(w_qkv.shape, const),
            pl.BlockSpec(w_f.shape, const),
            pl.BlockSpec(w_uv.shape, const),
            pl.BlockSpec((1, SGU_W), const),
        ],
        out_specs=[pl.BlockSpec((tm, ATTN_W), row)] * 3 + [
            pl.BlockSpec((N_HEADS, tm), lambda i: (0, i)),
            pl.BlockSpec((tm, SGU_W), row),
            pl.BlockSpec((tm, SGU_W), row)],
        out_shape=outs,
        name="inproj",
    )(x2, mod_l, g_pre, w_qkv, w_f, w_uv, g_v)


def _gates_kernel(z_ref, b_ref, o_ref):
    z = z_ref[...] + b_ref[...]
    ls = jnp.minimum(z, 0.0) - jnp.log1p(jnp.exp(-jnp.abs(z)))
    seq = ls.shape[-1]
    pos = lax.broadcasted_iota(jnp.int32, ls.shape, 1)
    shift = 1
    while shift < seq:
        ls = ls + jnp.where(pos >= shift, pltpu.roll(ls, shift, 1), 0.0)
        shift *= 2
    o_ref[...] = ls * LOG2E


def _forget_cumsum(zf_t, b_forget, seq):
    h, n = zf_t.shape
    return pl.pallas_call(
        _gates_kernel,
        grid=(n // seq,),
        in_specs=[pl.BlockSpec((h, seq), lambda b: (0, b)),
                  pl.BlockSpec((h, 1), lambda b: (0, 0))],
        out_specs=pl.BlockSpec((h, seq), lambda b: (0, b)),
        out_shape=jax.ShapeDtypeStruct((h, n), F32),
        name="forget_cumsum",
    )(zf_t, b_forget.reshape(h, 1))


def _attn_kernel(q_ref, k_ref, v_ref, lk_ref, *rest):
    n_ride = (len(rest) - 1) // 2
    o_ref = rest[n_ride]
    for wsrc_ref, wdst_ref in zip(rest[:n_ride], rest[n_ride + 1:]):
        wdst_ref[...] = wsrc_ref[...].astype(wdst_ref.dtype)
    tq = ATT_TQ
    seq = q_ref.shape[0]
    nq = seq // tq
    l_rows = jnp.concatenate([lk_ref[0], jnp.zeros((LANES - 2, seq), F32)], axis=0)
    l_cols = l_rows.T
    lane = lax.broadcasted_iota(jnp.int32, (tq, LANES), 1)
    left = lane < HEAD_DIM
    rows = lax.broadcasted_iota(jnp.int32, (tq, tq), 0)
    cols = lax.broadcasted_iota(jnp.int32, (tq, tq), 1)
    causal = cols <= rows
    zero = jnp.zeros((tq, LANES), BF16)
    one = jnp.ones((tq, LANES), BF16)
    def scores(qi):
        qs = slice(qi * tq, (qi + 1) * tq)
        q2 = q_ref[qs, :]
        qq = jnp.concatenate([jnp.where(left, q2, zero), jnp.where(left, zero, q2)], axis=0)
        t_blocks = [[], []]
        mt = [None, None]
        for j in range(qi + 1):
            ks = slice(j * tq, (j + 1) * tq)
            t2 = lax.dot_general(qq, k_ref[ks, :], (((1,), (1,)), ((), ())),
                                 preferred_element_type=F32)
            for hh in range(2):
                t = t2[hh * tq:(hh + 1) * tq] - lk_ref[0, hh:hh + 1, ks]
                if j == qi:
                    t = jnp.where(causal, t, -jnp.inf)
                t_blocks[hh].append(t)
                mj = jnp.maximum(t[:, :LANES], t[:, LANES:])
                mt[hh] = mj if mt[hh] is None else jnp.maximum(mt[hh], mj)
        shift = []
        for hh in range(2):
            lq = l_cols[qs, hh:hh + 1]
            row_max = jnp.max(mt[hh], axis=-1, keepdims=True) + lq
            shift.append(row_max - lq)
        return t_blocks, shift

    def values(qi, t_blocks, shift):
        qs = slice(qi * tq, (qi + 1) * tq)
        accs = [None, None]
        for j in range(qi + 1):
            ks = slice(j * tq, (j + 1) * tq)
            pp = jnp.concatenate([jnp.exp2(t_blocks[hh][j] - shift[hh]) for hh in range(2)],
                                 axis=0).astype(BF16)
            vaug = jnp.concatenate([v_ref[ks, :], one], axis=1)
            part = _dot(pp, vaug)
            accs[j % 2] = part if accs[j % 2] is None else accs[j % 2] + part
        acc = accs[0] if accs[1] is None else accs[0] + accs[1]
        out0 = acc[:tq, :LANES] / acc[:tq, LANES:]
        out1 = acc[tq:, :LANES] / acc[tq:, LANES:]
        o_ref[qs, :] = jnp.where(left, out0, out1).astype(o_ref.dtype)

    order = list(range(nq - 1, -1, -1))
    pending = scores(order[0])
    for pos, qi in enumerate(order):
        nxt = scores(order[pos + 1]) if pos + 1 < nq else None
        values(qi, *pending)
        pending = nxt


def _attention(q, k, v, l_pairs, bsz, seq, riders=()):
    n = q.shape[0]
    npair = N_HEADS // 2
    steps = bsz * npair
    blk = pl.BlockSpec((seq, LANES), lambda b, p: (b, p))
    in_specs = [blk, blk, blk, pl.BlockSpec((1, 2, seq), lambda b, p: (p, 0, b))]
    out_specs = [blk]
    out_shape = [jax.ShapeDtypeStruct((n, ATTN_W), BF16)]
    for w in riders:
        rows, cols = w.shape
        share = 1
        while rows % (steps // share) or (rows // (steps // share)) % COMB_ALIGN:
            share *= 2
        wblk = pl.BlockSpec((rows // (steps // share), cols),
                            lambda b, p, share=share: ((b * npair + p) // share, 0))
        in_specs.append(wblk)
        out_specs.append(wblk)
        out_shape.append(jax.ShapeDtypeStruct((rows, cols), BF16))
    return pl.pallas_call(
        _attn_kernel,
        grid=(bsz, npair),
        in_specs=in_specs,
        out_specs=out_specs,
        out_shape=out_shape,
        name="fox_attention",
    )(q, k, v, l_pairs, *riders)


def _sgu_kernel(u_ref, sv_ref, w_ref, bt_ref, g_ref, o_ref):
    tm = u_ref.shape[0]
    npair = N_GROUPS // 2
    lane = lax.broadcasted_iota(jnp.int32, (CHUNK, LANES), 1)
    left = lane < HEAD_DIM
    r = lax.broadcasted_iota(jnp.int32, (CHUNK, CHUNK), 0)
    c = lax.broadcasted_iota(jnp.int32, (CHUNK, CHUNK), 1)
    causal = c <= r
    lhs, bias = [], []
    for j in range(npair):
        wa = jnp.where(causal, w_ref[2 * j], jnp.zeros((), BF16))
        wb = jnp.where(causal, w_ref[2 * j + 1], jnp.zeros((), BF16))
        lhs.append(jnp.concatenate([wa, wb], axis=1))
        bias.append(jnp.where(left, bt_ref[:, 2 * j:2 * j + 1], bt_ref[:, 2 * j + 1:2 * j + 2]))
    for ci in range(tm // CHUNK):
        rs = slice(ci * CHUNK, (ci + 1) * CHUNK)
        blks = []
        ssq = jnp.zeros((CHUNK, 1), F32)
        for j in range(npair):
            cs = slice(j * LANES, (j + 1) * LANES)
            svb = sv_ref[rs, cs]
            zero = jnp.zeros_like(svb)
            rhs = jnp.concatenate([jnp.where(left, svb, zero), jnp.where(left, zero, svb)], axis=0)
            mixed = _dot(lhs[j], rhs) + bias[j]
            ob = u_ref[rs, cs].astype(F32) * mixed
            ssq = ssq + jnp.sum(ob * ob, axis=-1, keepdims=True)
            blks.append(ob)
        inv = lax.rsqrt(ssq * (1.0 / SGU_W) + EPS)
        for j in range(npair):
            cs = slice(j * LANES, (j + 1) * LANES)
            o_ref[rs, cs] = (blks[j] * inv * g_ref[:, cs]).astype(o_ref.dtype)


def _sgu(u, sv, w_sp, b_sp_t, g_out):
    n = u.shape[0]
    tm = TOK_TILE
    row = lambda i: (i, 0)
    return pl.pallas_call(
        _sgu_kernel,
        grid=(n // tm,),
        in_specs=[
            pl.BlockSpec((tm, SGU_W), row),
            pl.BlockSpec((tm, SGU_W), row),
            pl.BlockSpec(w_sp.shape, lambda i: (0, 0, 0)),
            pl.BlockSpec(b_sp_t.shape, lambda i: (0, 0)),
            pl.BlockSpec((1, SGU_W), lambda i: (0, 0)),
        ],
        out_specs=pl.BlockSpec((tm, SGU_W), row),
        out_shape=jax.ShapeDtypeStruct((n, SGU_W), BF16),
        name="sgu",
    )(u, sv, w_sp, b_sp_t, g_out)


def _outproj_kernel(a_ref, s_ref, ga_ref, wa_ref, ws_ref, x_ref, mod_ref, gpost_ref, gffn_ref,
                    *rest, with_router):
    if with_router:
        wr_ref, xo_ref, h_ref, lg_ref = rest
    else:
        xo_ref, h_ref = rest
    gt_m = mod_ref[0, 2:3, :]
    sh_f = mod_ref[0, 3:4, :]
    sc_f = mod_ref[0, 4:5, :]
    if with_router:
        wr = wr_ref[...]
        w_hi = wr.astype(BF16)
        w_lo = (wr - w_hi.astype(F32)).astype(BF16)
    for rs in _row_groups(x_ref.shape[0]):
        a = _rms(a_ref[rs, :].astype(F32), ga_ref[...]).astype(BF16)
        y = _dot(a, wa_ref[...]) + _dot(s_ref[rs, :], ws_ref[...])
        xn = x_ref[rs, :] + (1.0 + gt_m) * _rms(y, gpost_ref[...])
        xo_ref[rs, :] = xn
        h = _rms(xn, gffn_ref[...]) * (1.0 + sc_f) + sh_f
        h_hi = h.astype(BF16)
        h_ref[rs, :] = h_hi
        if with_router:
            h_lo = (h - h_hi.astype(F32)).astype(BF16)
            lg_ref[rs, :] = _dot(h_hi, w_hi) + (_dot(h_hi, w_lo) + _dot(h_lo, w_hi))


def _outproj(attn, sgun, g_attn, w_a, w_s, x2, mod_l, g_post, g_ffn, w_router, seq):
    n, d = x2.shape
    tm = PROJ_TILE
    per_b = seq // tm
    row = lambda i: (i, 0)
    const = lambda i: (0, 0)
    with_router = w_router is not None
    in_specs = [
        pl.BlockSpec((tm, ATTN_W), row),
        pl.BlockSpec((tm, SGU_W), row),
        pl.BlockSpec((1, ATTN_W), const),
        pl.BlockSpec(w_a.shape, const),
        pl.BlockSpec(w_s.shape, const),
        pl.BlockSpec((tm, d), row),
        pl.BlockSpec((1, N_MOD, d), lambda i: (i // per_b, 0, 0)),
        pl.BlockSpec((1, d), const),
        pl.BlockSpec((1, d), const),
    ]
    args = [attn, sgun, g_attn, w_a, w_s, x2, mod_l, g_post, g_ffn]
    out_specs = [pl.BlockSpec((tm, d), row), pl.BlockSpec((tm, d), row)]
    out_shape = [jax.ShapeDtypeStruct((n, d), F32), jax.ShapeDtypeStruct((n, d), BF16)]
    if with_router:
        in_specs.append(pl.BlockSpec(w_router.shape, const))
        args.append(w_router)
        out_specs.append(pl.BlockSpec((tm, LANES), row))
        out_shape.append(jax.ShapeDtypeStruct((n, LANES), F32))
    return pl.pallas_call(
        functools.partial(_outproj_kernel, with_router=with_router),
        grid=(n // tm,),
        in_specs=in_specs,
        out_specs=out_specs,
        out_shape=out_shape,
        name="outproj_router" if with_router else "outproj",
    )(*args)


def _ffn_kernel(h_ref, wgu_ref, wd_ref, x_ref, mod_ref, gpost_ref, o_ref):
    h = h_ref[...]
    y = None
    for a, b in FFN_CHUNKS:
        g = _dot(h, wgu_ref[:, a:b])
        u = _dot(h, wgu_ref[:, D_FF_DENSE + a:D_FF_DENSE + b])
        act = (g * jax.nn.sigmoid(g) * u).astype(BF16)
        part = _dot(act, wd_ref[a:b, :])
        y = part if y is None else y + part
    gt_f = mod_ref[0, 5:6, :]
    o_ref[...] = x_ref[...] + (1.0 + gt_f) * _rms(y, gpost_ref[...])


def _ffn_dense(h2, w_gu, w_down, x2, mod_l, g_post, seq):
    n, d = x2.shape
    tm = TOK_TILE
    per_b = seq // tm
    resident = dict(pipeline_mode=pl.Buffered(1))
    return pl.pallas_call(
        _ffn_kernel,
        grid=(n // tm,),
        in_specs=[
            pl.BlockSpec((tm, d), lambda i: (i, 0)),
            pl.BlockSpec(w_gu.shape, lambda i: (0, 0), **resident),
            pl.BlockSpec(w_down.shape, lambda i: (0, 0), **resident),
            pl.BlockSpec((tm, d), lambda i: (i, 0)),
            pl.BlockSpec((1, N_MOD, d), lambda i: (i // per_b, 0, 0)),
            pl.BlockSpec((1, d), lambda i: (0, 0)),
        ],
        out_specs=pl.BlockSpec((tm, d), lambda i: (i, 0)),
        out_shape=jax.ShapeDtypeStruct((n, d), F32),
        name="ffn_dense",
    )(h2, w_gu, w_down, x2, mod_l, g_post)


def _route_kernel(lg_ref, keyc_ref, gate_ref, keyt_ref, ends_ref, base_ref):
    s = pl.program_id(0)
    tm = lg_ref.shape[0]

    @pl.when(s == 0)
    def _():
        base_ref[...] = jnp.zeros_like(base_ref)

    lane = lax.broadcasted_iota(jnp.int32, (tm, LANES), 1)
    lg = jnp.where(lane < N_EXPERTS, lg_ref[...], -jnp.inf)
    m1 = jnp.max(lg, axis=-1, keepdims=True)
    i1 = jnp.min(jnp.where(lg == m1, lane, LANES), axis=-1, keepdims=True)
    sel1 = lane == i1
    lg2 = jnp.where(sel1, -jnp.inf, lg)
    m2 = jnp.max(lg2, axis=-1, keepdims=True)
    i2 = jnp.min(jnp.where(lg2 == m2, lane, LANES), axis=-1, keepdims=True)
    sel2 = lane == i2
    e2 = jnp.exp(m2 - m1)
    w1 = 1.0 / (1.0 + e2)
    w2 = e2 / (1.0 + e2)
    gate_ref[...] = jnp.where(sel1, w1, 0.0) + jnp.where(sel2, w2, 0.0)
    chosen = jnp.logical_or(sel1, sel2)
    mask = jnp.where(chosen, 1.0, 0.0)
    r = lax.broadcasted_iota(jnp.int32, (tm, tm), 0)
    c = lax.broadcasted_iota(jnp.int32, (tm, tm), 1)
    tri = jnp.where(c < r, 1.0, 0.0).astype(BF16)
    base = base_ref[0:1, :]
    rank = _dot(tri, mask.astype(BF16)) + base
    key = jnp.where(chosen, rank, UNROUTED)
    keyc_ref[...] = key
    keyt_ref[...] = key.T
    new_base = base + jnp.sum(mask, axis=0, keepdims=True)
    base_ref[...] = jnp.broadcast_to(new_base, base_ref.shape)
    ends_ref[0] = jnp.broadcast_to(new_base, ends_ref.shape[1:])


def _route(logits):
    n = logits.shape[0]
    tm = TOK_TILE
    ns = n // tm
    row = lambda s: (s, 0)
    return pl.pallas_call(
        _route_kernel,
        grid=(ns,),
        in_specs=[pl.BlockSpec((tm, LANES), row)],
        out_specs=[
            pl.BlockSpec((tm, LANES), row),
            pl.BlockSpec((tm, LANES), row),
            pl.BlockSpec((LANES, tm), lambda s: (0, s)),
            pl.BlockSpec((1, 8, LANES), lambda s: (s, 0, 0)),
        ],
        out_shape=[
            jax.ShapeDtypeStruct((n, LANES), F32),
            jax.ShapeDtypeStruct((n, LANES), F32),
            jax.ShapeDtypeStruct((LANES, n), F32),
            jax.ShapeDtypeStruct((ns, 8, LANES), F32),
        ],
        scratch_shapes=[pltpu.VMEM((8, LANES), F32)],
        name="moe_route",
    )(logits)


def _work_lists(ends, n_tokens):
    e_n = N_EXPERTS
    sub = EXP_TM // DISP_TD
    max_tiles = (2 * n_tokens) // EXP_TM + e_n
    i32 = jnp.int32

    ends_i = ends[:, 0, :e_n].astype(i32)
    base = jnp.concatenate([jnp.zeros((1, e_n), i32), ends_i], axis=0)
    cnt = base[-1]
    ntile = (cnt + EXP_TM - 1) // EXP_TM
    tile_end = jnp.cumsum(ntile)
    tile_start = tile_end - ntile
    total_tiles = tile_end[-1]
    npass = (cnt + EXP_SUB - 1) // EXP_SUB
    off = tile_start * EXP_TM

    e_ids = jnp.arange(e_n, dtype=i32)

    def expert_of(tile):
        e = jnp.sum((tile_end[None, :] <= tile[:, None]).astype(i32), axis=1)
        return e, (e[:, None] == e_ids[None, :]).astype(i32)

    def pick(onehot, per_expert):
        return jnp.sum(onehot * per_expert[None, :], axis=1)

    t = jnp.arange(max_tiles, dtype=i32)
    tc = jnp.minimum(t, total_tiles - 1)
    tile_e, oh_t = expert_of(tc)
    tile_pass = jnp.clip(pick(oh_t, npass) - (tc - pick(oh_t, tile_start)) * (EXP_TM // EXP_SUB),
                         1, EXP_TM // EXP_SUB)
    tile_pass = jnp.where(t < total_tiles, tile_pass, 0)

    j = jnp.arange(max_tiles * sub, dtype=i32)
    jt = jnp.minimum(j // sub, total_tiles - 1)
    je, oh_j = expert_of(jt)
    jm = (jt - pick(oh_j, tile_start)) * sub + j % sub
    lo_rank = jm * DISP_TD
    hi_rank = jnp.minimum(lo_rank + DISP_TD, pick(oh_j, cnt))
    base_e = jnp.sum(oh_j[:, :, None] * base.T[None, :, :], axis=1)
    s_lo = jnp.sum((base_e[:, 1:] <= lo_rank[:, None]).astype(i32), axis=1)
    s_hi = jnp.sum((base_e[:, :-1] < hi_rank[:, None]).astype(i32), axis=1)
    n_src = jnp.where(j // sub < total_tiles, jnp.maximum(s_hi - s_lo, 0), 0)

    rows = max_tiles * EXP_TM
    start = off[None, :] + base[:-1]
    stop = off[None, :] + base[1:]
    wstart = jnp.minimum(start // COMB_ALIGN * COMB_ALIGN, rows - COMB_WIN)
    need = jnp.where(stop > start, stop - wstart, 0)
    rel = wstart - off[None, :]
    split = COMB_CHUNKS[0][1]
    wtail = jnp.where(need > split, (wstart + split) // COMB_ALIGN, 0)
    return dict(
        disp=(je, jm, s_lo, n_src),
        comb=((wstart // COMB_ALIGN).reshape(-1), wtail.reshape(-1), rel.reshape(-1),
              need.reshape(-1)),
        tiles=(tc, tile_e, tile_pass),
        max_tiles=max_tiles)


def _dispatch_kernel(e_ref, m_ref, slo_ref, n_ref, h_ref, keyt_ref, o_ref, acc_ref):
    j = pl.program_id(0)
    td = o_ref.shape[0]
    acc_ref[...] = jnp.zeros_like(acc_ref)
    e = e_ref[j]
    slot = (m_ref[j] * td + lax.broadcasted_iota(jnp.int32, (td, TOK_TILE), 0)).astype(F32)

    def body(i, carry):
        ts = pl.multiple_of((slo_ref[j] + i) * TOK_TILE, TOK_TILE)
        key = keyt_ref[pl.ds(e, 1), pl.ds(ts, TOK_TILE)]
        onehot = jnp.where(key == slot, 1.0, 0.0).astype(BF16)
        acc_ref[...] += _dot(onehot, h_ref[pl.ds(ts, TOK_TILE), :])
        return carry

    lax.fori_loop(0, n_ref[j], body, 0)
    o_ref[...] = acc_ref[...].astype(o_ref.dtype)


def _dispatch(h2, key_t, lists, max_tiles):
    n, d = h2.shape
    rows = max_tiles * EXP_TM
    resident = dict(pipeline_mode=pl.Buffered(1))
    grid_spec = pltpu.PrefetchScalarGridSpec(
        num_scalar_prefetch=4,
        grid=(rows // DISP_TD,),
        in_specs=[
            pl.BlockSpec((n, d), lambda j, *_: (0, 0), **resident),
            pl.BlockSpec((8, n), lambda j, *_: (0, 0), **resident),
        ],
        out_specs=pl.BlockSpec((DISP_TD, d), lambda j, *_: (j, 0)),
        scratch_shapes=[pltpu.VMEM((DISP_TD, d), F32)],
    )
    return pl.pallas_call(
        _dispatch_kernel,
        grid_spec=grid_spec,
        out_shape=jax.ShapeDtypeStruct((rows, d), BF16),
        name="moe_dispatch",
    )(*lists, h2, key_t)


def _expert_kernel(tidx_ref, te_ref, tp_ref, x_ref, wg_ref, wu_ref, wd_ref, y_ref, acc_ref):
    i = pl.program_id(0)
    f = pl.program_id(1)
    nf = pl.num_programs(1)

    max_pass = EXP_TM // EXP_SUB
    n_steps = D_FF_EXPERT // EXP_TF

    def run(npass, first, last):
        for r in range(max_pass):
            rs = slice(r * EXP_SUB, (r + 1) * EXP_SUB)
            if r >= npass:
                if last:
                    y_ref[rs, :] = jnp.zeros((EXP_SUB, y_ref.shape[1]), y_ref.dtype)
                continue
            x = x_ref[rs, :]
            g = _dot(x, wg_ref[0])
            u = _dot(x, wu_ref[0])
            act = (g * jax.nn.sigmoid(g) * u).astype(BF16)
            part = _dot(act, wd_ref[0])
            total = part if first else acc_ref[rs, :] + part
            if last:
                y_ref[rs, :] = total.astype(y_ref.dtype)
            else:
                acc_ref[rs, :] = total

    kinds = [(f == 0, True, n_steps == 1), (f == nf - 1, n_steps == 1, True)]
    if n_steps > 2:
        kinds.append((jnp.logical_and(f > 0, f < nf - 1), False, False))
    for npass in range(max_pass + 1):
        for cond, first, last in kinds[:1] if n_steps == 1 else kinds:
            @pl.when(jnp.logical_and(tp_ref[i] == npass, cond))
            def _(npass=npass, first=first, last=last):
                run(npass, first, last)


def _experts(xs, w_gu, w_down, tiles, max_tiles):
    rows, d = xs.shape
    tm, tf = EXP_TM, EXP_TF
    nf = D_FF_EXPERT // tf

    def fsel(i, f, tp):
        return jnp.where(tp[i] > 0, f, nf - 1)

    grid_spec = pltpu.PrefetchScalarGridSpec(
        num_scalar_prefetch=3,
        grid=(max_tiles, nf),
        in_specs=[
            pl.BlockSpec((tm, d), lambda i, f, ti, te, tv: (ti[i], 0)),
            pl.BlockSpec((1, d, tf), lambda i, f, ti, te, tv: (te[i], 0, fsel(i, f, tv))),
            pl.BlockSpec((1, d, tf), lambda i, f, ti, te, tv: (te[i], 0, nf + fsel(i, f, tv))),
            pl.BlockSpec((1, tf, d), lambda i, f, ti, te, tv: (te[i], fsel(i, f, tv), 0)),
        ],
        out_specs=pl.BlockSpec((tm, d), lambda i, f, ti, te, tv: (i, 0)),
        scratch_shapes=[pltpu.VMEM((tm, d), F32)],
    )
    return pl.pallas_call(
        _expert_kernel,
        grid_spec=grid_spec,
        out_shape=jax.ShapeDtypeStruct((rows, d), BF16),
        name="moe_experts",
    )(*tiles, xs, w_gu, w_gu, w_down)


def _combine_kernel(ws_ref, wt_ref, rel_ref, need_ref, *refs):
    head_refs = refs[:N_EXPERTS]
    tail_refs = refs[N_EXPERTS:2 * N_EXPERTS]
    keyc_ref, gate_ref, x_ref, mod_ref, gpost_ref, o_ref, acc_ref = refs[2 * N_EXPERTS:]
    s = pl.program_id(0)
    ts = acc_ref.shape[0]

    def term(e, y_ref, a):
        key = keyc_ref[:, e:e + 1]
        rel = rel_ref[s * N_EXPERTS + e]
        slot = (rel + a + lax.broadcasted_iota(jnp.int32, (ts, y_ref.shape[0]), 1)).astype(F32)
        onehot = jnp.where(key == slot, 1.0, 0.0).astype(BF16)
        return gate_ref[:, e:e + 1] * _dot(onehot, y_ref[...])

    acc = term(0, head_refs[0], 0)
    for e in range(1, N_EXPERTS):
        acc = acc + term(e, head_refs[e], 0)
    acc_ref[...] = acc
    split = COMB_CHUNKS[0][1]
    for e in range(N_EXPERTS):
        @pl.when(need_ref[s * N_EXPERTS + e] > split)
        def _(e=e):
            acc_ref[...] += term(e, tail_refs[e], split)
    gt_f = mod_ref[0, 5:6, :]
    o_ref[...] = x_ref[...] + (1.0 + gt_f) * _rms(acc_ref[...], gpost_ref[...])


def _combine(y, key_c, gates, x2, mod_l, g_post, lists, seq):
    n, d = x2.shape
    per_b = seq // TOK_TILE
    tok = lambda s, *_: (s, 0)

    (a0, b0), (a1, b1) = COMB_CHUNKS

    def head(e):
        return pl.BlockSpec((pl.Element(b0 - a0), pl.Element(d)),
                            lambda s, ws, wt, rel, need: (ws[s * N_EXPERTS + e] * COMB_ALIGN, 0))

    def tail(e):
        return pl.BlockSpec((pl.Element(b1 - a1), pl.Element(d)),
                            lambda s, ws, wt, rel, need: (wt[s * N_EXPERTS + e] * COMB_ALIGN, 0))

    grid_spec = pltpu.PrefetchScalarGridSpec(
        num_scalar_prefetch=4,
        grid=(n // TOK_TILE,),
        in_specs=[head(e) for e in range(N_EXPERTS)] + [tail(e) for e in range(N_EXPERTS)] + [
            pl.BlockSpec((TOK_TILE, LANES), tok),
            pl.BlockSpec((TOK_TILE, LANES), tok),
            pl.BlockSpec((TOK_TILE, d), tok),
            pl.BlockSpec((1, N_MOD, d), lambda s, *_: (s // per_b, 0, 0)),
            pl.BlockSpec((1, d), lambda s, *_: (0, 0)),
        ],
        out_specs=pl.BlockSpec((TOK_TILE, d), tok),
        scratch_shapes=[pltpu.VMEM((TOK_TILE, d), F32)],
    )
    return pl.pallas_call(
        _combine_kernel,
        grid_spec=grid_spec,
        out_shape=jax.ShapeDtypeStruct((n, d), F32),
        name="moe_combine",
    )(*lists, *([y] * (2 * N_EXPERTS)), key_c, gates, x2, mod_l, g_post)


def kernel(x, c, w_ada, b_ada, g_pre_mix, g_post_mix, g_pre_ffn, g_post_ffn, w_in, b_forget, g_v,
           w_spatial, b_spatial, g_out_attn, g_out_sgu, w_out, w_gate_up_dense, w_down_dense,
           w_router, w_gate_up_exp, w_down_exp):
    bsz, seq, d = x.shape
    depth = w_ada.shape[0]
    n = bsz * seq
    npair = N_HEADS // 2
    assert d == D_MODEL and seq % PROJ_TILE == 0 and seq % TOK_TILE == 0 and seq % ATT_TQ == 0

    mod = _modulation(c, w_ada, b_ada).reshape(depth, bsz, N_MOD, d)
    x2 = x.reshape(n, d)
    o3 = 3 * ATTN_W
    o4 = o3 + N_HEADS
    bf16 = {("w_in", 0): w_in[0].astype(BF16), ("w_out", 0): w_out[0].astype(BF16)}
    for l in range(depth):
        mod_l = mod[l]
        moe = (l % 2 == 1)
        w_l = bf16[("w_in", l)]
        w_qkv = w_l[:, :o3]
        w_f = jnp.pad(w_l[:, o3:o4], ((0, 0), (0, LANES - N_HEADS)))
        w_uv = w_l[:, o4:]
        q, k, v, zf_t, u, sv = _inproj(x2, mod_l, g_pre_mix[l].reshape(1, d), w_qkv, w_f, w_uv,
                                       g_v[l].reshape(1, SGU_W), seq)
        lcum = _forget_cumsum(zf_t, b_forget[l], seq)
        riders = {}
        if l + 1 < depth:
            riders[("w_in", l + 1)] = w_in[l + 1]
            riders[("w_out", l + 1)] = w_out[l + 1]
        if moe:
            riders[("w_gu_exp", l // 2)] = w_gate_up_exp[l // 2].reshape(N_EXPERTS * d, -1)
        else:
            riders[("w_gu_dense", l // 2)] = w_gate_up_dense[l // 2]
            riders[("w_down_dense", l // 2)] = w_down_dense[l // 2]
            if l + 1 < depth:
                riders[("w_down_exp", l // 2)] = w_down_exp[l // 2].reshape(-1, d)
        attn, *casts = _attention(q, k, v, lcum.reshape(npair, 2, n), bsz, seq,
                                  tuple(riders.values()))
        bf16.update(zip(riders.keys(), casts))
        sgun = _sgu(u, sv, w_spatial[l].astype(BF16), b_spatial[l].T,
                    g_out_sgu[l].reshape(1, SGU_W))
        w_o = bf16[("w_out", l)]
        w_r = None
        if moe:
            w_r = jnp.pad(w_router[l // 2], ((0, 0), (0, LANES - N_EXPERTS)))
        res = _outproj(attn, sgun, g_out_attn[l].reshape(1, ATTN_W), w_o[:ATTN_W], w_o[ATTN_W:],
                       x2, mod_l, g_post_mix[l].reshape(1, d), g_pre_ffn[l].reshape(1, d), w_r, seq)
        g_post = g_post_ffn[l].reshape(1, d)
        if not moe:
            x2, h2 = res
            x2 = _ffn_dense(h2, bf16[("w_gu_dense", l // 2)], bf16[("w_down_dense", l // 2)],
                            x2, mod_l, g_post, seq)
        else:
            x2, h2, logits = res
            key_c, gates, key_t, ends = _route(logits)
            wl = _work_lists(ends, n)
            xs = _dispatch(h2, key_t, wl["disp"], wl["max_tiles"])
            w_gu = bf16[("w_gu_exp", l // 2)].reshape(N_EXPERTS, d, 2 * D_FF_EXPERT)
            w_dn = bf16[("w_down_exp", l // 2)].reshape(N_EXPERTS, D_FF_EXPERT, d)
            y = _experts(xs, w_gu, w_dn, wl["tiles"], wl["max_tiles"])
            x2 = _combine(y, key_c, gates, x2, mod_l, g_post, wl["comb"], seq)
    return x2.reshape(bsz, seq, d)
```

```python
import functools
import math

import jax
import jax.numpy as jnp
from jax import lax
from jax.experimental import pallas as pl
from jax.experimental.pallas import tpu as pltpu

F32 = jnp.float32
BF16 = jnp.bfloat16

D_MODEL = 1024
HEAD_DIM = 64
N_HEADS = 8
N_GROUPS = 8
ATTN_W = N_HEADS * HEAD_DIM
SGU_W = N_GROUPS * HEAD_DIM
CHUNK = 128
D_FF_DENSE = 2816
N_EXPERTS = 8
D_FF_EXPERT = 3584
N_MOD = 6
EPS = 1e-6

LANES = 128
TOK_TILE = 512
PROJ_TILE = 1024
ROW_GROUP = 256
ATT_TQ = 256
MXU_TILE = 256
FFN_CHUNKS = ((0, 6 * MXU_TILE), (6 * MXU_TILE, D_FF_DENSE))
EXP_TM = 1024
EXP_SUB = 512
EXP_TF = 7 * MXU_TILE
LOG2E = math.log2(math.e)
DISP_TD = 256
COMB_ALIGN = 16
COMB_WIN = TOK_TILE + COMB_ALIGN
COMB_CHUNKS = ((0, MXU_TILE), (MXU_TILE, COMB_WIN))
UNROUTED = -float(2 ** 30)
GELU_C = math.sqrt(2.0 / math.pi)


def _rms(x, g):
    return x * lax.rsqrt(jnp.mean(x * x, axis=-1, keepdims=True) + EPS) * g


def _dot(a, b):
    return jnp.dot(a, b, preferred_element_type=F32)


def _row_groups(rows):
    return [slice(r, r + ROW_GROUP) for r in range(0, rows, ROW_GROUP)]


def _mod_kernel(c_ref, w_ref, b_ref, o_ref):
    c = c_ref[...]
    ca = (c * jax.nn.sigmoid(c)).astype(BF16)
    o_ref[0] = _dot(ca, w_ref[0].astype(BF16)) + b_ref[0]


def _modulation(c, w_ada, b_ada):
    depth, d, m = w_ada.shape
    bsz = c.shape[0]
    nblk = m // d
    return pl.pallas_call(
        _mod_kernel,
        grid=(depth, nblk),
        in_specs=[
            pl.BlockSpec((bsz, d), lambda l, j: (0, 0)),
            pl.BlockSpec((1, d, d), lambda l, j: (l, 0, j)),
            pl.BlockSpec((1, 1, d), lambda l, j: (l, 0, j)),
        ],
        out_specs=pl.BlockSpec((1, bsz, d), lambda l, j: (l, 0, j)),
        out_shape=jax.ShapeDtypeStruct((depth, bsz, m), F32),
        name="adaln_mod",
    )(c, w_ada, b_ada.reshape(depth, 1, m))


def _inproj_kernel(x_ref, mod_ref, gpre_ref, w_ref, gv_ref,
                   q_ref, k_ref, v_ref, zf_ref, u_ref, sv_ref, wuv_ref):
    o3 = 3 * ATTN_W
    o4 = o3 + N_HEADS

    @pl.when(pl.program_id(0) == 0)
    def _():
        wuv_ref[...] = w_ref[:, o4:]

    sh = mod_ref[0, 0:1, :]
    sc = mod_ref[0, 1:2, :]
    for rs in _row_groups(x_ref.shape[0]):
        h = (_rms(x_ref[rs, :], gpre_ref[...]) * (1.0 + sc) + sh).astype(BF16)
        qkv = _dot(h, w_ref[:, :o3])
        q_ref[rs, :] = (qkv[:, :ATTN_W] * (HEAD_DIM ** -0.5 * LOG2E)).astype(BF16)
        k_ref[rs, :] = qkv[:, ATTN_W:2 * ATTN_W].astype(BF16)
        v_ref[rs, :] = qkv[:, 2 * ATTN_W:].astype(BF16)
        zf_ref[:, rs] = _dot(h, w_ref[:, o3:o3 + LANES]).T[:N_HEADS, :]
        uv = _dot(h, wuv_ref[...])
        uv = uv * (0.5 * (1.0 + jnp.tanh(GELU_C * (uv + 0.044715 * (uv * uv * uv)))))
        u_ref[rs, :] = uv[:, :SGU_W].astype(BF16)
        s = uv[:, SGU_W:]
        sc_ = s - jnp.mean(s, axis=-1, keepdims=True)
        sv = sc_ * lax.rsqrt(jnp.mean(sc_ * sc_, axis=-1, keepdims=True) + EPS) * gv_ref[...]
        sv_ref[rs, :] = sv.astype(BF16)


def _inproj(x2, mod_l, g_pre, w_in_l, g_v, seq):
    n, d = x2.shape
    tm = PROJ_TILE
    per_b = seq // tm
    row = lambda i: (i, 0)
    const = lambda i: (0, 0)
    resident = dict(pipeline_mode=pl.Buffered(1))
    outs = [jax.ShapeDtypeStruct((n, ATTN_W), BF16)] * 3 + [
        jax.ShapeDtypeStruct((N_HEADS, n), F32),
        jax.ShapeDtypeStruct((n, SGU_W), BF16),
        jax.ShapeDtypeStruct((n, SGU_W), BF16)]
    return pl.pallas_call(
        _inproj_kernel,
        grid=(n // tm,),
        in_specs=[
            pl.BlockSpec((tm, d), row),
            pl.BlockSpec((1, N_MOD, d), lambda i: (i // per_b, 0, 0)),
            pl.BlockSpec((1, d), const),
            pl.BlockSpec(w_in_l.shape, const, **resident),
            pl.BlockSpec((1, SGU_W), const),
        ],
        out_specs=[pl.BlockSpec((tm, ATTN_W), row)] * 3 + [
            pl.BlockSpec((N_HEADS, tm), lambda i: (0, i)),
            pl.BlockSpec((tm, SGU_W), row),
            pl.BlockSpec((tm, SGU_W), row)],
        out_shape=outs,
        scratch_shapes=[pltpu.VMEM((d, 2 * SGU_W), BF16)],
        name="inproj",
    )(x2, mod_l, g_pre, w_in_l, g_v)


def _gates_kernel(z_ref, b_ref, o_ref):
    z = z_ref[...] + b_ref[...]
    ls = jnp.minimum(z, 0.0) - jnp.log1p(jnp.exp(-jnp.abs(z)))
    seq = ls.shape[-1]
    pos = lax.broadcasted_iota(jnp.int32, ls.shape, 1)
    shift = 1
    while shift < seq:
        ls = ls + jnp.where(pos >= shift, pltpu.roll(ls, shift, 1), 0.0)
        shift *= 2
    o_ref[...] = ls * LOG2E


def _forget_cumsum(zf_t, b_forget, seq):
    h, n = zf_t.shape
    return pl.pallas_call(
        _gates_kernel,
        grid=(n // seq,),
        in_specs=[pl.BlockSpec((h, seq), lambda b: (0, b)),
                  pl.BlockSpec((h, 1), lambda b: (0, 0))],
        out_specs=pl.BlockSpec((h, seq), lambda b: (0, b)),
        out_shape=jax.ShapeDtypeStruct((h, n), F32),
        name="forget_cumsum",
    )(zf_t, b_forget.reshape(h, 1))


def _attn_kernel(q_ref, k_ref, v_ref, lk_ref, *rest):
    n_ride = (len(rest) - 1) // 2
    o_ref = rest[n_ride]
    for wsrc_ref, wdst_ref in zip(rest[:n_ride], rest[n_ride + 1:]):
        wdst_ref[...] = wsrc_ref[...].astype(wdst_ref.dtype)
    tq = ATT_TQ
    seq = q_ref.shape[0]
    nq = seq // tq
    l_rows = jnp.concatenate([lk_ref[0], jnp.zeros((LANES - 2, seq), F32)], axis=0)
    l_cols = l_rows.T
    lane = lax.broadcasted_iota(jnp.int32, (tq, LANES), 1)
    left = lane < HEAD_DIM
    rows = lax.broadcasted_iota(jnp.int32, (tq, tq), 0)
    cols = lax.broadcasted_iota(jnp.int32, (tq, tq), 1)
    causal = cols <= rows
    zero = jnp.zeros((tq, LANES), BF16)
    one = jnp.ones((tq, LANES), BF16)
    def scores(qi):
        qs = slice(qi * tq, (qi + 1) * tq)
        q2 = q_ref[qs, :]
        qq = jnp.concatenate([jnp.where(left, q2, zero), jnp.where(left, zero, q2)], axis=0)
        t_blocks = [[], []]
        mt = [None, None]
        for j in range(qi + 1):
            ks = slice(j * tq, (j + 1) * tq)
            t2 = lax.dot_general(qq, k_ref[ks, :], (((1,), (1,)), ((), ())),
                                 preferred_element_type=F32)
            for hh in range(2):
                t = t2[hh * tq:(hh + 1) * tq] - lk_ref[0, hh:hh + 1, ks]
                if j == qi:
                    t = jnp.where(causal, t, -jnp.inf)
                t_blocks[hh].append(t)
                mj = jnp.maximum(t[:, :LANES], t[:, LANES:])
                mt[hh] = mj if mt[hh] is None else jnp.maximum(mt[hh], mj)
        shift = []
        for hh in range(2):
            lq = l_cols[qs, hh:hh + 1]
            row_max = jnp.max(mt[hh], axis=-1, keepdims=True) + lq
            shift.append(row_max - lq)
        return t_blocks, shift

    def values(qi, t_blocks, shift):
        qs = slice(qi * tq, (qi + 1) * tq)
        accs = [None, None]
        for j in range(qi + 1):
            ks = slice(j * tq, (j + 1) * tq)
            pp = jnp.concatenate([jnp.exp2(t_blocks[hh][j] - shift[hh]) for hh in range(2)],
                                 axis=0).astype(BF16)
            vaug = jnp.concatenate([v_ref[ks, :], one], axis=1)
            part = _dot(pp, vaug)
            accs[j % 2] = part if accs[j % 2] is None else accs[j % 2] + part
        acc = accs[0] if accs[1] is None else accs[0] + accs[1]
        out0 = acc[:tq, :LANES] / acc[:tq, LANES:]
        out1 = acc[tq:, :LANES] / acc[tq:, LANES:]
        o_ref[qs, :] = jnp.where(left, out0, out1).astype(o_ref.dtype)

    order = list(range(nq - 1, -1, -1))
    pending = scores(order[0])
    for pos, qi in enumerate(order):
        nxt = scores(order[pos + 1]) if pos + 1 < nq else None
        values(qi, *pending)
        pending = nxt


def _attention(q, k, v, l_pairs, bsz, seq, riders=()):
    n = q.shape[0]
    npair = N_HEADS // 2
    steps = bsz * npair
    blk = pl.BlockSpec((seq, LANES), lambda b, p: (b, p))
    in_specs = [blk, blk, blk, pl.BlockSpec((1, 2, seq), lambda b, p: (p, 0, b))]
    out_specs = [blk]
    out_shape = [jax.ShapeDtypeStruct((n, ATTN_W), BF16)]
    for w in riders:
        rows, cols = w.shape
        share = 1
        while rows % (steps // share) or (rows // (steps // share)) % COMB_ALIGN:
            share *= 2
        wblk = pl.BlockSpec((rows // (steps // share), cols),
                            lambda b, p, share=share: ((b * npair + p) // share, 0))
        in_specs.append(wblk)
        out_specs.append(wblk)
        out_shape.append(jax.ShapeDtypeStruct((rows, cols), BF16))
    return pl.pallas_call(
        _attn_kernel,
        grid=(bsz, npair),
        in_specs=in_specs,
        out_specs=out_specs,
        out_shape=out_shape,
        name="fox_attention",
    )(q, k, v, l_pairs, *riders)


def _sgu_kernel(u_ref, sv_ref, w_ref, bt_ref, g_ref, o_ref):
    tm = u_ref.shape[0]
    npair = N_GROUPS // 2
    lane = lax.broadcasted_iota(jnp.int32, (CHUNK, LANES), 1)
    left = lane < HEAD_DIM
    r = lax.broadcasted_iota(jnp.int32, (CHUNK, CHUNK), 0)
    c = lax.broadcasted_iota(jnp.int32, (CHUNK, CHUNK), 1)
    causal = c <= r
    lhs, bias = [], []
    for j in range(npair):
        wa = jnp.where(causal, w_ref[2 * j], jnp.zeros((), BF16))
        wb = jnp.where(causal, w_ref[2 * j + 1], jnp.zeros((), BF16))
        lhs.append(jnp.concatenate([wa, wb], axis=1))
        bias.append(jnp.where(left, bt_ref[:, 2 * j:2 * j + 1], bt_ref[:, 2 * j + 1:2 * j + 2]))
    for ci in range(tm // CHUNK):
        rs = slice(ci * CHUNK, (ci + 1) * CHUNK)
        blks = []
        ssq = jnp.zeros((CHUNK, 1), F32)
        for j in range(npair):
            cs = slice(j * LANES, (j + 1) * LANES)
            svb = sv_ref[rs, cs]
            zero = jnp.zeros_like(svb)
            rhs = jnp.concatenate([jnp.where(left, svb, zero), jnp.where(left, zero, svb)], axis=0)
            mixed = _dot(lhs[j], rhs) + bias[j]
            ob = u_ref[rs, cs].astype(F32) * mixed
            ssq = ssq + jnp.sum(ob * ob, axis=-1, keepdims=True)
            blks.append(ob)
        inv = lax.rsqrt(ssq * (1.0 / SGU_W) + EPS)
        for j in range(npair):
            cs = slice(j * LANES, (j + 1) * LANES)
            o_ref[rs, cs] = (blks[j] * inv * g_ref[:, cs]).astype(o_ref.dtype)


def _sgu(u, sv, w_sp, b_sp_t, g_out):
    n = u.shape[0]
    tm = TOK_TILE
    row = lambda i: (i, 0)
    return pl.pallas_call(
        _sgu_kernel,
        grid=(n // tm,),
        in_specs=[
            pl.BlockSpec((tm, SGU_W), row),
            pl.BlockSpec((tm, SGU_W), row),
            pl.BlockSpec(w_sp.shape, lambda i: (0, 0, 0)),
            pl.BlockSpec(b_sp_t.shape, lambda i: (0, 0)),
            pl.BlockSpec((1, SGU_W), lambda i: (0, 0)),
        ],
        out_specs=pl.BlockSpec((tm, SGU_W), row),
        out_shape=jax.ShapeDtypeStruct((n, SGU_W), BF16),
        name="sgu",
    )(u, sv, w_sp, b_sp_t, g_out)


def _outproj_kernel(a_ref, s_ref, ga_ref, wo_ref, x_ref, mod_ref, gpost_ref, gffn_ref,
                    *rest, with_router):
    if with_router:
        wr_ref, xo_ref, h_ref, lg_ref = rest
    else:
        xo_ref, h_ref = rest
    gt_m = mod_ref[0, 2:3, :]
    sh_f = mod_ref[0, 3:4, :]
    sc_f = mod_ref[0, 4:5, :]
    if with_router:
        wr = wr_ref[...]
        w_hi = wr.astype(BF16)
        w_lo = (wr - w_hi.astype(F32)).astype(BF16)
    for rs in _row_groups(x_ref.shape[0]):
        a = _rms(a_ref[rs, :].astype(F32), ga_ref[...]).astype(BF16)
        y = _dot(a, wo_ref[:ATTN_W, :]) + _dot(s_ref[rs, :], wo_ref[ATTN_W:, :])
        xn = x_ref[rs, :] + (1.0 + gt_m) * _rms(y, gpost_ref[...])
        xo_ref[rs, :] = xn
        h = _rms(xn, gffn_ref[...]) * (1.0 + sc_f) + sh_f
        h_hi = h.astype(BF16)
        h_ref[rs, :] = h_hi
        if with_router:
            h_lo = (h - h_hi.astype(F32)).astype(BF16)
            lg_ref[rs, :] = _dot(h_hi, w_hi) + (_dot(h_hi, w_lo) + _dot(h_lo, w_hi))


def _outproj(attn, sgun, g_attn, w_o, x2, mod_l, g_post, g_ffn, w_router, seq):
    n, d = x2.shape
    tm = PROJ_TILE
    per_b = seq // tm
    row = lambda i: (i, 0)
    const = lambda i: (0, 0)
    with_router = w_router is not None
    in_specs = [
        pl.BlockSpec((tm, ATTN_W), row),
        pl.BlockSpec((tm, SGU_W), row),
        pl.BlockSpec((1, ATTN_W), const),
        pl.BlockSpec(w_o.shape, const),
        pl.BlockSpec((tm, d), row),
        pl.BlockSpec((1, N_MOD, d), lambda i: (i // per_b, 0, 0)),
        pl.BlockSpec((1, d), const),
        pl.BlockSpec((1, d), const),
    ]
    args = [attn, sgun, g_attn, w_o, x2, mod_l, g_post, g_ffn]
    out_specs = [pl.BlockSpec((tm, d), row), pl.BlockSpec((tm, d), row)]
    out_shape = [jax.ShapeDtypeStruct((n, d), F32), jax.ShapeDtypeStruct((n, d), BF16)]
    if with_router:
        in_specs.append(pl.BlockSpec(w_router.shape, const))
        args.append(w_router)
        out_specs.append(pl.BlockSpec((tm, LANES), row))
        out_shape.append(jax.ShapeDtypeStruct((n, LANES), F32))
    return pl.pallas_call(
        functools.partial(_outproj_kernel, with_router=with_router),
        grid=(n // tm,),
        in_specs=in_specs,
        out_specs=out_specs,
        out_shape=out_shape,
        name="outproj_router" if with_router else "outproj",
    )(*args)


def _ffn_kernel(h_ref, wgu_ref, wd_ref, x_ref, mod_ref, gpost_ref, o_ref):
    h = h_ref[...]
    y = None
    for a, b in FFN_CHUNKS:
        g = _dot(h, wgu_ref[:, a:b])
        u = _dot(h, wgu_ref[:, D_FF_DENSE + a:D_FF_DENSE + b])
        act = (g * jax.nn.sigmoid(g) * u).astype(BF16)
        part = _dot(act, wd_ref[a:b, :])
        y = part if y is None else y + part
    gt_f = mod_ref[0, 5:6, :]
    o_ref[...] = x_ref[...] + (1.0 + gt_f) * _rms(y, gpost_ref[...])


def _ffn_dense(h2, w_gu, w_down, x2, mod_l, g_post, seq):
    n, d = x2.shape
    tm = TOK_TILE
    per_b = seq // tm
    resident = dict(pipeline_mode=pl.Buffered(1))
    return pl.pallas_call(
        _ffn_kernel,
        grid=(n // tm,),
        in_specs=[
            pl.BlockSpec((tm, d), lambda i: (i, 0)),
            pl.BlockSpec(w_gu.shape, lambda i: (0, 0), **resident),
            pl.BlockSpec(w_down.shape, lambda i: (0, 0), **resident),
            pl.BlockSpec((tm, d), lambda i: (i, 0)),
            pl.BlockSpec((1, N_MOD, d), lambda i: (i // per_b, 0, 0)),
            pl.BlockSpec((1, d), lambda i: (0, 0)),
        ],
        out_specs=pl.BlockSpec((tm, d), lambda i: (i, 0)),
        out_shape=jax.ShapeDtypeStruct((n, d), F32),
        name="ffn_dense",
    )(h2, w_gu, w_down, x2, mod_l, g_post)


def _route_kernel(lg_ref, keyc_ref, gate_ref, keyt_ref, ends_ref, base_ref):
    s = pl.program_id(0)
    tm = lg_ref.shape[0]

    @pl.when(s == 0)
    def _():
        base_ref[...] = jnp.zeros_like(base_ref)

    lane = lax.broadcasted_iota(jnp.int32, (tm, LANES), 1)
    lg = jnp.where(lane < N_EXPERTS, lg_ref[...], -jnp.inf)
    m1 = jnp.max(lg, axis=-1, keepdims=True)
    i1 = jnp.min(jnp.where(lg == m1, lane, LANES), axis=-1, keepdims=True)
    sel1 = lane == i1
    lg2 = jnp.where(sel1, -jnp.inf, lg)
    m2 = jnp.max(lg2, axis=-1, keepdims=True)
    i2 = jnp.min(jnp.where(lg2 == m2, lane, LANES), axis=-1, keepdims=True)
    sel2 = lane == i2
    e2 = jnp.exp(m2 - m1)
    w1 = 1.0 / (1.0 + e2)
    w2 = e2 / (1.0 + e2)
    gate_ref[...] = (jnp.where(sel1, w1, 0.0) + jnp.where(sel2, w2, 0.0)).T
    chosen = jnp.logical_or(sel1, sel2)
    mask = jnp.where(chosen, 1.0, 0.0)
    r = lax.broadcasted_iota(jnp.int32, (tm, tm), 0)
    c = lax.broadcasted_iota(jnp.int32, (tm, tm), 1)
    tri = jnp.where(c < r, 1.0, 0.0).astype(BF16)
    base = base_ref[0:1, :]
    rank = _dot(tri, mask.astype(BF16)) + base
    key = jnp.where(chosen, rank, UNROUTED)
    keyc_ref[...] = key
    keyt_ref[...] = key.T
    new_base = base + jnp.sum(mask, axis=0, keepdims=True)
    base_ref[...] = jnp.broadcast_to(new_base, base_ref.shape)
    ends_ref[0] = jnp.broadcast_to(new_base, ends_ref.shape[1:])


def _route(logits):
    n = logits.shape[0]
    tm = TOK_TILE
    ns = n // tm
    row = lambda s: (s, 0)
    return pl.pallas_call(
        _route_kernel,
        grid=(ns,),
        in_specs=[pl.BlockSpec((tm, LANES), row)],
        out_specs=[
            pl.BlockSpec((tm, LANES), row),
            pl.BlockSpec((LANES, tm), lambda s: (0, s)),
            pl.BlockSpec((LANES, tm), lambda s: (0, s)),
            pl.BlockSpec((1, 8, LANES), lambda s: (s, 0, 0)),
        ],
        out_shape=[
            jax.ShapeDtypeStruct((n, LANES), F32),
            jax.ShapeDtypeStruct((LANES, n), F32),
            jax.ShapeDtypeStruct((LANES, n), F32),
            jax.ShapeDtypeStruct((ns, 8, LANES), F32),
        ],
        scratch_shapes=[pltpu.VMEM((8, LANES), F32)],
        name="moe_route",
    )(logits)


def _work_lists(ends, n_tokens):
    e_n = N_EXPERTS
    sub = EXP_TM // DISP_TD
    max_tiles = (2 * n_tokens) // EXP_TM + e_n
    i32 = jnp.int32

    ends_i = ends[:, 0, :e_n].astype(i32)
    base = jnp.concatenate([jnp.zeros((1, e_n), i32), ends_i], axis=0)
    cnt = base[-1]
    ntile = (cnt + EXP_TM - 1) // EXP_TM
    tile_end = jnp.cumsum(ntile)
    tile_start = tile_end - ntile
    total_tiles = tile_end[-1]
    npass = (cnt + EXP_SUB - 1) // EXP_SUB
    off = tile_start * EXP_TM

    e_ids = jnp.arange(e_n, dtype=i32)

    def expert_of(tile):
        e = jnp.sum((tile_end[None, :] <= tile[:, None]).astype(i32), axis=1)
        return e, (e[:, None] == e_ids[None, :]).astype(i32)

    def pick(onehot, per_expert):
        return jnp.sum(onehot * per_expert[None, :], axis=1)

    t = jnp.arange(max_tiles, dtype=i32)
    tc = jnp.minimum(t, total_tiles - 1)
    tile_e, oh_t = expert_of(tc)
    tile_pass = jnp.clip(pick(oh_t, npass) - (tc - pick(oh_t, tile_start)) * (EXP_TM // EXP_SUB),
                         1, EXP_TM // EXP_SUB)
    tile_pass = jnp.where(t < total_tiles, tile_pass, 0)

    j = jnp.arange(max_tiles * sub, dtype=i32)
    jt = jnp.minimum(j // sub, total_tiles - 1)
    je, oh_j = expert_of(jt)
    jm = (jt - pick(oh_j, tile_start)) * sub + j % sub
    lo_rank = jm * DISP_TD
    hi_rank = jnp.minimum(lo_rank + DISP_TD, pick(oh_j, cnt))
    base_e = jnp.sum(oh_j[:, :, None] * base.T[None, :, :], axis=1)
    s_lo = jnp.sum((base_e[:, 1:] <= lo_rank[:, None]).astype(i32), axis=1)
    s_hi = jnp.sum((base_e[:, :-1] < hi_rank[:, None]).astype(i32), axis=1)
    n_src = jnp.where(j // sub < total_tiles, jnp.maximum(s_hi - s_lo, 0), 0)

    rows = max_tiles * EXP_TM
    start = off[None, :] + base[:-1]
    stop = off[None, :] + base[1:]
    wstart = jnp.minimum(start // COMB_ALIGN * COMB_ALIGN, rows - COMB_WIN)
    need = jnp.where(stop > start, stop - wstart, 0)
    rel = wstart - off[None, :]
    split = COMB_CHUNKS[0][1]
    wtail = jnp.where(need > split, (wstart + split) // COMB_ALIGN, 0)
    return dict(
        disp=(je, jm, s_lo, n_src),
        comb=((wstart // COMB_ALIGN).reshape(-1), wtail.reshape(-1), rel.reshape(-1),
              need.reshape(-1)),
        tiles=(tc, tile_e, tile_pass),
        max_tiles=max_tiles)


def _dispatch_kernel(e_ref, m_ref, slo_ref, n_ref, h_ref, keyt_ref, gatet_ref, o_ref, gs_ref,
                     acc_ref, gacc_ref):
    j = pl.program_id(0)
    td = o_ref.shape[0]
    acc_ref[...] = jnp.zeros_like(acc_ref)
    gacc_ref[...] = jnp.zeros_like(gacc_ref)
    e = e_ref[j]
    slot = (m_ref[j] * td + lax.broadcasted_iota(jnp.int32, (td, TOK_TILE), 0)).astype(F32)

    def body(i, carry):
        ts = pl.multiple_of((slo_ref[j] + i) * TOK_TILE, TOK_TILE)
        key = keyt_ref[pl.ds(e, 1), pl.ds(ts, TOK_TILE)]
        hit = key == slot
        onehot = jnp.where(hit, 1.0, 0.0).astype(BF16)
        acc_ref[...] += _dot(onehot, h_ref[pl.ds(ts, TOK_TILE), :])
        gate = gatet_ref[pl.ds(e, 1), pl.ds(ts, TOK_TILE)]
        gacc_ref[...] += jnp.sum(jnp.where(hit, gate, 0.0), axis=-1, keepdims=True)
        return carry

    lax.fori_loop(0, n_ref[j], body, 0)
    o_ref[...] = acc_ref[...].astype(o_ref.dtype)
    gs_ref[...] = gacc_ref[...]


def _dispatch(h2, key_t, gate_t, lists, max_tiles):
    n, d = h2.shape
    rows = max_tiles * EXP_TM
    resident = dict(pipeline_mode=pl.Buffered(1))
    grid_spec = pltpu.PrefetchScalarGridSpec(
        num_scalar_prefetch=4,
        grid=(rows // DISP_TD,),
        in_specs=[
            pl.BlockSpec((n, d), lambda j, *_: (0, 0), **resident),
            pl.BlockSpec((8, n), lambda j, *_: (0, 0), **resident),
            pl.BlockSpec((8, n), lambda j, *_: (0, 0), **resident),
        ],
        out_specs=[pl.BlockSpec((DISP_TD, d), lambda j, *_: (j, 0)),
                   pl.BlockSpec((DISP_TD, 1), lambda j, *_: (j, 0))],
        scratch_shapes=[pltpu.VMEM((DISP_TD, d), F32), pltpu.VMEM((DISP_TD, 1), F32)],
    )
    return pl.pallas_call(
        _dispatch_kernel,
        grid_spec=grid_spec,
        out_shape=[jax.ShapeDtypeStruct((rows, d), BF16),
                   jax.ShapeDtypeStruct((rows, 1), F32)],
        name="moe_dispatch",
    )(*lists, h2, key_t, gate_t)


def _expert_kernel(tidx_ref, te_ref, tp_ref, x_ref, gs_ref, wg_ref, wu_ref, wd_ref, y_ref,
                   acc_ref):
    i = pl.program_id(0)
    f = pl.program_id(1)
    nf = pl.num_programs(1)

    max_pass = EXP_TM // EXP_SUB
    n_steps = D_FF_EXPERT // EXP_TF

    def run(npass, first, last):
        for r in range(max_pass):
            rs = slice(r * EXP_SUB, (r + 1) * EXP_SUB)
            if r >= npass:
                if last:
                    y_ref[rs, :] = jnp.zeros((EXP_SUB, y_ref.shape[1]), y_ref.dtype)
                continue
            x = x_ref[rs, :]
            g = _dot(x, wg_ref[0])
            u = _dot(x, wu_ref[0])
            act = (g * jax.nn.sigmoid(g) * u).astype(BF16)
            part = _dot(act, wd_ref[0])
            total = part if first else acc_ref[rs, :] + part
            if last:
                y_ref[rs, :] = (gs_ref[rs, :] * total).astype(y_ref.dtype)
            else:
                acc_ref[rs, :] = total

    kinds = [(f == 0, True, n_steps == 1), (f == nf - 1, n_steps == 1, True)]
    if n_steps > 2:
        kinds.append((jnp.logical_and(f > 0, f < nf - 1), False, False))
    for npass in range(max_pass + 1):
        for cond, first, last in kinds[:1] if n_steps == 1 else kinds:
            @pl.when(jnp.logical_and(tp_ref[i] == npass, cond))
            def _(npass=npass, first=first, last=last):
                run(npass, first, last)


def _experts(xs, gs, w_gu, w_down, tiles, max_tiles):
    rows, d = xs.shape
    tm, tf = EXP_TM, EXP_TF
    nf = D_FF_EXPERT // tf

    def fsel(i, f, tp):
        return jnp.where(tp[i] > 0, f, nf - 1)

    grid_spec = pltpu.PrefetchScalarGridSpec(
        num_scalar_prefetch=3,
        grid=(max_tiles, nf),
        in_specs=[
            pl.BlockSpec((tm, d), lambda i, f, ti, te, tv: (ti[i], 0)),
            pl.BlockSpec((tm, 1), lambda i, f, ti, te, tv: (ti[i], 0)),
            pl.BlockSpec((1, d, tf), lambda i, f, ti, te, tv: (te[i], 0, fsel(i, f, tv))),
            pl.BlockSpec((1, d, tf), lambda i, f, ti, te, tv: (te[i], 0, nf + fsel(i, f, tv))),
            pl.BlockSpec((1, tf, d), lambda i, f, ti, te, tv: (te[i], fsel(i, f, tv), 0)),
        ],
        out_specs=pl.BlockSpec((tm, d), lambda i, f, ti, te, tv: (i, 0)),
        scratch_shapes=[pltpu.VMEM((tm, d), F32)],
    )
    return pl.pallas_call(
        _expert_kernel,
        grid_spec=grid_spec,
        out_shape=jax.ShapeDtypeStruct((rows, d), BF16),
        name="moe_experts",
    )(*tiles, xs, gs, w_gu, w_gu, w_down)


def _combine_kernel(ws_ref, wt_ref, rel_ref, need_ref, *refs):
    head_refs = refs[:N_EXPERTS]
    tail_refs = refs[N_EXPERTS:2 * N_EXPERTS]
    keyc_ref, x_ref, mod_ref, gpost_ref, o_ref, acc_ref = refs[2 * N_EXPERTS:]
    s = pl.program_id(0)
    ts = acc_ref.shape[0]

    def term(e, y_ref, a):
        key = keyc_ref[:, e:e + 1]
        rel = rel_ref[s * N_EXPERTS + e]
        slot = (rel + a + lax.broadcasted_iota(jnp.int32, (ts, y_ref.shape[0]), 1)).astype(F32)
        onehot = jnp.where(key == slot, 1.0, 0.0).astype(BF16)
        return _dot(onehot, y_ref[...])

    acc = term(0, head_refs[0], 0)
    for e in range(1, N_EXPERTS):
        acc = acc + term(e, head_refs[e], 0)
    acc_ref[...] = acc
    split = COMB_CHUNKS[0][1]
    for e in range(N_EXPERTS):
        @pl.when(need_ref[s * N_EXPERTS + e] > split)
        def _(e=e):
            acc_ref[...] += term(e, tail_refs[e], split)
    gt_f = mod_ref[0, 5:6, :]
    o_ref[...] = x_ref[...] + (1.0 + gt_f) * _rms(acc_ref[...], gpost_ref[...])


def _combine(y, key_c, x2, mod_l, g_post, lists, seq):
    n, d = x2.shape
    per_b = seq // TOK_TILE
    tok = lambda s, *_: (s, 0)

    (a0, b0), (a1, b1) = COMB_CHUNKS

    def head(e):
        return pl.BlockSpec((pl.Element(b0 - a0), pl.Element(d)),
                            lambda s, ws, wt, rel, need: (ws[s * N_EXPERTS + e] * COMB_ALIGN, 0))

    def tail(e):
        return pl.BlockSpec((pl.Element(b1 - a1), pl.Element(d)),
                            lambda s, ws, wt, rel, need: (wt[s * N_EXPERTS + e] * COMB_ALIGN, 0))

    grid_spec = pltpu.PrefetchScalarGridSpec(
        num_scalar_prefetch=4,
        grid=(n // TOK_TILE,),
        in_specs=[head(e) for e in range(N_EXPERTS)] + [tail(e) for e in range(N_EXPERTS)] + [
            pl.BlockSpec((TOK_TILE, LANES), tok),
            pl.BlockSpec((TOK_TILE, d), tok),
            pl.BlockSpec((1, N_MOD, d), lambda s, *_: (s // per_b, 0, 0)),
            pl.BlockSpec((1, d), lambda s, *_: (0, 0)),
        ],
        out_specs=pl.BlockSpec((TOK_TILE, d), tok),
        scratch_shapes=[pltpu.VMEM((TOK_TILE, d), F32)],
    )
    return pl.pallas_call(
        _combine_kernel,
        grid_spec=grid_spec,
        out_shape=jax.ShapeDtypeStruct((n, d), F32),
        name="moe_combine",
    )(*lists, *([y] * (2 * N_EXPERTS)), key_c, x2, mod_l, g_post)


def kernel(x, c, w_ada, b_ada, g_pre_mix, g_post_mix, g_pre_ffn, g_post_ffn, w_in, b_forget, g_v,
           w_spatial, b_spatial, g_out_attn, g_out_sgu, w_out, w_gate_up_dense, w_down_dense,
           w_router, w_gate_up_exp, w_down_exp):
    bsz, seq, d = x.shape
    depth = w_ada.shape[0]
    n = bsz * seq
    npair = N_HEADS // 2
    assert d == D_MODEL and seq % PROJ_TILE == 0 and seq % TOK_TILE == 0 and seq % ATT_TQ == 0

    mod = _modulation(c, w_ada, b_ada).reshape(depth, bsz, N_MOD, d)
    x2 = x.reshape(n, d)
    bf16 = {("w_in", 0): w_in[0].astype(BF16), ("w_out", 0): w_out[0].astype(BF16)}
    for l in range(depth):
        mod_l = mod[l]
        moe = (l % 2 == 1)
        q, k, v, zf_t, u, sv = _inproj(x2, mod_l, g_pre_mix[l].reshape(1, d), bf16[("w_in", l)],
                                       g_v[l].reshape(1, SGU_W), seq)
        lcum = _forget_cumsum(zf_t, b_forget[l], seq)
        riders = {}
        if l + 1 < depth:
            riders[("w_in", l + 1)] = w_in[l + 1]
            riders[("w_out", l + 1)] = w_out[l + 1]
        if moe:
            riders[("w_gu_exp", l // 2)] = w_gate_up_exp[l // 2].reshape(N_EXPERTS * d, -1)
        else:
            riders[("w_gu_dense", l // 2)] = w_gate_up_dense[l // 2]
            riders[("w_down_dense", l // 2)] = w_down_dense[l // 2]
            if l + 1 < depth:
                riders[("w_down_exp", l // 2)] = w_down_exp[l // 2].reshape(-1, d)
        attn, *casts = _attention(q, k, v, lcum.reshape(npair, 2, n), bsz, seq,
                                  tuple(riders.values()))
        bf16.update(zip(riders.keys(), casts))
        sgun = _sgu(u, sv, w_spatial[l].astype(BF16), b_spatial[l].T,
                    g_out_sgu[l].reshape(1, SGU_W))
        w_r = None
        if moe:
            w_r = jnp.pad(w_router[l // 2], ((0, 0), (0, LANES - N_EXPERTS)))
        res = _outproj(attn, sgun, g_out_attn[l].reshape(1, ATTN_W), bf16[("w_out", l)],
                       x2, mod_l, g_post_mix[l].reshape(1, d), g_pre_ffn[l].reshape(1, d), w_r, seq)
        g_post = g_post_ffn[l].reshape(1, d)
        if not moe:
            x2, h2 = res
            x2 = _ffn_dense(h2, bf16[("w_gu_dense", l // 2)], bf16[("w_down_dense", l // 2)],
                            x2, mod_l, g_post, seq)
        else:
            x2, h2, logits = res
            key_c, gate_t, key_t, ends = _route(logits)
            wl = _work_lists(ends, n)
            xs, gs = _dispatch(h2, key_t, gate_t, wl["disp"], wl["max_tiles"])
            w_gu = bf16[("w_gu_exp", l // 2)].reshape(N_EXPERTS, d, 2 * D_FF_EXPERT)
            w_dn = bf16[("w_down_exp", l // 2)].reshape(N_EXPERTS, D_FF_EXPERT, d)
            y = _experts(xs, gs, w_gu, w_dn, wl["tiles"], wl["max_tiles"])
            x2 = _combine(y, key_c, x2, mod_l, g_post, wl["comb"], seq)
    return x2.reshape(bsz, seq, d)
```

```python
import functools
import math

import jax
import jax.numpy as jnp
from jax import lax
from jax.experimental import pallas as pl
from jax.experimental.pallas import tpu as pltpu

F32 = jnp.float32
BF16 = jnp.bfloat16

D_MODEL = 1024
HEAD_DIM = 64
N_HEADS = 8
N_GROUPS = 8
ATTN_W = N_HEADS * HEAD_DIM
SGU_W = N_GROUPS * HEAD_DIM
CHUNK = 128
D_FF_DENSE = 2816
N_EXPERTS = 8
D_FF_EXPERT = 3584
N_MOD = 6
EPS = 1e-6

LANES = 128
TOK_TILE = 512
PROJ_TILE = 1024
ROW_GROUP = 256
SGU_TILE = 2048
ATT_TQ = 256
MXU_TILE = 256
FFN_CHUNKS = ((0, 6 * MXU_TILE), (6 * MXU_TILE, D_FF_DENSE))
EXP_TM = 1024
EXP_SUB = 512
EXP_UNIT = 256
EXP_TF = 7 * MXU_TILE
LOG2E = math.log2(math.e)
DISP_TD = 256
COMB_ALIGN = 16
COMB_WIN = TOK_TILE + COMB_ALIGN
COMB_CHUNKS = ((0, MXU_TILE), (MXU_TILE, COMB_WIN))
UNROUTED = -float(2 ** 30)
GELU_C = math.sqrt(2.0 / math.pi)


def _rms(x, g):
    return x * lax.rsqrt(jnp.mean(x * x, axis=-1, keepdims=True) + EPS) * g


def _dot(a, b):
    return jnp.dot(a, b, preferred_element_type=F32)


def _row_groups(rows):
    return [slice(r, r + ROW_GROUP) for r in range(0, rows, ROW_GROUP)]


def _mod_kernel(c_ref, w_ref, b_ref, o_ref):
    c = c_ref[...]
    ca = (c * jax.nn.sigmoid(c)).astype(BF16)
    o_ref[0] = _dot(ca, w_ref[0].astype(BF16)) + b_ref[0]


def _modulation(c, w_ada, b_ada):
    depth, d, m = w_ada.shape
    bsz = c.shape[0]
    nblk = m // d
    return pl.pallas_call(
        _mod_kernel,
        grid=(depth, nblk),
        in_specs=[
            pl.BlockSpec((bsz, d), lambda l, j: (0, 0)),
            pl.BlockSpec((1, d, d), lambda l, j: (l, 0, j)),
            pl.BlockSpec((1, 1, d), lambda l, j: (l, 0, j)),
        ],
        out_specs=pl.BlockSpec((1, bsz, d), lambda l, j: (l, 0, j)),
        out_shape=jax.ShapeDtypeStruct((depth, bsz, m), F32),
        name="adaln_mod",
    )(c, w_ada, b_ada.reshape(depth, 1, m))


def _inproj_kernel(x_ref, mod_ref, gpre_ref, w_ref, gv_ref,
                   q_ref, k_ref, v_ref, zf_ref, u_ref, sv_ref, wuv_ref):
    o3 = 3 * ATTN_W
    o4 = o3 + N_HEADS

    @pl.when(pl.program_id(0) == 0)
    def _():
        wuv_ref[...] = w_ref[:, o4:]

    sh = mod_ref[0, 0:1, :]
    sc = mod_ref[0, 1:2, :]
    for rs in _row_groups(x_ref.shape[0]):
        h = (_rms(x_ref[rs, :], gpre_ref[...]) * (1.0 + sc) + sh).astype(BF16)
        qkv = _dot(h, w_ref[:, :o3])
        q_ref[rs, :] = (qkv[:, :ATTN_W] * (HEAD_DIM ** -0.5 * LOG2E)).astype(BF16)
        k_ref[rs, :] = qkv[:, ATTN_W:2 * ATTN_W].astype(BF16)
        v_ref[rs, :] = qkv[:, 2 * ATTN_W:].astype(BF16)
        zf_ref[:, rs] = _dot(h, w_ref[:, o3:o3 + LANES]).T[:N_HEADS, :]
        uv = _dot(h, wuv_ref[...])
        uv = uv * (0.5 * (1.0 + jnp.tanh(GELU_C * (uv + 0.044715 * (uv * uv * uv)))))
        u_ref[rs, :] = uv[:, :SGU_W].astype(BF16)
        s = uv[:, SGU_W:]
        sc_ = s - jnp.mean(s, axis=-1, keepdims=True)
        sv = sc_ * lax.rsqrt(jnp.mean(sc_ * sc_, axis=-1, keepdims=True) + EPS) * gv_ref[...]
        sv_ref[rs, :] = sv.astype(BF16)


def _inproj(x2, mod_l, g_pre, w_in_l, g_v, seq):
    n, d = x2.shape
    tm = PROJ_TILE
    per_b = seq // tm
    row = lambda i: (i, 0)
    const = lambda i: (0, 0)
    resident = dict(pipeline_mode=pl.Buffered(1))
    outs = [jax.ShapeDtypeStruct((n, ATTN_W), BF16)] * 3 + [
        jax.ShapeDtypeStruct((N_HEADS, n), F32),
        jax.ShapeDtypeStruct((n, SGU_W), BF16),
        jax.ShapeDtypeStruct((n, SGU_W), BF16)]
    return pl.pallas_call(
        _inproj_kernel,
        grid=(n // tm,),
        in_specs=[
            pl.BlockSpec((tm, d), row),
            pl.BlockSpec((1, N_MOD, d), lambda i: (i // per_b, 0, 0)),
            pl.BlockSpec((1, d), const),
            pl.BlockSpec(w_in_l.shape, const, **resident),
            pl.BlockSpec((1, SGU_W), const),
        ],
        out_specs=[pl.BlockSpec((tm, ATTN_W), row)] * 3 + [
            pl.BlockSpec((N_HEADS, tm), lambda i: (0, i)),
            pl.BlockSpec((tm, SGU_W), row),
            pl.BlockSpec((tm, SGU_W), row)],
        out_shape=outs,
        scratch_shapes=[pltpu.VMEM((d, 2 * SGU_W), BF16)],
        name="inproj",
    )(x2, mod_l, g_pre, w_in_l, g_v)


def _cum_log_gates(z):
    ls = jnp.minimum(z, 0.0) - jnp.log1p(jnp.exp(-jnp.abs(z)))
    seq = ls.shape[-1]
    pos = lax.broadcasted_iota(jnp.int32, ls.shape, 1)
    shift = 1
    while shift < seq:
        ls = ls + jnp.where(pos >= shift, pltpu.roll(ls, shift, 1), 0.0)
        shift *= 2
    return ls * LOG2E


def _attn_kernel(q_ref, k_ref, v_ref, zf_ref, bf_ref, *rest):
    n_ride = (len(rest) - 1) // 2
    o_ref = rest[n_ride]
    for wsrc_ref, wdst_ref in zip(rest[:n_ride], rest[n_ride + 1:]):
        wdst_ref[...] = wsrc_ref[...].astype(wdst_ref.dtype)
    tq = ATT_TQ
    seq = q_ref.shape[0]
    nq = seq // tq
    z2 = zf_ref[0] + bf_ref[0]
    l_keys = _cum_log_gates(jnp.concatenate([z2, jnp.zeros((8 - 2, seq), F32)], axis=0))
    l_rows = jnp.concatenate([l_keys, jnp.zeros((LANES - 8, seq), F32)], axis=0)
    l_cols = l_rows.T
    lane = lax.broadcasted_iota(jnp.int32, (tq, LANES), 1)
    left = lane < HEAD_DIM
    rows = lax.broadcasted_iota(jnp.int32, (tq, tq), 0)
    cols = lax.broadcasted_iota(jnp.int32, (tq, tq), 1)
    causal = cols <= rows
    zero = jnp.zeros((tq, LANES), BF16)
    one = jnp.ones((tq, LANES), BF16)
    def scores(qi):
        qs = slice(qi * tq, (qi + 1) * tq)
        q2 = q_ref[qs, :]
        qq = jnp.concatenate([jnp.where(left, q2, zero), jnp.where(left, zero, q2)], axis=0)
        t_blocks = [[], []]
        mt = [None, None]
        for j in range(qi + 1):
            ks = slice(j * tq, (j + 1) * tq)
            t2 = lax.dot_general(qq, k_ref[ks, :], (((1,), (1,)), ((), ())),
                                 preferred_element_type=F32)
            for hh in range(2):
                t = t2[hh * tq:(hh + 1) * tq] - l_keys[hh:hh + 1, ks]
                if j == qi:
                    t = jnp.where(causal, t, -jnp.inf)
                t_blocks[hh].append(t)
                mj = jnp.maximum(t[:, :LANES], t[:, LANES:])
                mt[hh] = mj if mt[hh] is None else jnp.maximum(mt[hh], mj)
        shift = []
        for hh in range(2):
            lq = l_cols[qs, hh:hh + 1]
            row_max = jnp.max(mt[hh], axis=-1, keepdims=True) + lq
            shift.append(row_max - lq)
        return t_blocks, shift

    def values(qi, t_blocks, shift):
        qs = slice(qi * tq, (qi + 1) * tq)
        accs = [None, None]
        for j in range(qi + 1):
            ks = slice(j * tq, (j + 1) * tq)
            pp = jnp.concatenate([jnp.exp2(t_blocks[hh][j] - shift[hh]) for hh in range(2)],
                                 axis=0).astype(BF16)
            vaug = jnp.concatenate([v_ref[ks, :], one], axis=1)
            part = _dot(pp, vaug)
            accs[j % 2] = part if accs[j % 2] is None else accs[j % 2] + part
        acc = accs[0] if accs[1] is None else accs[0] + accs[1]
        out0 = acc[:tq, :LANES] / acc[:tq, LANES:]
        out1 = acc[tq:, :LANES] / acc[tq:, LANES:]
        o_ref[qs, :] = jnp.where(left, out0, out1).astype(o_ref.dtype)

    order = list(range(nq - 1, -1, -1))
    pending = scores(order[0])
    for pos, qi in enumerate(order):
        nxt = scores(order[pos + 1]) if pos + 1 < nq else None
        values(qi, *pending)
        pending = nxt


def _attention(q, k, v, zf_pairs, bf_pairs, bsz, seq, riders=()):
    n = q.shape[0]
    npair = N_HEADS // 2
    steps = bsz * npair
    blk = pl.BlockSpec((seq, LANES), lambda b, p: (b, p))
    in_specs = [blk, blk, blk, pl.BlockSpec((1, 2, seq), lambda b, p: (p, 0, b)),
                pl.BlockSpec((1, 2, 1), lambda b, p: (p, 0, 0))]
    out_specs = [blk]
    out_shape = [jax.ShapeDtypeStruct((n, ATTN_W), BF16)]
    for w in riders:
        rows, cols = w.shape
        share = 1
        while rows % (steps // share) or (rows // (steps // share)) % COMB_ALIGN:
            share *= 2
        wblk = pl.BlockSpec((rows // (steps // share), cols),
                            lambda b, p, share=share: ((b * npair + p) // share, 0))
        in_specs.append(wblk)
        out_specs.append(wblk)
        out_shape.append(jax.ShapeDtypeStruct((rows, cols), BF16))
    return pl.pallas_call(
        _attn_kernel,
        grid=(bsz, npair),
        in_specs=in_specs,
        out_specs=out_specs,
        out_shape=out_shape,
        name="fox_attention",
    )(q, k, v, zf_pairs, bf_pairs, *riders)


def _sgu_kernel(u_ref, sv_ref, w_ref, bt_ref, g_ref, o_ref):
    tm = u_ref.shape[0]
    npair = N_GROUPS // 2
    lane = lax.broadcasted_iota(jnp.int32, (CHUNK, LANES), 1)
    left = lane < HEAD_DIM
    r = lax.broadcasted_iota(jnp.int32, (CHUNK, CHUNK), 0)
    c = lax.broadcasted_iota(jnp.int32, (CHUNK, CHUNK), 1)
    causal = c <= r
    lhs, bias = [], []
    for j in range(npair):
        wa = jnp.where(causal, w_ref[2 * j], jnp.zeros((), BF16))
        wb = jnp.where(causal, w_ref[2 * j + 1], jnp.zeros((), BF16))
        lhs.append(jnp.concatenate([wa, wb], axis=1))
        bias.append(jnp.where(left, bt_ref[:, 2 * j:2 * j + 1], bt_ref[:, 2 * j + 1:2 * j + 2]))
    for ci in range(tm // CHUNK):
        rs = slice(ci * CHUNK, (ci + 1) * CHUNK)
        blks = []
        ssq = jnp.zeros((CHUNK, 1), F32)
        for j in range(npair):
            cs = slice(j * LANES, (j + 1) * LANES)
            svb = sv_ref[rs, cs]
            zero = jnp.zeros_like(svb)
            rhs = jnp.concatenate([jnp.where(left, svb, zero), jnp.where(left, zero, svb)], axis=0)
            mixed = _dot(lhs[j], rhs) + bias[j]
            ob = u_ref[rs, cs].astype(F32) * mixed
            ssq = ssq + jnp.sum(ob * ob, axis=-1, keepdims=True)
            blks.append(ob)
        inv = lax.rsqrt(ssq * (1.0 / SGU_W) + EPS)
        for j in range(npair):
            cs = slice(j * LANES, (j + 1) * LANES)
            o_ref[rs, cs] = (blks[j] * inv * g_ref[:, cs]).astype(o_ref.dtype)


def _sgu(u, sv, w_sp, b_sp_t, g_out):
    n = u.shape[0]
    tm = SGU_TILE
    row = lambda i: (i, 0)
    return pl.pallas_call(
        _sgu_kernel,
        grid=(n // tm,),
        in_specs=[
            pl.BlockSpec((tm, SGU_W), row),
            pl.BlockSpec((tm, SGU_W), row),
            pl.BlockSpec(w_sp.shape, lambda i: (0, 0, 0)),
            pl.BlockSpec(b_sp_t.shape, lambda i: (0, 0)),
            pl.BlockSpec((1, SGU_W), lambda i: (0, 0)),
        ],
        out_specs=pl.BlockSpec((tm, SGU_W), row),
        out_shape=jax.ShapeDtypeStruct((n, SGU_W), BF16),
        name="sgu",
    )(u, sv, w_sp, b_sp_t, g_out)


def _outproj_kernel(a_ref, s_ref, ga_ref, wo_ref, x_ref, mod_ref, gpost_ref, gffn_ref,
                    *rest, with_router):
    if with_router:
        wr_ref, xo_ref, h_ref, lg_ref = rest
    else:
        xo_ref, h_ref = rest
    gt_m = mod_ref[0, 2:3, :]
    sh_f = mod_ref[0, 3:4, :]
    sc_f = mod_ref[0, 4:5, :]
    if with_router:
        wr = wr_ref[...]
        w_hi = wr.astype(BF16)
        w_lo = (wr - w_hi.astype(F32)).astype(BF16)
    for rs in _row_groups(x_ref.shape[0]):
        a = _rms(a_ref[rs, :].astype(F32), ga_ref[...]).astype(BF16)
        y = _dot(a, wo_ref[:ATTN_W, :]) + _dot(s_ref[rs, :], wo_ref[ATTN_W:, :])
        xn = x_ref[rs, :] + (1.0 + gt_m) * _rms(y, gpost_ref[...])
        xo_ref[rs, :] = xn
        h = _rms(xn, gffn_ref[...]) * (1.0 + sc_f) + sh_f
        h_hi = h.astype(BF16)
        h_ref[rs, :] = h_hi
        if with_router:
            h_lo = (h - h_hi.astype(F32)).astype(BF16)
            lg_ref[rs, :] = _dot(h_hi, w_hi) + (_dot(h_hi, w_lo) + _dot(h_lo, w_hi))


def _outproj(attn, sgun, g_attn, w_o, x2, mod_l, g_post, g_ffn, w_router, seq):
    n, d = x2.shape
    tm = PROJ_TILE
    per_b = seq // tm
    row = lambda i: (i, 0)
    const = lambda i: (0, 0)
    with_router = w_router is not None
    in_specs = [
        pl.BlockSpec((tm, ATTN_W), row),
        pl.BlockSpec((tm, SGU_W), row),
        pl.BlockSpec((1, ATTN_W), const),
        pl.BlockSpec(w_o.shape, const),
        pl.BlockSpec((tm, d), row),
        pl.BlockSpec((1, N_MOD, d), lambda i: (i // per_b, 0, 0)),
        pl.BlockSpec((1, d), const),
        pl.BlockSpec((1, d), const),
    ]
    args = [attn, sgun, g_attn, w_o, x2, mod_l, g_post, g_ffn]
    out_specs = [pl.BlockSpec((tm, d), row), pl.BlockSpec((tm, d), row)]
    out_shape = [jax.ShapeDtypeStruct((n, d), F32), jax.ShapeDtypeStruct((n, d), BF16)]
    if with_router:
        in_specs.append(pl.BlockSpec(w_router.shape, const))
        args.append(w_router)
        out_specs.append(pl.BlockSpec((tm, LANES), row))
        out_shape.append(jax.ShapeDtypeStruct((n, LANES), F32))
    return pl.pallas_call(
        functools.partial(_outproj_kernel, with_router=with_router),
        grid=(n // tm,),
        in_specs=in_specs,
        out_specs=out_specs,
        out_shape=out_shape,
        name="outproj_router" if with_router else "outproj",
    )(*args)


def _ffn_kernel(h_ref, wgu_ref, wd_ref, x_ref, mod_ref, gpost_ref, o_ref):
    h = h_ref[...]
    y = None
    for a, b in FFN_CHUNKS:
        g = _dot(h, wgu_ref[:, a:b])
        u = _dot(h, wgu_ref[:, D_FF_DENSE + a:D_FF_DENSE + b])
        act = (g * jax.nn.sigmoid(g) * u).astype(BF16)
        part = _dot(act, wd_ref[a:b, :])
        y = part if y is None else y + part
    gt_f = mod_ref[0, 5:6, :]
    o_ref[...] = x_ref[...] + (1.0 + gt_f) * _rms(y, gpost_ref[...])


def _ffn_dense(h2, w_gu, w_down, x2, mod_l, g_post, seq):
    n, d = x2.shape
    tm = TOK_TILE
    per_b = seq // tm
    resident = dict(pipeline_mode=pl.Buffered(1))
    return pl.pallas_call(
        _ffn_kernel,
        grid=(n // tm,),
        in_specs=[
            pl.BlockSpec((tm, d), lambda i: (i, 0)),
            pl.BlockSpec(w_gu.shape, lambda i: (0, 0), **resident),
            pl.BlockSpec(w_down.shape, lambda i: (0, 0), **resident),
            pl.BlockSpec((tm, d), lambda i: (i, 0)),
            pl.BlockSpec((1, N_MOD, d), lambda i: (i // per_b, 0, 0)),
            pl.BlockSpec((1, d), lambda i: (0, 0)),
        ],
        out_specs=pl.BlockSpec((tm, d), lambda i: (i, 0)),
        out_shape=jax.ShapeDtypeStruct((n, d), F32),
        name="ffn_dense",
    )(h2, w_gu, w_down, x2, mod_l, g_post)


def _route_kernel(lg_ref, keyc_ref, gate_ref, keyt_ref, ends_ref, base_ref):
    s = pl.program_id(0)
    tm = lg_ref.shape[0]

    @pl.when(s == 0)
    def _():
        base_ref[...] = jnp.zeros_like(base_ref)

    lane = lax.broadcasted_iota(jnp.int32, (tm, LANES), 1)
    lg = jnp.where(lane < N_EXPERTS, lg_ref[...], -jnp.inf)
    m1 = jnp.max(lg, axis=-1, keepdims=True)
    i1 = jnp.min(jnp.where(lg == m1, lane, LANES), axis=-1, keepdims=True)
    sel1 = lane == i1
    lg2 = jnp.where(sel1, -jnp.inf, lg)
    m2 = jnp.max(lg2, axis=-1, keepdims=True)
    i2 = jnp.min(jnp.where(lg2 == m2, lane, LANES), axis=-1, keepdims=True)
    sel2 = lane == i2
    e2 = jnp.exp(m2 - m1)
    w1 = 1.0 / (1.0 + e2)
    w2 = e2 / (1.0 + e2)
    gate_ref[...] = (jnp.where(sel1, w1, 0.0) + jnp.where(sel2, w2, 0.0)).T
    chosen = jnp.logical_or(sel1, sel2)
    mask = jnp.where(chosen, 1.0, 0.0)
    r = lax.broadcasted_iota(jnp.int32, (tm, tm), 0)
    c = lax.broadcasted_iota(jnp.int32, (tm, tm), 1)
    tri = jnp.where(c < r, 1.0, 0.0).astype(BF16)
    base = base_ref[0:1, :]
    rank = _dot(tri, mask.astype(BF16)) + base
    key = jnp.where(chosen, rank, UNROUTED)
    keyc_ref[...] = key
    keyt_ref[...] = key.T
    new_base = base + jnp.sum(mask, axis=0, keepdims=True)
    base_ref[...] = jnp.broadcast_to(new_base, base_ref.shape)
    ends_ref[0] = jnp.broadcast_to(new_base, ends_ref.shape[1:])


def _route(logits):
    n = logits.shape[0]
    tm = TOK_TILE
    ns = n // tm
    row = lambda s: (s, 0)
    return pl.pallas_call(
        _route_kernel,
        grid=(ns,),
        in_specs=[pl.BlockSpec((tm, LANES), row)],
        out_specs=[
            pl.BlockSpec((tm, LANES), row),
            pl.BlockSpec((LANES, tm), lambda s: (0, s)),
            pl.BlockSpec((LANES, tm), lambda s: (0, s)),
            pl.BlockSpec((1, 8, LANES), lambda s: (s, 0, 0)),
        ],
        out_shape=[
            jax.ShapeDtypeStruct((n, LANES), F32),
            jax.ShapeDtypeStruct((LANES, n), F32),
            jax.ShapeDtypeStruct((LANES, n), F32),
            jax.ShapeDtypeStruct((ns, 8, LANES), F32),
        ],
        scratch_shapes=[pltpu.VMEM((8, LANES), F32)],
        name="moe_route",
    )(logits)


def _work_lists(ends, n_tokens):
    e_n = N_EXPERTS
    sub = EXP_TM // DISP_TD
    max_tiles = (2 * n_tokens) // EXP_TM + e_n
    i32 = jnp.int32

    ends_i = ends[:, 0, :e_n].astype(i32)
    base = jnp.concatenate([jnp.zeros((1, e_n), i32), ends_i], axis=0)
    cnt = base[-1]
    ntile = (cnt + EXP_TM - 1) // EXP_TM
    tile_end = jnp.cumsum(ntile)
    tile_start = tile_end - ntile
    total_tiles = tile_end[-1]
    nunit = (cnt + EXP_UNIT - 1) // EXP_UNIT
    off = tile_start * EXP_TM

    e_ids = jnp.arange(e_n, dtype=i32)

    def expert_of(tile):
        e = jnp.sum((tile_end[None, :] <= tile[:, None]).astype(i32), axis=1)
        return e, (e[:, None] == e_ids[None, :]).astype(i32)

    def pick(onehot, per_expert):
        return jnp.sum(onehot * per_expert[None, :], axis=1)

    t = jnp.arange(max_tiles, dtype=i32)
    tc = jnp.minimum(t, total_tiles - 1)
    tile_e, oh_t = expert_of(tc)
    tile_pass = jnp.clip(pick(oh_t, nunit) - (tc - pick(oh_t, tile_start)) * (EXP_TM // EXP_UNIT),
                         1, EXP_TM // EXP_UNIT)
    tile_pass = jnp.where(t < total_tiles, tile_pass, 0)

    j = jnp.arange(max_tiles * sub, dtype=i32)
    jt = jnp.minimum(j // sub, total_tiles - 1)
    je, oh_j = expert_of(jt)
    jm = (jt - pick(oh_j, tile_start)) * sub + j % sub
    lo_rank = jm * DISP_TD
    hi_rank = jnp.minimum(lo_rank + DISP_TD, pick(oh_j, cnt))
    base_e = jnp.sum(oh_j[:, :, None] * base.T[None, :, :], axis=1)
    s_lo = jnp.sum((base_e[:, 1:] <= lo_rank[:, None]).astype(i32), axis=1)
    s_hi = jnp.sum((base_e[:, :-1] < hi_rank[:, None]).astype(i32), axis=1)
    n_src = jnp.where(j // sub < total_tiles, jnp.maximum(s_hi - s_lo, 0), 0)

    rows = max_tiles * EXP_TM
    start = off[None, :] + base[:-1]
    stop = off[None, :] + base[1:]
    wstart = jnp.minimum(start // COMB_ALIGN * COMB_ALIGN, rows - COMB_WIN)
    need = jnp.where(stop > start, stop - wstart, 0)
    rel = wstart - off[None, :]
    split = COMB_CHUNKS[0][1]
    wtail = jnp.where(need > split, (wstart + split) // COMB_ALIGN, 0)
    return dict(
        disp=(je, jm, s_lo, n_src),
        comb=((wstart // COMB_ALIGN).reshape(-1), wtail.reshape(-1), rel.reshape(-1),
              need.reshape(-1)),
        tiles=(tc, tile_e, tile_pass),
        max_tiles=max_tiles)


def _dispatch_kernel(e_ref, m_ref, slo_ref, n_ref, h_ref, keyt_ref, gatet_ref, o_ref, gs_ref,
                     acc_ref, gacc_ref):
    j = pl.program_id(0)
    td = o_ref.shape[0]
    e = e_ref[j]
    slot = (m_ref[j] * td + lax.broadcasted_iota(jnp.int32, (td, TOK_TILE), 0)).astype(F32)

    def item(i, assign):
        ts = pl.multiple_of((slo_ref[j] + i) * TOK_TILE, TOK_TILE)
        key = keyt_ref[pl.ds(e, 1), pl.ds(ts, TOK_TILE)]
        hit = key == slot
        onehot = jnp.where(hit, 1.0, 0.0).astype(BF16)
        rows = _dot(onehot, h_ref[pl.ds(ts, TOK_TILE), :])
        gate = gatet_ref[pl.ds(e, 1), pl.ds(ts, TOK_TILE)]
        gsum = jnp.sum(jnp.where(hit, gate, 0.0), axis=-1, keepdims=True)
        if assign:
            acc_ref[...] = rows
            gacc_ref[...] = gsum
        else:
            acc_ref[...] += rows
            gacc_ref[...] += gsum

    @pl.when(n_ref[j] == 0)
    def _():
        acc_ref[...] = jnp.zeros_like(acc_ref)
        gacc_ref[...] = jnp.zeros_like(gacc_ref)

    @pl.when(n_ref[j] > 0)
    def _():
        item(0, True)

    def body(i, carry):
        item(i, False)
        return carry

    lax.fori_loop(1, n_ref[j], body, 0)
    o_ref[...] = acc_ref[...].astype(o_ref.dtype)
    gs_ref[...] = gacc_ref[...]


def _dispatch(h2, key_t, gate_t, lists, max_tiles):
    n, d = h2.shape
    rows = max_tiles * EXP_TM
    resident = dict(pipeline_mode=pl.Buffered(1))
    grid_spec = pltpu.PrefetchScalarGridSpec(
        num_scalar_prefetch=4,
        grid=(rows // DISP_TD,),
        in_specs=[
            pl.BlockSpec((n, d), lambda j, *_: (0, 0), **resident),
            pl.BlockSpec((8, n), lambda j, *_: (0, 0), **resident),
            pl.BlockSpec((8, n), lambda j, *_: (0, 0), **resident),
        ],
        out_specs=[pl.BlockSpec((DISP_TD, d), lambda j, *_: (j, 0)),
                   pl.BlockSpec((DISP_TD, 1), lambda j, *_: (j, 0))],
        scratch_shapes=[pltpu.VMEM((DISP_TD, d), F32), pltpu.VMEM((DISP_TD, 1), F32)],
    )
    return pl.pallas_call(
        _dispatch_kernel,
        grid_spec=grid_spec,
        out_shape=[jax.ShapeDtypeStruct((rows, d), BF16),
                   jax.ShapeDtypeStruct((rows, 1), F32)],
        name="moe_dispatch",
    )(*lists, h2, key_t, gate_t)


def _expert_kernel(tidx_ref, te_ref, tp_ref, x_ref, gs_ref, wg_ref, wu_ref, wd_ref, y_ref,
                   acc_ref):
    i = pl.program_id(0)
    f = pl.program_id(1)
    nf = pl.num_programs(1)

    max_units = EXP_TM // EXP_UNIT
    n_steps = D_FF_EXPERT // EXP_TF

    def run(units, first, last):
        done = 0
        while done < units * EXP_UNIT:
            size = EXP_SUB if units * EXP_UNIT - done >= EXP_SUB else EXP_UNIT
            rs = slice(done, done + size)
            done += size
            x = x_ref[rs, :]
            g = _dot(x, wg_ref[0])
            u = _dot(x, wu_ref[0])
            act = (g * jax.nn.sigmoid(g) * u).astype(BF16)
            part = _dot(act, wd_ref[0])
            total = part if first else acc_ref[rs, :] + part
            if last:
                y_ref[rs, :] = (gs_ref[rs, :] * total).astype(y_ref.dtype)
            else:
                acc_ref[rs, :] = total
        if last and done < EXP_TM:
            y_ref[done:, :] = jnp.zeros((EXP_TM - done, y_ref.shape[1]), y_ref.dtype)

    kinds = [(f == 0, True, n_steps == 1), (f == nf - 1, n_steps == 1, True)]
    if n_steps > 2:
        kinds.append((jnp.logical_and(f > 0, f < nf - 1), False, False))
    for units in range(max_units + 1):
        for cond, first, last in kinds[:1] if n_steps == 1 else kinds:
            @pl.when(jnp.logical_and(tp_ref[i] == units, cond))
            def _(units=units, first=first, last=last):
                run(units, first, last)


def _experts(xs, gs, w_gu, w_down, tiles, max_tiles):
    rows, d = xs.shape
    tm, tf = EXP_TM, EXP_TF
    nf = D_FF_EXPERT // tf

    def fsel(i, f, tp):
        return jnp.where(tp[i] > 0, f, nf - 1)

    grid_spec = pltpu.PrefetchScalarGridSpec(
        num_scalar_prefetch=3,
        grid=(max_tiles, nf),
        in_specs=[
            pl.BlockSpec((tm, d), lambda i, f, ti, te, tv: (ti[i], 0)),
            pl.BlockSpec((tm, 1), lambda i, f, ti, te, tv: (ti[i], 0)),
            pl.BlockSpec((1, d, tf), lambda i, f, ti, te, tv: (te[i], 0, fsel(i, f, tv))),
            pl.BlockSpec((1, d, tf), lambda i, f, ti, te, tv: (te[i], 0, nf + fsel(i, f, tv))),
            pl.BlockSpec((1, tf, d), lambda i, f, ti, te, tv: (te[i], fsel(i, f, tv), 0)),
        ],
        out_specs=pl.BlockSpec((tm, d), lambda i, f, ti, te, tv: (i, 0)),
        scratch_shapes=[pltpu.VMEM((tm, d), F32)],
    )
    return pl.pallas_call(
        _expert_kernel,
        grid_spec=grid_spec,
        out_shape=jax.ShapeDtypeStruct((rows, d), BF16),
        name="moe_experts",
    )(*tiles, xs, gs, w_gu, w_gu, w_down)


def _combine_kernel(ws_ref, wt_ref, rel_ref, need_ref, *refs):
    head_refs = refs[:N_EXPERTS]
    tail_refs = refs[N_EXPERTS:2 * N_EXPERTS]
    keyc_ref, x_ref, mod_ref, gpost_ref, o_ref, acc_ref = refs[2 * N_EXPERTS:]
    s = pl.program_id(0)
    ts = acc_ref.shape[0]

    def term(e, y_ref, a):
        key = keyc_ref[:, e:e + 1]
        rel = rel_ref[s * N_EXPERTS + e]
        slot = (rel + a + lax.broadcasted_iota(jnp.int32, (ts, y_ref.shape[0]), 1)).astype(F32)
        onehot = jnp.where(key == slot, 1.0, 0.0).astype(BF16)
        return _dot(onehot, y_ref[...])

    acc = term(0, head_refs[0], 0)
    for e in range(1, N_EXPERTS):
        acc = acc + term(e, head_refs[e], 0)
    acc_ref[...] = acc
    split = COMB_CHUNKS[0][1]
    for e in range(N_EXPERTS):
        @pl.when(need_ref[s * N_EXPERTS + e] > split)
        def _(e=e):
            acc_ref[...] += term(e, tail_refs[e], split)
    gt_f = mod_ref[0, 5:6, :]
    o_ref[...] = x_ref[...] + (1.0 + gt_f) * _rms(acc_ref[...], gpost_ref[...])


def _combine(y, key_c, x2, mod_l, g_post, lists, seq):
    n, d = x2.shape
    per_b = seq // TOK_TILE
    tok = lambda s, *_: (s, 0)

    (a0, b0), (a1, b1) = COMB_CHUNKS

    def head(e):
        return pl.BlockSpec((pl.Element(b0 - a0), pl.Element(d)),
                            lambda s, ws, wt, rel, need: (ws[s * N_EXPERTS + e] * COMB_ALIGN, 0))

    def tail(e):
        return pl.BlockSpec((pl.Element(b1 - a1), pl.Element(d)),
                            lambda s, ws, wt, rel, need: (wt[s * N_EXPERTS + e] * COMB_ALIGN, 0))

    grid_spec = pltpu.PrefetchScalarGridSpec(
        num_scalar_prefetch=4,
        grid=(n // TOK_TILE,),
        in_specs=[head(e) for e in range(N_EXPERTS)] + [tail(e) for e in range(N_EXPERTS)] + [
            pl.BlockSpec((TOK_TILE, LANES), tok),
            pl.BlockSpec((TOK_TILE, d), tok),
            pl.BlockSpec((1, N_MOD, d), lambda s, *_: (s // per_b, 0, 0)),
            pl.BlockSpec((1, d), lambda s, *_: (0, 0)),
        ],
        out_specs=pl.BlockSpec((TOK_TILE, d), tok),
        scratch_shapes=[pltpu.VMEM((TOK_TILE, d), F32)],
    )
    return pl.pallas_call(
        _combine_kernel,
        grid_spec=grid_spec,
        out_shape=jax.ShapeDtypeStruct((n, d), F32),
        name="moe_combine",
    )(*lists, *([y] * (2 * N_EXPERTS)), key_c, x2, mod_l, g_post)


def kernel(x, c, w_ada, b_ada, g_pre_mix, g_post_mix, g_pre_ffn, g_post_ffn, w_in, b_forget, g_v,
           w_spatial, b_spatial, g_out_attn, g_out_sgu, w_out, w_gate_up_dense, w_down_dense,
           w_router, w_gate_up_exp, w_down_exp):
    bsz, seq, d = x.shape
    depth = w_ada.shape[0]
    n = bsz * seq
    npair = N_HEADS // 2
    assert d == D_MODEL and seq % PROJ_TILE == 0 and seq % TOK_TILE == 0 and seq % ATT_TQ == 0

    mod = _modulation(c, w_ada, b_ada).reshape(depth, bsz, N_MOD, d)
    x2 = x.reshape(n, d)
    bf16 = {("w_in", 0): w_in[0].astype(BF16), ("w_out", 0): w_out[0].astype(BF16)}
    for l in range(depth):
        mod_l = mod[l]
        moe = (l % 2 == 1)
        q, k, v, zf_t, u, sv = _inproj(x2, mod_l, g_pre_mix[l].reshape(1, d), bf16[("w_in", l)],
                                       g_v[l].reshape(1, SGU_W), seq)
        riders = {}
        if l + 1 < depth:
            riders[("w_in", l + 1)] = w_in[l + 1]
            riders[("w_out", l + 1)] = w_out[l + 1]
        if moe:
            riders[("w_gu_exp", l // 2)] = w_gate_up_exp[l // 2].reshape(N_EXPERTS * d, -1)
        else:
            riders[("w_gu_dense", l // 2)] = w_gate_up_dense[l // 2]
            riders[("w_down_dense", l // 2)] = w_down_dense[l // 2]
            if l + 1 < depth:
                riders[("w_down_exp", l // 2)] = w_down_exp[l // 2].reshape(-1, d)
        attn, *casts = _attention(q, k, v, zf_t.reshape(npair, 2, n),
                                  b_forget[l].reshape(npair, 2, 1), bsz, seq,
                                  tuple(riders.values()))
        bf16.update(zip(riders.keys(), casts))
        sgun = _sgu(u, sv, w_spatial[l].astype(BF16), b_spatial[l].T,
                    g_out_sgu[l].reshape(1, SGU_W))
        w_r = None
        if moe:
            w_r = jnp.pad(w_router[l // 2], ((0, 0), (0, LANES - N_EXPERTS)))
        res = _outproj(attn, sgun, g_out_attn[l].reshape(1, ATTN_W), bf16[("w_out", l)],
                       x2, mod_l, g_post_mix[l].reshape(1, d), g_pre_ffn[l].reshape(1, d), w_r, seq)
        g_post = g_post_ffn[l].reshape(1, d)
        if not moe:
            x2, h2 = res
            x2 = _ffn_dense(h2, bf16[("w_gu_dense", l // 2)], bf16[("w_down_dense", l // 2)],
                            x2, mod_l, g_post, seq)
        else:
            x2, h2, logits = res
            key_c, gate_t, key_t, ends = _route(logits)
            wl = _work_lists(ends, n)
            xs, gs = _dispatch(h2, key_t, gate_t, wl["disp"], wl["max_tiles"])
            w_gu = bf16[("w_gu_exp", l // 2)].reshape(N_EXPERTS, d, 2 * D_FF_EXPERT)
            w_dn = bf16[("w_down_exp", l // 2)].reshape(N_EXPERTS, D_FF_EXPERT, d)
            y = _experts(xs, gs, w_gu, w_dn, wl["tiles"], wl["max_tiles"])
            x2 = _combine(y, key_c, x2, mod_l, g_post, wl["comb"], seq)
    return x2.reshape(bsz, seq, d)
```

```python
import functools
import math

import jax
import jax.numpy as jnp
from jax import lax
from jax.experimental import pallas as pl
from jax.experimental.pallas import tpu as pltpu

F32 = jnp.float32
BF16 = jnp.bfloat16

D_MODEL = 1024
HEAD_DIM = 64
N_HEADS = 8
N_GROUPS = 8
ATTN_W = N_HEADS * HEAD_DIM
SGU_W = N_GROUPS * HEAD_DIM
CHUNK = 128
D_FF_DENSE = 2816
N_EXPERTS = 8
D_FF_EXPERT = 3584
N_MOD = 6
EPS = 1e-6

LANES = 128
TOK_TILE = 512
PROJ_TILE = 1024
ROW_GROUP = 256
SGU_TILE = 2048
ATT_TQ = 256
MXU_TILE = 256
FFN_CHUNKS = ((0, 6 * MXU_TILE), (6 * MXU_TILE, D_FF_DENSE))
EXP_TM = 1024
EXP_SUB = 512
EXP_UNIT = 512
EXP_TF = 7 * MXU_TILE
LOG2E = math.log2(math.e)
DISP_TD = 256
COMB_ALIGN = 16
COMB_WIN = TOK_TILE + COMB_ALIGN
COMB_CHUNKS = ((0, MXU_TILE), (MXU_TILE, COMB_WIN))
UNROUTED = -float(2 ** 30)
GELU_C = math.sqrt(2.0 / math.pi)


def _rms(x, g):
    return x * lax.rsqrt(jnp.mean(x * x, axis=-1, keepdims=True) + EPS) * g


def _dot(a, b):
    return jnp.dot(a, b, preferred_element_type=F32)


def _row_groups(rows):
    return [slice(r, r + ROW_GROUP) for r in range(0, rows, ROW_GROUP)]


def _mod_kernel(c_ref, w_ref, b_ref, o_ref):
    c = c_ref[...]
    ca = (c * jax.nn.sigmoid(c)).astype(BF16)
    o_ref[0] = _dot(ca, w_ref[0].astype(BF16)) + b_ref[0]


def _modulation(c, w_ada, b_ada):
    depth, d, m = w_ada.shape
    bsz = c.shape[0]
    nblk = m // d
    return pl.pallas_call(
        _mod_kernel,
        grid=(depth, nblk),
        in_specs=[
            pl.BlockSpec((bsz, d), lambda l, j: (0, 0)),
            pl.BlockSpec((1, d, d), lambda l, j: (l, 0, j)),
            pl.BlockSpec((1, 1, d), lambda l, j: (l, 0, j)),
        ],
        out_specs=pl.BlockSpec((1, bsz, d), lambda l, j: (l, 0, j)),
        out_shape=jax.ShapeDtypeStruct((depth, bsz, m), F32),
        name="adaln_mod",
    )(c, w_ada, b_ada.reshape(depth, 1, m))


def _inproj_kernel(x_ref, mod_ref, gpre_ref, w_ref, gv_ref,
                   q_ref, k_ref, v_ref, zf_ref, u_ref, sv_ref, wuv_ref):
    o3 = 3 * ATTN_W
    o4 = o3 + N_HEADS

    @pl.when(pl.program_id(0) == 0)
    def _():
        wuv_ref[...] = w_ref[:, o4:]

    sh = mod_ref[0, 0:1, :]
    sc = mod_ref[0, 1:2, :]
    for rs in _row_groups(x_ref.shape[0]):
        h = (_rms(x_ref[rs, :], gpre_ref[...]) * (1.0 + sc) + sh).astype(BF16)
        qkv = _dot(h, w_ref[:, :o3])
        q_ref[rs, :] = (qkv[:, :ATTN_W] * (HEAD_DIM ** -0.5 * LOG2E)).astype(BF16)
        k_ref[rs, :] = qkv[:, ATTN_W:2 * ATTN_W].astype(BF16)
        v_ref[rs, :] = qkv[:, 2 * ATTN_W:].astype(BF16)
        zf_ref[:, rs] = _dot(h, w_ref[:, o3:o3 + LANES]).T[:N_HEADS, :]
        uv = _dot(h, wuv_ref[...])
        uv = uv * (0.5 * (1.0 + jnp.tanh(GELU_C * (uv + 0.044715 * (uv * uv * uv)))))
        u_ref[rs, :] = uv[:, :SGU_W].astype(BF16)
        s = uv[:, SGU_W:]
        sc_ = s - jnp.mean(s, axis=-1, keepdims=True)
        sv = sc_ * lax.rsqrt(jnp.mean(sc_ * sc_, axis=-1, keepdims=True) + EPS) * gv_ref[...]
        sv_ref[rs, :] = sv.astype(BF16)


def _inproj(x2, mod_l, g_pre, w_in_l, g_v, seq):
    n, d = x2.shape
    tm = PROJ_TILE
    per_b = seq // tm
    row = lambda i: (i, 0)
    const = lambda i: (0, 0)
    resident = dict(pipeline_mode=pl.Buffered(1))
    outs = [jax.ShapeDtypeStruct((n, ATTN_W), BF16)] * 3 + [
        jax.ShapeDtypeStruct((N_HEADS, n), F32),
        jax.ShapeDtypeStruct((n, SGU_W), BF16),
        jax.ShapeDtypeStruct((n, SGU_W), BF16)]
    return pl.pallas_call(
        _inproj_kernel,
        grid=(n // tm,),
        in_specs=[
            pl.BlockSpec((tm, d), row),
            pl.BlockSpec((1, N_MOD, d), lambda i: (i // per_b, 0, 0)),
            pl.BlockSpec((1, d), const),
            pl.BlockSpec(w_in_l.shape, const, **resident),
            pl.BlockSpec((1, SGU_W), const),
        ],
        out_specs=[pl.BlockSpec((tm, ATTN_W), row)] * 3 + [
            pl.BlockSpec((N_HEADS, tm), lambda i: (0, i)),
            pl.BlockSpec((tm, SGU_W), row),
            pl.BlockSpec((tm, SGU_W), row)],
        out_shape=outs,
        scratch_shapes=[pltpu.VMEM((d, 2 * SGU_W), BF16)],
        name="inproj",
    )(x2, mod_l, g_pre, w_in_l, g_v)


def _cum_log_gates(z):
    ls = jnp.minimum(z, 0.0) - jnp.log1p(jnp.exp(-jnp.abs(z)))
    seq = ls.shape[-1]
    pos = lax.broadcasted_iota(jnp.int32, ls.shape, 1)
    shift = 1
    while shift < seq:
        ls = ls + jnp.where(pos >= shift, pltpu.roll(ls, shift, 1), 0.0)
        shift *= 2
    return ls * LOG2E


def _attn_kernel(q_ref, k_ref, v_ref, zf_ref, bf_ref, *rest):
    n_ride = (len(rest) - 1) // 2
    o_ref = rest[n_ride]
    for wsrc_ref, wdst_ref in zip(rest[:n_ride], rest[n_ride + 1:]):
        wdst_ref[...] = wsrc_ref[...].astype(wdst_ref.dtype)
    tq = ATT_TQ
    seq = q_ref.shape[0]
    nq = seq // tq
    z2 = zf_ref[0] + bf_ref[0]
    l_keys = _cum_log_gates(jnp.concatenate([z2, jnp.zeros((8 - 2, seq), F32)], axis=0))
    l_rows = jnp.concatenate([l_keys, jnp.zeros((LANES - 8, seq), F32)], axis=0)
    l_cols = l_rows.T
    lane = lax.broadcasted_iota(jnp.int32, (tq, LANES), 1)
    left = lane < HEAD_DIM
    rows = lax.broadcasted_iota(jnp.int32, (tq, tq), 0)
    cols = lax.broadcasted_iota(jnp.int32, (tq, tq), 1)
    causal = cols <= rows
    zero = jnp.zeros((tq, LANES), BF16)
    one = jnp.ones((tq, LANES), BF16)
    def scores(qi):
        qs = slice(qi * tq, (qi + 1) * tq)
        q2 = q_ref[qs, :]
        qq = jnp.concatenate([jnp.where(left, q2, zero), jnp.where(left, zero, q2)], axis=0)
        t_blocks = [[], []]
        mt = [None, None]
        for j in range(qi + 1):
            ks = slice(j * tq, (j + 1) * tq)
            t2 = lax.dot_general(qq, k_ref[ks, :], (((1,), (1,)), ((), ())),
                                 preferred_element_type=F32)
            for hh in range(2):
                t = t2[hh * tq:(hh + 1) * tq] - l_keys[hh:hh + 1, ks]
                if j == qi:
                    t = jnp.where(causal, t, -jnp.inf)
                t_blocks[hh].append(t)
                mj = jnp.maximum(t[:, :LANES], t[:, LANES:])
                mt[hh] = mj if mt[hh] is None else jnp.maximum(mt[hh], mj)
        shift = []
        for hh in range(2):
            lq = l_cols[qs, hh:hh + 1]
            row_max = jnp.max(mt[hh], axis=-1, keepdims=True) + lq
            shift.append(row_max - lq)
        return t_blocks, shift

    def values(qi, t_blocks, shift):
        qs = slice(qi * tq, (qi + 1) * tq)
        accs = [None, None]
        for j in range(qi + 1):
            ks = slice(j * tq, (j + 1) * tq)
            pp = jnp.concatenate([jnp.exp2(t_blocks[hh][j] - shift[hh]) for hh in range(2)],
                                 axis=0).astype(BF16)
            vaug = jnp.concatenate([v_ref[ks, :], one], axis=1)
            part = _dot(pp, vaug)
            accs[j % 2] = part if accs[j % 2] is None else accs[j % 2] + part
        acc = accs[0] if accs[1] is None else accs[0] + accs[1]
        out0 = acc[:tq, :LANES] / acc[:tq, LANES:]
        out1 = acc[tq:, :LANES] / acc[tq:, LANES:]
        o_ref[qs, :] = jnp.where(left, out0, out1).astype(o_ref.dtype)

    order = list(range(nq - 1, -1, -1))
    pending = scores(order[0])
    for pos, qi in enumerate(order):
        nxt = scores(order[pos + 1]) if pos + 1 < nq else None
        values(qi, *pending)
        pending = nxt


def _attention(q, k, v, zf_pairs, bf_pairs, bsz, seq, riders=()):
    n = q.shape[0]
    npair = N_HEADS // 2
    steps = bsz * npair
    blk = pl.BlockSpec((seq, LANES), lambda b, p: (b, p))
    in_specs = [blk, blk, blk, pl.BlockSpec((1, 2, seq), lambda b, p: (p, 0, b)),
                pl.BlockSpec((1, 2, 1), lambda b, p: (p, 0, 0))]
    out_specs = [blk]
    out_shape = [jax.ShapeDtypeStruct((n, ATTN_W), BF16)]
    for w in riders:
        rows, cols = w.shape
        share = 1
        while rows % (steps // share) or (rows // (steps // share)) % COMB_ALIGN:
            share *= 2
        wblk = pl.BlockSpec((rows // (steps // share), cols),
                            lambda b, p, share=share: ((b * npair + p) // share, 0))
        in_specs.append(wblk)
        out_specs.append(wblk)
        out_shape.append(jax.ShapeDtypeStruct((rows, cols), BF16))
    return pl.pallas_call(
        _attn_kernel,
        grid=(bsz, npair),
        in_specs=in_specs,
        out_specs=out_specs,
        out_shape=out_shape,
        name="fox_attention",
    )(q, k, v, zf_pairs, bf_pairs, *riders)


def _sgu_kernel(u_ref, sv_ref, w_ref, bt_ref, g_ref, o_ref):
    tm = u_ref.shape[0]
    npair = N_GROUPS // 2
    lane = lax.broadcasted_iota(jnp.int32, (CHUNK, LANES), 1)
    left = lane < HEAD_DIM
    r = lax.broadcasted_iota(jnp.int32, (CHUNK, CHUNK), 0)
    c = lax.broadcasted_iota(jnp.int32, (CHUNK, CHUNK), 1)
    causal = c <= r
    lhs, bias = [], []
    for j in range(npair):
        wa = jnp.where(causal, w_ref[2 * j], jnp.zeros((), BF16))
        wb = jnp.where(causal, w_ref[2 * j + 1], jnp.zeros((), BF16))
        lhs.append(jnp.concatenate([wa, wb], axis=1))
        bias.append(jnp.where(left, bt_ref[:, 2 * j:2 * j + 1], bt_ref[:, 2 * j + 1:2 * j + 2]))
    for ci in range(tm // CHUNK):
        rs = slice(ci * CHUNK, (ci + 1) * CHUNK)
        blks = []
        ssq = jnp.zeros((CHUNK, 1), F32)
        for j in range(npair):
            cs = slice(j * LANES, (j + 1) * LANES)
            svb = sv_ref[rs, cs]
            zero = jnp.zeros_like(svb)
            rhs = jnp.concatenate([jnp.where(left, svb, zero), jnp.where(left, zero, svb)], axis=0)
            mixed = _dot(lhs[j], rhs) + bias[j]
            ob = u_ref[rs, cs].astype(F32) * mixed
            ssq = ssq + jnp.sum(ob * ob, axis=-1, keepdims=True)
            blks.append(ob)
        inv = lax.rsqrt(ssq * (1.0 / SGU_W) + EPS)
        for j in range(npair):
            cs = slice(j * LANES, (j + 1) * LANES)
            o_ref[rs, cs] = (blks[j] * inv * g_ref[:, cs]).astype(o_ref.dtype)


def _sgu(u, sv, w_sp, b_sp_t, g_out):
    n = u.shape[0]
    tm = SGU_TILE
    row = lambda i: (i, 0)
    return pl.pallas_call(
        _sgu_kernel,
        grid=(n // tm,),
        in_specs=[
            pl.BlockSpec((tm, SGU_W), row),
            pl.BlockSpec((tm, SGU_W), row),
            pl.BlockSpec(w_sp.shape, lambda i: (0, 0, 0)),
            pl.BlockSpec(b_sp_t.shape, lambda i: (0, 0)),
            pl.BlockSpec((1, SGU_W), lambda i: (0, 0)),
        ],
        out_specs=pl.BlockSpec((tm, SGU_W), row),
        out_shape=jax.ShapeDtypeStruct((n, SGU_W), BF16),
        name="sgu",
    )(u, sv, w_sp, b_sp_t, g_out)


def _outproj_kernel(a_ref, s_ref, ga_ref, wo_ref, x_ref, mod_ref, gpost_ref, gffn_ref,
                    *rest, with_router):
    if with_router:
        wr_ref, xo_ref, h_ref, lg_ref = rest
    else:
        xo_ref, h_ref = rest
    gt_m = mod_ref[0, 2:3, :]
    sh_f = mod_ref[0, 3:4, :]
    sc_f = mod_ref[0, 4:5, :]
    if with_router:
        wr = wr_ref[...]
        w_hi = wr.astype(BF16)
        w_lo = (wr - w_hi.astype(F32)).astype(BF16)
    for rs in _row_groups(x_ref.shape[0]):
        a = _rms(a_ref[rs, :].astype(F32), ga_ref[...]).astype(BF16)
        y = _dot(a, wo_ref[:ATTN_W, :]) + _dot(s_ref[rs, :], wo_ref[ATTN_W:, :])
        xn = x_ref[rs, :] + (1.0 + gt_m) * _rms(y, gpost_ref[...])
        xo_ref[rs, :] = xn
        h = _rms(xn, gffn_ref[...]) * (1.0 + sc_f) + sh_f
        h_hi = h.astype(BF16)
        h_ref[rs, :] = h_hi
        if with_router:
            h_lo = (h - h_hi.astype(F32)).astype(BF16)
            lg_ref[rs, :] = _dot(h_hi, w_hi) + (_dot(h_hi, w_lo) + _dot(h_lo, w_hi))


def _outproj(attn, sgun, g_attn, w_o, x2, mod_l, g_post, g_ffn, w_router, seq):
    n, d = x2.shape
    tm = PROJ_TILE
    per_b = seq // tm
    row = lambda i: (i, 0)
    const = lambda i: (0, 0)
    with_router = w_router is not None
    in_specs = [
        pl.BlockSpec((tm, ATTN_W), row),
        pl.BlockSpec((tm, SGU_W), row),
        pl.BlockSpec((1, ATTN_W), const),
        pl.BlockSpec(w_o.shape, const),
        pl.BlockSpec((tm, d), row),
        pl.BlockSpec((1, N_MOD, d), lambda i: (i // per_b, 0, 0)),
        pl.BlockSpec((1, d), const),
        pl.BlockSpec((1, d), const),
    ]
    args = [attn, sgun, g_attn, w_o, x2, mod_l, g_post, g_ffn]
    out_specs = [pl.BlockSpec((tm, d), row), pl.BlockSpec((tm, d), row)]
    out_shape = [jax.ShapeDtypeStruct((n, d), F32), jax.ShapeDtypeStruct((n, d), BF16)]
    if with_router:
        in_specs.append(pl.BlockSpec(w_router.shape, const))
        args.append(w_router)
        out_specs.append(pl.BlockSpec((tm, LANES), row))
        out_shape.append(jax.ShapeDtypeStruct((n, LANES), F32))
    return pl.pallas_call(
        functools.partial(_outproj_kernel, with_router=with_router),
        grid=(n // tm,),
        in_specs=in_specs,
        out_specs=out_specs,
        out_shape=out_shape,
        name="outproj_router" if with_router else "outproj",
    )(*args)


def _ffn_kernel(h_ref, wgu_ref, wd_ref, x_ref, mod_ref, gpost_ref, o_ref):
    h = h_ref[...]
    y = None
    for a, b in FFN_CHUNKS:
        g = _dot(h, wgu_ref[:, a:b])
        u = _dot(h, wgu_ref[:, D_FF_DENSE + a:D_FF_DENSE + b])
        act = (g * jax.nn.sigmoid(g) * u).astype(BF16)
        part = _dot(act, wd_ref[a:b, :])
        y = part if y is None else y + part
    gt_f = mod_ref[0, 5:6, :]
    o_ref[...] = x_ref[...] + (1.0 + gt_f) * _rms(y, gpost_ref[...])


def _ffn_dense(h2, w_gu, w_down, x2, mod_l, g_post, seq):
    n, d = x2.shape
    tm = TOK_TILE
    per_b = seq // tm
    resident = dict(pipeline_mode=pl.Buffered(1))
    return pl.pallas_call(
        _ffn_kernel,
        grid=(n // tm,),
        in_specs=[
            pl.BlockSpec((tm, d), lambda i: (i, 0)),
            pl.BlockSpec(w_gu.shape, lambda i: (0, 0), **resident),
            pl.BlockSpec(w_down.shape, lambda i: (0, 0), **resident),
            pl.BlockSpec((tm, d), lambda i: (i, 0)),
            pl.BlockSpec((1, N_MOD, d), lambda i: (i // per_b, 0, 0)),
            pl.BlockSpec((1, d), lambda i: (0, 0)),
        ],
        out_specs=pl.BlockSpec((tm, d), lambda i: (i, 0)),
        out_shape=jax.ShapeDtypeStruct((n, d), F32),
        name="ffn_dense",
    )(h2, w_gu, w_down, x2, mod_l, g_post)


def _route_kernel(lg_ref, keyc_ref, gate_ref, keyt_ref, ends_ref, base_ref):
    s = pl.program_id(0)
    tm = lg_ref.shape[0]

    @pl.when(s == 0)
    def _():
        base_ref[...] = jnp.zeros_like(base_ref)

    lane = lax.broadcasted_iota(jnp.int32, (tm, LANES), 1)
    lg = jnp.where(lane < N_EXPERTS, lg_ref[...], -jnp.inf)
    m1 = jnp.max(lg, axis=-1, keepdims=True)
    i1 = jnp.min(jnp.where(lg == m1, lane, LANES), axis=-1, keepdims=True)
    sel1 = lane == i1
    lg2 = jnp.where(sel1, -jnp.inf, lg)
    m2 = jnp.max(lg2, axis=-1, keepdims=True)
    i2 = jnp.min(jnp.where(lg2 == m2, lane, LANES), axis=-1, keepdims=True)
    sel2 = lane == i2
    e2 = jnp.exp(m2 - m1)
    w1 = 1.0 / (1.0 + e2)
    w2 = e2 / (1.0 + e2)
    gate_ref[...] = (jnp.where(sel1, w1, 0.0) + jnp.where(sel2, w2, 0.0)).T
    chosen = jnp.logical_or(sel1, sel2)
    mask = jnp.where(chosen, 1.0, 0.0)
    r = lax.broadcasted_iota(jnp.int32, (tm, tm), 0)
    c = lax.broadcasted_iota(jnp.int32, (tm, tm), 1)
    tri = jnp.where(c < r, 1.0, 0.0).astype(BF16)
    base = base_ref[0:1, :]
    rank = _dot(tri, mask.astype(BF16)) + base
    key = jnp.where(chosen, rank, UNROUTED)
    keyc_ref[...] = key
    keyt_ref[...] = key.T
    new_base = base + jnp.sum(mask, axis=0, keepdims=True)
    base_ref[...] = jnp.broadcast_to(new_base, base_ref.shape)
    ends_ref[0] = jnp.broadcast_to(new_base, ends_ref.shape[1:])


def _route(logits):
    n = logits.shape[0]
    tm = TOK_TILE
    ns = n // tm
    row = lambda s: (s, 0)
    return pl.pallas_call(
        _route_kernel,
        grid=(ns,),
        in_specs=[pl.BlockSpec((tm, LANES), row)],
        out_specs=[
            pl.BlockSpec((tm, LANES), row),
            pl.BlockSpec((LANES, tm), lambda s: (0, s)),
            pl.BlockSpec((LANES, tm), lambda s: (0, s)),
            pl.BlockSpec((1, 8, LANES), lambda s: (s, 0, 0)),
        ],
        out_shape=[
            jax.ShapeDtypeStruct((n, LANES), F32),
            jax.ShapeDtypeStruct((LANES, n), F32),
            jax.ShapeDtypeStruct((LANES, n), F32),
            jax.ShapeDtypeStruct((ns, 8, LANES), F32),
        ],
        scratch_shapes=[pltpu.VMEM((8, LANES), F32)],
        name="moe_route",
    )(logits)


def _work_lists(ends, n_tokens):
    e_n = N_EXPERTS
    sub = EXP_TM // DISP_TD
    max_tiles = (2 * n_tokens) // EXP_TM + e_n
    i32 = jnp.int32

    ends_i = ends[:, 0, :e_n].astype(i32)
    base = jnp.concatenate([jnp.zeros((1, e_n), i32), ends_i], axis=0)
    cnt = base[-1]
    ntile = (cnt + EXP_TM - 1) // EXP_TM
    tile_end = jnp.cumsum(ntile)
    tile_start = tile_end - ntile
    total_tiles = tile_end[-1]
    nunit = (cnt + EXP_UNIT - 1) // EXP_UNIT
    off = tile_start * EXP_TM

    e_ids = jnp.arange(e_n, dtype=i32)

    def expert_of(tile):
        e = jnp.sum((tile_end[None, :] <= tile[:, None]).astype(i32), axis=1)
        return e, (e[:, None] == e_ids[None, :]).astype(i32)

    def pick(onehot, per_expert):
        return jnp.sum(onehot * per_expert[None, :], axis=1)

    t = jnp.arange(max_tiles, dtype=i32)
    tc = jnp.minimum(t, total_tiles - 1)
    tile_e, oh_t = expert_of(tc)
    tile_pass = jnp.clip(pick(oh_t, nunit) - (tc - pick(oh_t, tile_start)) * (EXP_TM // EXP_UNIT),
                         1, EXP_TM // EXP_UNIT)
    tile_pass = jnp.where(t < total_tiles, tile_pass, 0)

    j = jnp.arange(max_tiles * sub, dtype=i32)
    jt = jnp.minimum(j // sub, total_tiles - 1)
    je, oh_j = expert_of(jt)
    jm = (jt - pick(oh_j, tile_start)) * sub + j % sub
    lo_rank = jm * DISP_TD
    hi_rank = jnp.minimum(lo_rank + DISP_TD, pick(oh_j, cnt))
    base_e = jnp.sum(oh_j[:, :, None] * base.T[None, :, :], axis=1)
    s_lo = jnp.sum((base_e[:, 1:] <= lo_rank[:, None]).astype(i32), axis=1)
    s_hi = jnp.sum((base_e[:, :-1] < hi_rank[:, None]).astype(i32), axis=1)
    n_src = jnp.where(j // sub < total_tiles, jnp.maximum(s_hi - s_lo, 0), 0)

    rows = max_tiles * EXP_TM
    start = off[None, :] + base[:-1]
    stop = off[None, :] + base[1:]
    wstart = jnp.minimum(start // COMB_ALIGN * COMB_ALIGN, rows - COMB_WIN)
    need = jnp.where(stop > start, stop - wstart, 0)
    rel = wstart - off[None, :]
    split = COMB_CHUNKS[0][1]
    wtail = jnp.where(need > split, (wstart + split) // COMB_ALIGN, 0)
    return dict(
        disp=(je, jm, s_lo, n_src),
        comb=((wstart // COMB_ALIGN).reshape(-1), wtail.reshape(-1), rel.reshape(-1),
              need.reshape(-1)),
        tiles=(tc, tile_e, tile_pass),
        max_tiles=max_tiles)


def _dispatch_kernel(e_ref, m_ref, slo_ref, n_ref, h_ref, keyt_ref, gatet_ref, o_ref, gs_ref,
                     acc_ref, gacc_ref):
    j = pl.program_id(0)
    td = o_ref.shape[0]
    e = e_ref[j]
    slot = (m_ref[j] * td + lax.broadcasted_iota(jnp.int32, (td, TOK_TILE), 0)).astype(F32)

    def item(i, assign):
        ts = pl.multiple_of((slo_ref[j] + i) * TOK_TILE, TOK_TILE)
        key = keyt_ref[pl.ds(e, 1), pl.ds(ts, TOK_TILE)]
        hit = key == slot
        onehot = jnp.where(hit, 1.0, 0.0).astype(BF16)
        rows = _dot(onehot, h_ref[pl.ds(ts, TOK_TILE), :])
        gate = gatet_ref[pl.ds(e, 1), pl.ds(ts, TOK_TILE)]
        gsum = jnp.sum(jnp.where(hit, gate, 0.0), axis=-1, keepdims=True)
        if assign:
            acc_ref[...] = rows
            gacc_ref[...] = gsum
        else:
            acc_ref[...] += rows
            gacc_ref[...] += gsum

    @pl.when(n_ref[j] == 0)
    def _():
        acc_ref[...] = jnp.zeros_like(acc_ref)
        gacc_ref[...] = jnp.zeros_like(gacc_ref)

    @pl.when(n_ref[j] > 0)
    def _():
        item(0, True)

    def body(i, carry):
        item(i, False)
        return carry

    lax.fori_loop(1, n_ref[j], body, 0)
    o_ref[...] = acc_ref[...].astype(o_ref.dtype)
    gs_ref[...] = gacc_ref[...]


def _dispatch(h2, key_t, gate_t, lists, max_tiles):
    n, d = h2.shape
    rows = max_tiles * EXP_TM
    resident = dict(pipeline_mode=pl.Buffered(1))
    grid_spec = pltpu.PrefetchScalarGridSpec(
        num_scalar_prefetch=4,
        grid=(rows // DISP_TD,),
        in_specs=[
            pl.BlockSpec((n, d), lambda j, *_: (0, 0), **resident),
            pl.BlockSpec((8, n), lambda j, *_: (0, 0), **resident),
            pl.BlockSpec((8, n), lambda j, *_: (0, 0), **resident),
        ],
        out_specs=[pl.BlockSpec((DISP_TD, d), lambda j, *_: (j, 0)),
                   pl.BlockSpec((DISP_TD, 1), lambda j, *_: (j, 0))],
        scratch_shapes=[pltpu.VMEM((DISP_TD, d), F32), pltpu.VMEM((DISP_TD, 1), F32)],
    )
    return pl.pallas_call(
        _dispatch_kernel,
        grid_spec=grid_spec,
        out_shape=[jax.ShapeDtypeStruct((rows, d), BF16),
                   jax.ShapeDtypeStruct((rows, 1), F32)],
        name="moe_dispatch",
    )(*lists, h2, key_t, gate_t)


def _expert_kernel(tidx_ref, te_ref, tp_ref, x_ref, gs_ref, wg_ref, wu_ref, wd_ref, y_ref,
                   acc_ref):
    i = pl.program_id(0)
    f = pl.program_id(1)
    nf = pl.num_programs(1)

    max_units = EXP_TM // EXP_UNIT
    n_steps = D_FF_EXPERT // EXP_TF

    def run(units, first, last):
        done = 0
        while done < units * EXP_UNIT:
            size = EXP_SUB if units * EXP_UNIT - done >= EXP_SUB else EXP_UNIT
            rs = slice(done, done + size)
            done += size
            x = x_ref[rs, :]
            g = _dot(x, wg_ref[0])
            u = _dot(x, wu_ref[0])
            act = (g * jax.nn.sigmoid(g) * u).astype(BF16)
            part = _dot(act, wd_ref[0])
            total = part if first else acc_ref[rs, :] + part
            if last:
                y_ref[rs, :] = (gs_ref[rs, :] * total).astype(y_ref.dtype)
            else:
                acc_ref[rs, :] = total
        if last and done < EXP_TM:
            y_ref[done:, :] = jnp.zeros((EXP_TM - done, y_ref.shape[1]), y_ref.dtype)

    kinds = [(f == 0, True, n_steps == 1), (f == nf - 1, n_steps == 1, True)]
    if n_steps > 2:
        kinds.append((jnp.logical_and(f > 0, f < nf - 1), False, False))
    for units in range(max_units + 1):
        for cond, first, last in kinds[:1] if n_steps == 1 else kinds:
            @pl.when(jnp.logical_and(tp_ref[i] == units, cond))
            def _(units=units, first=first, last=last):
                run(units, first, last)


def _experts(xs, gs, w_gu, w_down, tiles, max_tiles):
    rows, d = xs.shape
    tm, tf = EXP_TM, EXP_TF
    nf = D_FF_EXPERT // tf

    def fsel(i, f, tp):
        return jnp.where(tp[i] > 0, f, nf - 1)

    grid_spec = pltpu.PrefetchScalarGridSpec(
        num_scalar_prefetch=3,
        grid=(max_tiles, nf),
        in_specs=[
            pl.BlockSpec((tm, d), lambda i, f, ti, te, tv: (ti[i], 0)),
            pl.BlockSpec((tm, 1), lambda i, f, ti, te, tv: (ti[i], 0)),
            pl.BlockSpec((1, d, tf), lambda i, f, ti, te, tv: (te[i], 0, fsel(i, f, tv))),
            pl.BlockSpec((1, d, tf), lambda i, f, ti, te, tv: (te[i], 0, nf + fsel(i, f, tv))),
            pl.BlockSpec((1, tf, d), lambda i, f, ti, te, tv: (te[i], fsel(i, f, tv), 0)),
        ],
        out_specs=pl.BlockSpec((tm, d), lambda i, f, ti, te, tv: (i, 0)),
        scratch_shapes=[pltpu.VMEM((tm, d), F32)],
    )
    return pl.pallas_call(
        _expert_kernel,
        grid_spec=grid_spec,
        out_shape=jax.ShapeDtypeStruct((rows, d), BF16),
        name="moe_experts",
    )(*tiles, xs, gs, w_gu, w_gu, w_down)


def _combine_kernel(ws_ref, wt_ref, rel_ref, need_ref, *refs):
    head_refs = refs[:N_EXPERTS]
    tail_refs = refs[N_EXPERTS:2 * N_EXPERTS]
    keyc_ref, x_ref, mod_ref, gpost_ref, o_ref, acc_ref = refs[2 * N_EXPERTS:]
    s = pl.program_id(0)
    ts = acc_ref.shape[0]

    def term(e, y_ref, a):
        key = keyc_ref[:, e:e + 1]
        rel = rel_ref[s * N_EXPERTS + e]
        slot = (rel + a + lax.broadcasted_iota(jnp.int32, (ts, y_ref.shape[0]), 1)).astype(F32)
        onehot = jnp.where(key == slot, 1.0, 0.0).astype(BF16)
        return _dot(onehot, y_ref[...])

    acc = term(0, head_refs[0], 0)
    for e in range(1, N_EXPERTS):
        acc = acc + term(e, head_refs[e], 0)
    acc_ref[...] = acc
    split = COMB_CHUNKS[0][1]
    for e in range(N_EXPERTS):
        @pl.when(need_ref[s * N_EXPERTS + e] > split)
        def _(e=e):
            acc_ref[...] += term(e, tail_refs[e], split)
    gt_f = mod_ref[0, 5:6, :]
    o_ref[...] = x_ref[...] + (1.0 + gt_f) * _rms(acc_ref[...], gpost_ref[...])


def _combine(y, key_c, x2, mod_l, g_post, lists, seq):
    n, d = x2.shape
    per_b = seq // TOK_TILE
    tok = lambda s, *_: (s, 0)

    (a0, b0), (a1, b1) = COMB_CHUNKS

    def head(e):
        return pl.BlockSpec((pl.Element(b0 - a0), pl.Element(d)),
                            lambda s, ws, wt, rel, need: (ws[s * N_EXPERTS + e] * COMB_ALIGN, 0))

    def tail(e):
        return pl.BlockSpec((pl.Element(b1 - a1), pl.Element(d)),
                            lambda s, ws, wt, rel, need: (wt[s * N_EXPERTS + e] * COMB_ALIGN, 0))

    grid_spec = pltpu.PrefetchScalarGridSpec(
        num_scalar_prefetch=4,
        grid=(n // TOK_TILE,),
        in_specs=[head(e) for e in range(N_EXPERTS)] + [tail(e) for e in range(N_EXPERTS)] + [
            pl.BlockSpec((TOK_TILE, LANES), tok),
            pl.BlockSpec((TOK_TILE, d), tok),
            pl.BlockSpec((1, N_MOD, d), lambda s, *_: (s // per_b, 0, 0)),
            pl.BlockSpec((1, d), lambda s, *_: (0, 0)),
        ],
        out_specs=pl.BlockSpec((TOK_TILE, d), tok),
        scratch_shapes=[pltpu.VMEM((TOK_TILE, d), F32)],
    )
    return pl.pallas_call(
        _combine_kernel,
        grid_spec=grid_spec,
        out_shape=jax.ShapeDtypeStruct((n, d), F32),
        name="moe_combine",
    )(*lists, *([y] * (2 * N_EXPERTS)), key_c, x2, mod_l, g_post)


def kernel(x, c, w_ada, b_ada, g_pre_mix, g_post_mix, g_pre_ffn, g_post_ffn, w_in, b_forget, g_v,
           w_spatial, b_spatial, g_out_attn, g_out_sgu, w_out, w_gate_up_dense, w_down_dense,
           w_router, w_gate_up_exp, w_down_exp):
    bsz, seq, d = x.shape
    depth = w_ada.shape[0]
    n = bsz * seq
    npair = N_HEADS // 2
    assert d == D_MODEL and seq % PROJ_TILE == 0 and seq % TOK_TILE == 0 and seq % ATT_TQ == 0

    mod = _modulation(c, w_ada, b_ada).reshape(depth, bsz, N_MOD, d)
    x2 = x.reshape(n, d)
    bf16 = {("w_in", 0): w_in[0].astype(BF16), ("w_out", 0): w_out[0].astype(BF16)}
    for l in range(depth):
        mod_l = mod[l]
        moe = (l % 2 == 1)
        q, k, v, zf_t, u, sv = _inproj(x2, mod_l, g_pre_mix[l].reshape(1, d), bf16[("w_in", l)],
                                       g_v[l].reshape(1, SGU_W), seq)
        riders = {}
        if l + 1 < depth:
            riders[("w_in", l + 1)] = w_in[l + 1]
            riders[("w_out", l + 1)] = w_out[l + 1]
        if moe:
            riders[("w_gu_exp", l // 2)] = w_gate_up_exp[l // 2].reshape(N_EXPERTS * d, -1)
        else:
            riders[("w_gu_dense", l // 2)] = w_gate_up_dense[l // 2]
            riders[("w_down_dense", l // 2)] = w_down_dense[l // 2]
            if l + 1 < depth:
                riders[("w_down_exp", l // 2)] = w_down_exp[l // 2].reshape(-1, d)
        attn, *casts = _attention(q, k, v, zf_t.reshape(npair, 2, n),
                                  b_forget[l].reshape(npair, 2, 1), bsz, seq,
                                  tuple(riders.values()))
        bf16.update(zip(riders.keys(), casts))
        sgun = _sgu(u, sv, w_spatial[l].astype(BF16), b_spatial[l].T,
                    g_out_sgu[l].reshape(1, SGU_W))
        w_r = None
        if moe:
            w_r = jnp.pad(w_router[l // 2], ((0, 0), (0, LANES - N_EXPERTS)))
        res = _outproj(attn, sgun, g_out_attn[l].reshape(1, ATTN_W), bf16[("w_out", l)],
                       x2, mod_l, g_post_mix[l].reshape(1, d), g_pre_ffn[l].reshape(1, d), w_r, seq)
        g_post = g_post_ffn[l].reshape(1, d)
        if not moe:
            x2, h2 = res
            x2 = _ffn_dense(h2, bf16[("w_gu_dense", l // 2)], bf16[("w_down_dense", l // 2)],
                            x2, mod_l, g_post, seq)
        else:
            x2, h2, logits = res
            key_c, gate_t, key_t, ends = _route(logits)
            wl = _work_lists(ends, n)
            xs, gs = _dispatch(h2, key_t, gate_t, wl["disp"], wl["max_tiles"])
            w_gu = bf16[("w_gu_exp", l // 2)].reshape(N_EXPERTS, d, 2 * D_FF_EXPERT)
            w_dn = bf16[("w_down_exp", l // 2)].reshape(N_EXPERTS, D_FF_EXPERT, d)
            y = _experts(xs, gs, w_gu, w_dn, wl["tiles"], wl["max_tiles"])
            x2 = _combine(y, key_c, x2, mod_l, g_post, wl["comb"], seq)
    return x2.reshape(bsz, seq, d)
```

```python
import functools
import math

import jax
import jax.numpy as jnp
from jax import lax
from jax.experimental import pallas as pl
from jax.experimental.pallas import tpu as pltpu

F32 = jnp.float32
BF16 = jnp.bfloat16

D_MODEL = 1024
HEAD_DIM = 64
N_HEADS = 8
N_GROUPS = 8
ATTN_W = N_HEADS * HEAD_DIM
SGU_W = N_GROUPS * HEAD_DIM
CHUNK = 128
D_FF_DENSE = 2816
N_EXPERTS = 8
D_FF_EXPERT = 3584
N_MOD = 6
EPS = 1e-6

LANES = 128
TOK_TILE = 512
PROJ_TILE = 1024
ROW_GROUP = 256
SGU_TILE = 2048
MOD_BLOCKS = 8
ATT_TQ = 256
MXU_TILE = 256
FFN_CHUNKS = ((0, 6 * MXU_TILE), (6 * MXU_TILE, D_FF_DENSE))
EXP_TM = 1024
EXP_SUB = 512
EXP_UNIT = 512
EXP_TF = 7 * MXU_TILE
LOG2E = math.log2(math.e)
DISP_TD = 256
COMB_ALIGN = 16
COMB_WIN = TOK_TILE + COMB_ALIGN
COMB_CHUNKS = ((0, MXU_TILE), (MXU_TILE, COMB_WIN))
UNROUTED = -float(2 ** 30)
GELU_C = math.sqrt(2.0 / math.pi)


def _rms(x, g):
    return x * lax.rsqrt(jnp.mean(x * x, axis=-1, keepdims=True) + EPS) * g


def _dot(a, b):
    return jnp.dot(a, b, preferred_element_type=F32)


def _row_groups(rows):
    return [slice(r, r + ROW_GROUP) for r in range(0, rows, ROW_GROUP)]


def _rider_specs(riders, steps, step_of):
    ins, outs, shapes = [], [], []
    for w, lead in riders:
        _, rows, cols = w.shape
        share = 1
        while rows % (steps // share) or (rows // (steps // share)) % COMB_ALIGN:
            share *= 2
        rb = rows // (steps // share)
        ins.append(pl.BlockSpec(
            (1, rb, cols), lambda *g, lead=lead, share=share: (lead, step_of(*g) // share, 0)))
        outs.append(pl.BlockSpec((rb, cols), lambda *g, share=share: (step_of(*g) // share, 0)))
        shapes.append(jax.ShapeDtypeStruct((rows, cols), BF16))
    return ins, outs, shapes


def _cast_riders(src_refs, dst_refs):
    for src, dst in zip(src_refs, dst_refs):
        dst[...] = src[0].astype(dst.dtype)


def _mod_kernel(c_ref, w_ref, b_ref, *rest):
    n_ride = (len(rest) - 1) // 2
    o_ref = rest[n_ride]
    _cast_riders(rest[:n_ride], rest[n_ride + 1:])
    c = c_ref[...]
    ca = (c * jax.nn.sigmoid(c)).astype(BF16)
    o_ref[0] = _dot(ca, w_ref[0].astype(BF16)) + b_ref[0]


def _modulation(c, w_ada, b_ada, riders=()):
    depth, d, m = w_ada.shape
    bsz = c.shape[0]
    nblk = MOD_BLOCKS
    cb = m // nblk
    r_in, r_out, r_shape = _rider_specs(riders, depth * nblk, lambda l, j: l * nblk + j)
    return pl.pallas_call(
        _mod_kernel,
        grid=(depth, nblk),
        in_specs=[
            pl.BlockSpec((bsz, d), lambda l, j: (0, 0)),
            pl.BlockSpec((1, d, cb), lambda l, j: (l, 0, j)),
            pl.BlockSpec((1, 1, cb), lambda l, j: (l, 0, j)),
        ] + r_in,
        out_specs=[pl.BlockSpec((1, bsz, cb), lambda l, j: (l, 0, j))] + r_out,
        out_shape=[jax.ShapeDtypeStruct((depth, bsz, m), F32)] + r_shape,
        name="adaln_mod",
    )(c, w_ada, b_ada.reshape(depth, 1, m), *[w for w, _ in riders])


def _inproj_kernel(x_ref, mod_ref, gpre_ref, w_ref, gv_ref,
                   q_ref, k_ref, v_ref, zf_ref, u_ref, sv_ref, wuv_ref):
    o3 = 3 * ATTN_W
    o4 = o3 + N_HEADS

    @pl.when(pl.program_id(0) == 0)
    def _():
        wuv_ref[...] = w_ref[:, o4:]

    sh = mod_ref[0, 0:1, :]
    sc = mod_ref[0, 1:2, :]
    for rs in _row_groups(x_ref.shape[0]):
        h = (_rms(x_ref[rs, :], gpre_ref[...]) * (1.0 + sc) + sh).astype(BF16)
        qkv = _dot(h, w_ref[:, :o3])
        q_ref[rs, :] = (qkv[:, :ATTN_W] * (HEAD_DIM ** -0.5 * LOG2E)).astype(BF16)
        k_ref[rs, :] = qkv[:, ATTN_W:2 * ATTN_W].astype(BF16)
        v_ref[rs, :] = qkv[:, 2 * ATTN_W:].astype(BF16)
        zf_ref[:, rs] = _dot(h, w_ref[:, o3:o3 + LANES]).T[:N_HEADS, :]
        uv = _dot(h, wuv_ref[...])
        uv = uv * (0.5 * (1.0 + jnp.tanh(GELU_C * (uv + 0.044715 * (uv * uv * uv)))))
        u_ref[rs, :] = uv[:, :SGU_W].astype(BF16)
        s = uv[:, SGU_W:]
        sc_ = s - jnp.mean(s, axis=-1, keepdims=True)
        sv = sc_ * lax.rsqrt(jnp.mean(sc_ * sc_, axis=-1, keepdims=True) + EPS) * gv_ref[...]
        sv_ref[rs, :] = sv.astype(BF16)


def _inproj(x2, mod_l, g_pre, w_in_l, g_v, seq):
    n, d = x2.shape
    tm = PROJ_TILE
    per_b = seq // tm
    row = lambda i: (i, 0)
    const = lambda i: (0, 0)
    resident = dict(pipeline_mode=pl.Buffered(1))
    outs = [jax.ShapeDtypeStruct((n, ATTN_W), BF16)] * 3 + [
        jax.ShapeDtypeStruct((N_HEADS, n), F32),
        jax.ShapeDtypeStruct((n, SGU_W), BF16),
        jax.ShapeDtypeStruct((n, SGU_W), BF16)]
    return pl.pallas_call(
        _inproj_kernel,
        grid=(n // tm,),
        in_specs=[
            pl.BlockSpec((tm, d), row),
            pl.BlockSpec((1, N_MOD, d), lambda i: (i // per_b, 0, 0)),
            pl.BlockSpec((1, d), const),
            pl.BlockSpec(w_in_l.shape, const, **resident),
            pl.BlockSpec((1, SGU_W), const),
        ],
        out_specs=[pl.BlockSpec((tm, ATTN_W), row)] * 3 + [
            pl.BlockSpec((N_HEADS, tm), lambda i: (0, i)),
            pl.BlockSpec((tm, SGU_W), row),
            pl.BlockSpec((tm, SGU_W), row)],
        out_shape=outs,
        scratch_shapes=[pltpu.VMEM((d, 2 * SGU_W), BF16)],
        name="inproj",
    )(x2, mod_l, g_pre, w_in_l, g_v)


def _cum_log_gates(z):
    ls = jnp.minimum(z, 0.0) - jnp.log1p(jnp.exp(-jnp.abs(z)))
    seq = ls.shape[-1]
    pos = lax.broadcasted_iota(jnp.int32, ls.shape, 1)
    shift = 1
    while shift < seq:
        ls = ls + jnp.where(pos >= shift, pltpu.roll(ls, shift, 1), 0.0)
        shift *= 2
    return ls * LOG2E


def _attn_kernel(q_ref, k_ref, v_ref, zf_ref, bf_ref, *rest):
    n_ride = (len(rest) - 1) // 2
    o_ref = rest[n_ride]
    _cast_riders(rest[:n_ride], rest[n_ride + 1:])
    tq = ATT_TQ
    seq = q_ref.shape[0]
    nq = seq // tq
    z2 = zf_ref[0] + bf_ref[0]
    l_keys = _cum_log_gates(jnp.concatenate([z2, jnp.zeros((8 - 2, seq), F32)], axis=0))
    l_rows = jnp.concatenate([l_keys, jnp.zeros((LANES - 8, seq), F32)], axis=0)
    l_cols = l_rows.T
    lane = lax.broadcasted_iota(jnp.int32, (tq, LANES), 1)
    left = lane < HEAD_DIM
    rows = lax.broadcasted_iota(jnp.int32, (tq, tq), 0)
    cols = lax.broadcasted_iota(jnp.int32, (tq, tq), 1)
    causal = cols <= rows
    zero = jnp.zeros((tq, LANES), BF16)
    one = jnp.ones((tq, LANES), BF16)
    def scores(qi):
        qs = slice(qi * tq, (qi + 1) * tq)
        q2 = q_ref[qs, :]
        qq = jnp.concatenate([jnp.where(left, q2, zero), jnp.where(left, zero, q2)], axis=0)
        t_blocks = [[], []]
        mt = [None, None]
        for j in range(qi + 1):
            ks = slice(j * tq, (j + 1) * tq)
            t2 = lax.dot_general(qq, k_ref[ks, :], (((1,), (1,)), ((), ())),
                                 preferred_element_type=F32)
            for hh in range(2):
                t = t2[hh * tq:(hh + 1) * tq] - l_keys[hh:hh + 1, ks]
                if j == qi:
                    t = jnp.where(causal, t, -jnp.inf)
                t_blocks[hh].append(t)
                mj = jnp.maximum(t[:, :LANES], t[:, LANES:])
                mt[hh] = mj if mt[hh] is None else jnp.maximum(mt[hh], mj)
        shift = []
        for hh in range(2):
            lq = l_cols[qs, hh:hh + 1]
            row_max = jnp.max(mt[hh], axis=-1, keepdims=True) + lq
            shift.append(row_max - lq)
        return t_blocks, shift

    def values(qi, t_blocks, shift):
        qs = slice(qi * tq, (qi + 1) * tq)
        accs = [None, None]
        for j in range(qi + 1):
            ks = slice(j * tq, (j + 1) * tq)
            pp = jnp.concatenate([jnp.exp2(t_blocks[hh][j] - shift[hh]) for hh in range(2)],
                                 axis=0).astype(BF16)
            vaug = jnp.concatenate([v_ref[ks, :], one], axis=1)
            part = _dot(pp, vaug)
            accs[j % 2] = part if accs[j % 2] is None else accs[j % 2] + part
        acc = accs[0] if accs[1] is None else accs[0] + accs[1]
        out0 = acc[:tq, :LANES] / acc[:tq, LANES:]
        out1 = acc[tq:, :LANES] / acc[tq:, LANES:]
        o_ref[qs, :] = jnp.where(left, out0, out1).astype(o_ref.dtype)

    order = list(range(nq - 1, -1, -1))
    pending = scores(order[0])
    for pos, qi in enumerate(order):
        nxt = scores(order[pos + 1]) if pos + 1 < nq else None
        values(qi, *pending)
        pending = nxt


def _attention(q, k, v, zf_pairs, bf_pairs, bsz, seq, riders=()):
    n = q.shape[0]
    npair = N_HEADS // 2
    blk = pl.BlockSpec((seq, LANES), lambda b, p: (b, p))
    r_in, r_out, r_shape = _rider_specs(riders, bsz * npair, lambda b, p: b * npair + p)
    return pl.pallas_call(
        _attn_kernel,
        grid=(bsz, npair),
        in_specs=[blk, blk, blk, pl.BlockSpec((1, 2, seq), lambda b, p: (p, 0, b)),
                  pl.BlockSpec((1, 2, 1), lambda b, p: (p, 0, 0))] + r_in,
        out_specs=[blk] + r_out,
        out_shape=[jax.ShapeDtypeStruct((n, ATTN_W), BF16)] + r_shape,
        name="fox_attention",
    )(q, k, v, zf_pairs, bf_pairs, *[w for w, _ in riders])


def _sgu_kernel(u_ref, sv_ref, w_ref, bt_ref, g_ref, o_ref):
    tm = u_ref.shape[0]
    npair = N_GROUPS // 2
    lane = lax.broadcasted_iota(jnp.int32, (CHUNK, LANES), 1)
    left = lane < HEAD_DIM
    r = lax.broadcasted_iota(jnp.int32, (CHUNK, CHUNK), 0)
    c = lax.broadcasted_iota(jnp.int32, (CHUNK, CHUNK), 1)
    causal = c <= r
    lhs, bias = [], []
    for j in range(npair):
        wa = jnp.where(causal, w_ref[2 * j], jnp.zeros((), BF16))
        wb = jnp.where(causal, w_ref[2 * j + 1], jnp.zeros((), BF16))
        lhs.append(jnp.concatenate([wa, wb], axis=1))
        bias.append(jnp.where(left, bt_ref[:, 2 * j:2 * j + 1], bt_ref[:, 2 * j + 1:2 * j + 2]))
    for ci in range(tm // CHUNK):
        rs = slice(ci * CHUNK, (ci + 1) * CHUNK)
        blks = []
        ssq = jnp.zeros((CHUNK, 1), F32)
        for j in range(npair):
            cs = slice(j * LANES, (j + 1) * LANES)
            svb = sv_ref[rs, cs]
            zero = jnp.zeros_like(svb)
            rhs = jnp.concatenate([jnp.where(left, svb, zero), jnp.where(left, zero, svb)], axis=0)
            mixed = _dot(lhs[j], rhs) + bias[j]
            ob = u_ref[rs, cs].astype(F32) * mixed
            ssq = ssq + jnp.sum(ob * ob, axis=-1, keepdims=True)
            blks.append(ob)
        inv = lax.rsqrt(ssq * (1.0 / SGU_W) + EPS)
        for j in range(npair):
            cs = slice(j * LANES, (j + 1) * LANES)
            o_ref[rs, cs] = (blks[j] * inv * g_ref[:, cs]).astype(o_ref.dtype)


def _sgu(u, sv, w_sp, b_sp_t, g_out):
    n = u.shape[0]
    tm = SGU_TILE
    row = lambda i: (i, 0)
    return pl.pallas_call(
        _sgu_kernel,
        grid=(n // tm,),
        in_specs=[
            pl.BlockSpec((tm, SGU_W), row),
            pl.BlockSpec((tm, SGU_W), row),
            pl.BlockSpec(w_sp.shape, lambda i: (0, 0, 0)),
            pl.BlockSpec(b_sp_t.shape, lambda i: (0, 0)),
            pl.BlockSpec((1, SGU_W), lambda i: (0, 0)),
        ],
        out_specs=pl.BlockSpec((tm, SGU_W), row),
        out_shape=jax.ShapeDtypeStruct((n, SGU_W), BF16),
        name="sgu",
    )(u, sv, w_sp, b_sp_t, g_out)


def _outproj_kernel(a_ref, s_ref, ga_ref, wo_ref, x_ref, mod_ref, gpost_ref, gffn_ref,
                    *rest, with_router):
    if with_router:
        wr_ref, xo_ref, h_ref, lg_ref = rest
    else:
        xo_ref, h_ref = rest
    gt_m = mod_ref[0, 2:3, :]
    sh_f = mod_ref[0, 3:4, :]
    sc_f = mod_ref[0, 4:5, :]
    if with_router:
        wr = wr_ref[...]
        w_hi = wr.astype(BF16)
        w_lo = (wr - w_hi.astype(F32)).astype(BF16)
    for rs in _row_groups(x_ref.shape[0]):
        a = _rms(a_ref[rs, :].astype(F32), ga_ref[...]).astype(BF16)
        y = _dot(a, wo_ref[:ATTN_W, :]) + _dot(s_ref[rs, :], wo_ref[ATTN_W:, :])
        xn = x_ref[rs, :] + (1.0 + gt_m) * _rms(y, gpost_ref[...])
        xo_ref[rs, :] = xn
        h = _rms(xn, gffn_ref[...]) * (1.0 + sc_f) + sh_f
        h_hi = h.astype(BF16)
        h_ref[rs, :] = h_hi
        if with_router:
            h_lo = (h - h_hi.astype(F32)).astype(BF16)
            lg_ref[rs, :] = _dot(h_hi, w_hi) + (_dot(h_hi, w_lo) + _dot(h_lo, w_hi))


def _outproj(attn, sgun, g_attn, w_o, x2, mod_l, g_post, g_ffn, w_router, seq):
    n, d = x2.shape
    tm = PROJ_TILE
    per_b = seq // tm
    row = lambda i: (i, 0)
    const = lambda i: (0, 0)
    with_router = w_router is not None
    in_specs = [
        pl.BlockSpec((tm, ATTN_W), row),
        pl.BlockSpec((tm, SGU_W), row),
        pl.BlockSpec((1, ATTN_W), const),
        pl.BlockSpec(w_o.shape, const),
        pl.BlockSpec((tm, d), row),
        pl.BlockSpec((1, N_MOD, d), lambda i: (i // per_b, 0, 0)),
        pl.BlockSpec((1, d), const),
        pl.BlockSpec((1, d), const),
    ]
    args = [attn, sgun, g_attn, w_o, x2, mod_l, g_post, g_ffn]
    out_specs = [pl.BlockSpec((tm, d), row), pl.BlockSpec((tm, d), row)]
    out_shape = [jax.ShapeDtypeStruct((n, d), F32), jax.ShapeDtypeStruct((n, d), BF16)]
    if with_router:
        in_specs.append(pl.BlockSpec(w_router.shape, const))
        args.append(w_router)
        out_specs.append(pl.BlockSpec((tm, LANES), row))
        out_shape.append(jax.ShapeDtypeStruct((n, LANES), F32))
    return pl.pallas_call(
        functools.partial(_outproj_kernel, with_router=with_router),
        grid=(n // tm,),
        in_specs=in_specs,
        out_specs=out_specs,
        out_shape=out_shape,
        name="outproj_router" if with_router else "outproj",
    )(*args)


def _ffn_kernel(h_ref, wgu_ref, wd_ref, x_ref, mod_ref, gpost_ref, o_ref):
    h = h_ref[...]
    y = None
    for a, b in FFN_CHUNKS:
        g = _dot(h, wgu_ref[:, a:b])
        u = _dot(h, wgu_ref[:, D_FF_DENSE + a:D_FF_DENSE + b])
        act = (g * jax.nn.sigmoid(g) * u).astype(BF16)
        part = _dot(act, wd_ref[a:b, :])
        y = part if y is None else y + part
    gt_f = mod_ref[0, 5:6, :]
    o_ref[...] = x_ref[...] + (1.0 + gt_f) * _rms(y, gpost_ref[...])


def _ffn_dense(h2, w_gu, w_down, x2, mod_l, g_post, seq):
    n, d = x2.shape
    tm = TOK_TILE
    per_b = seq // tm
    resident = dict(pipeline_mode=pl.Buffered(1))
    return pl.pallas_call(
        _ffn_kernel,
        grid=(n // tm,),
        in_specs=[
            pl.BlockSpec((tm, d), lambda i: (i, 0)),
            pl.BlockSpec(w_gu.shape, lambda i: (0, 0), **resident),
            pl.BlockSpec(w_down.shape, lambda i: (0, 0), **resident),
            pl.BlockSpec((tm, d), lambda i: (i, 0)),
            pl.BlockSpec((1, N_MOD, d), lambda i: (i // per_b, 0, 0)),
            pl.BlockSpec((1, d), lambda i: (0, 0)),
        ],
        out_specs=pl.BlockSpec((tm, d), lambda i: (i, 0)),
        out_shape=jax.ShapeDtypeStruct((n, d), F32),
        name="ffn_dense",
    )(h2, w_gu, w_down, x2, mod_l, g_post)


def _route_kernel(lg_ref, keyc_ref, gate_ref, keyt_ref, ends_ref, base_ref):
    s = pl.program_id(0)
    tm = lg_ref.shape[0]

    @pl.when(s == 0)
    def _():
        base_ref[...] = jnp.zeros_like(base_ref)

    lane = lax.broadcasted_iota(jnp.int32, (tm, LANES), 1)
    lg = jnp.where(lane < N_EXPERTS, lg_ref[...], -jnp.inf)
    m1 = jnp.max(lg, axis=-1, keepdims=True)
    i1 = jnp.min(jnp.where(lg == m1, lane, LANES), axis=-1, keepdims=True)
    sel1 = lane == i1
    lg2 = jnp.where(sel1, -jnp.inf, lg)
    m2 = jnp.max(lg2, axis=-1, keepdims=True)
    i2 = jnp.min(jnp.where(lg2 == m2, lane, LANES), axis=-1, keepdims=True)
    sel2 = lane == i2
    e2 = jnp.exp(m2 - m1)
    w1 = 1.0 / (1.0 + e2)
    w2 = e2 / (1.0 + e2)
    gate_ref[...] = (jnp.where(sel1, w1, 0.0) + jnp.where(sel2, w2, 0.0)).T
    chosen = jnp.logical_or(sel1, sel2)
    mask = jnp.where(chosen, 1.0, 0.0)
    r = lax.broadcasted_iota(jnp.int32, (tm, tm), 0)
    c = lax.broadcasted_iota(jnp.int32, (tm, tm), 1)
    tri = jnp.where(c < r, 1.0, 0.0).astype(BF16)
    base = base_ref[0:1, :]
    rank = _dot(tri, mask.astype(BF16)) + base
    key = jnp.where(chosen, rank, UNROUTED)
    keyc_ref[...] = key
    keyt_ref[...] = key.T
    new_base = base + jnp.sum(mask, axis=0, keepdims=True)
    base_ref[...] = jnp.broadcast_to(new_base, base_ref.shape)
    ends_ref[0] = jnp.broadcast_to(new_base, ends_ref.shape[1:])


def _route(logits):
    n = logits.shape[0]
    tm = TOK_TILE
    ns = n // tm
    row = lambda s: (s, 0)
    return pl.pallas_call(
        _route_kernel,
        grid=(ns,),
        in_specs=[pl.BlockSpec((tm, LANES), row)],
        out_specs=[
            pl.BlockSpec((tm, LANES), row),
            pl.BlockSpec((LANES, tm), lambda s: (0, s)),
            pl.BlockSpec((LANES, tm), lambda s: (0, s)),
            pl.BlockSpec((1, 8, LANES), lambda s: (s, 0, 0)),
        ],
        out_shape=[
            jax.ShapeDtypeStruct((n, LANES), F32),
            jax.ShapeDtypeStruct((LANES, n), F32),
            jax.ShapeDtypeStruct((LANES, n), F32),
            jax.ShapeDtypeStruct((ns, 8, LANES), F32),
        ],
        scratch_shapes=[pltpu.VMEM((8, LANES), F32)],
        name="moe_route",
    )(logits)


def _work_lists(ends, n_tokens):
    e_n = N_EXPERTS
    sub = EXP_TM // DISP_TD
    max_tiles = (2 * n_tokens) // EXP_TM + e_n
    i32 = jnp.int32

    ends_i = ends[:, 0, :e_n].astype(i32)
    base = jnp.concatenate([jnp.zeros((1, e_n), i32), ends_i], axis=0)
    cnt = base[-1]
    ntile = (cnt + EXP_TM - 1) // EXP_TM
    tile_end = jnp.cumsum(ntile)
    tile_start = tile_end - ntile
    total_tiles = tile_end[-1]
    nunit = (cnt + EXP_UNIT - 1) // EXP_UNIT
    off = tile_start * EXP_TM

    e_ids = jnp.arange(e_n, dtype=i32)

    def expert_of(tile):
        e = jnp.sum((tile_end[None, :] <= tile[:, None]).astype(i32), axis=1)
        return e, (e[:, None] == e_ids[None, :]).astype(i32)

    def pick(onehot, per_expert):
        return jnp.sum(onehot * per_expert[None, :], axis=1)

    t = jnp.arange(max_tiles, dtype=i32)
    tc = jnp.minimum(t, total_tiles - 1)
    tile_e, oh_t = expert_of(tc)
    tile_pass = jnp.clip(pick(oh_t, nunit) - (tc - pick(oh_t, tile_start)) * (EXP_TM // EXP_UNIT),
                         1, EXP_TM // EXP_UNIT)
    tile_pass = jnp.where(t < total_tiles, tile_pass, 0)

    j = jnp.arange(max_tiles * sub, dtype=i32)
    jt = jnp.minimum(j // sub, total_tiles - 1)
    je, oh_j = expert_of(jt)
    jm = (jt - pick(oh_j, tile_start)) * sub + j % sub
    lo_rank = jm * DISP_TD
    hi_rank = jnp.minimum(lo_rank + DISP_TD, pick(oh_j, cnt))
    base_e = jnp.sum(oh_j[:, :, None] * base.T[None, :, :], axis=1)
    s_lo = jnp.sum((base_e[:, 1:] <= lo_rank[:, None]).astype(i32), axis=1)
    s_hi = jnp.sum((base_e[:, :-1] < hi_rank[:, None]).astype(i32), axis=1)
    n_src = jnp.where(j // sub < total_tiles, jnp.maximum(s_hi - s_lo, 0), 0)

    rows = max_tiles * EXP_TM
    start = off[None, :] + base[:-1]
    stop = off[None, :] + base[1:]
    wstart = jnp.minimum(start // COMB_ALIGN * COMB_ALIGN, rows - COMB_WIN)
    need = jnp.where(stop > start, stop - wstart, 0)
    rel = wstart - off[None, :]
    split = COMB_CHUNKS[0][1]
    wtail = jnp.where(need > split, (wstart + split) // COMB_ALIGN, 0)
    return dict(
        disp=(je, jm, s_lo, n_src),
        comb=((wstart // COMB_ALIGN).reshape(-1), wtail.reshape(-1), rel.reshape(-1),
              need.reshape(-1)),
        tiles=(tc, tile_e, tile_pass),
        max_tiles=max_tiles)


def _dispatch_kernel(e_ref, m_ref, slo_ref, n_ref, h_ref, keyt_ref, gatet_ref, o_ref, gs_ref,
                     acc_ref, gacc_ref):
    j = pl.program_id(0)
    td = o_ref.shape[0]
    e = e_ref[j]
    slot = (m_ref[j] * td + lax.broadcasted_iota(jnp.int32, (td, TOK_TILE), 0)).astype(F32)

    def item(i, assign):
        ts = pl.multiple_of((slo_ref[j] + i) * TOK_TILE, TOK_TILE)
        key = keyt_ref[pl.ds(e, 1), pl.ds(ts, TOK_TILE)]
        hit = key == slot
        onehot = jnp.where(hit, 1.0, 0.0).astype(BF16)
        rows = _dot(onehot, h_ref[pl.ds(ts, TOK_TILE), :])
        gate = gatet_ref[pl.ds(e, 1), pl.ds(ts, TOK_TILE)]
        gsum = jnp.sum(jnp.where(hit, gate, 0.0), axis=-1, keepdims=True)
        if assign:
            acc_ref[...] = rows
            gacc_ref[...] = gsum
        else:
            acc_ref[...] += rows
            gacc_ref[...] += gsum

    @pl.when(n_ref[j] == 0)
    def _():
        acc_ref[...] = jnp.zeros_like(acc_ref)
        gacc_ref[...] = jnp.zeros_like(gacc_ref)

    @pl.when(n_ref[j] > 0)
    def _():
        item(0, True)

    def body(i, carry):
        item(i, False)
        return carry

    lax.fori_loop(1, n_ref[j], body, 0)
    o_ref[...] = acc_ref[...].astype(o_ref.dtype)
    gs_ref[...] = gacc_ref[...]


def _dispatch(h2, key_t, gate_t, lists, max_tiles):
    n, d = h2.shape
    rows = max_tiles * EXP_TM
    resident = dict(pipeline_mode=pl.Buffered(1))
    grid_spec = pltpu.PrefetchScalarGridSpec(
        num_scalar_prefetch=4,
        grid=(rows // DISP_TD,),
        in_specs=[
            pl.BlockSpec((n, d), lambda j, *_: (0, 0), **resident),
            pl.BlockSpec((8, n), lambda j, *_: (0, 0), **resident),
            pl.BlockSpec((8, n), lambda j, *_: (0, 0), **resident),
        ],
        out_specs=[pl.BlockSpec((DISP_TD, d), lambda j, *_: (j, 0)),
                   pl.BlockSpec((DISP_TD, 1), lambda j, *_: (j, 0))],
        scratch_shapes=[pltpu.VMEM((DISP_TD, d), F32), pltpu.VMEM((DISP_TD, 1), F32)],
    )
    return pl.pallas_call(
        _dispatch_kernel,
        grid_spec=grid_spec,
        out_shape=[jax.ShapeDtypeStruct((rows, d), BF16),
                   jax.ShapeDtypeStruct((rows, 1), F32)],
        name="moe_dispatch",
    )(*lists, h2, key_t, gate_t)


def _expert_kernel(tidx_ref, te_ref, tp_ref, x_ref, gs_ref, wg_ref, wu_ref, wd_ref, y_ref,
                   acc_ref):
    i = pl.program_id(0)
    f = pl.program_id(1)
    nf = pl.num_programs(1)

    max_units = EXP_TM // EXP_UNIT
    n_steps = D_FF_EXPERT // EXP_TF

    def run(units, first, last):
        done = 0
        while done < units * EXP_UNIT:
            size = EXP_SUB if units * EXP_UNIT - done >= EXP_SUB else EXP_UNIT
            rs = slice(done, done + size)
            done += size
            x = x_ref[rs, :]
            g = _dot(x, wg_ref[0])
            u = _dot(x, wu_ref[0])
            act = (g * jax.nn.sigmoid(g) * u).astype(BF16)
            part = _dot(act, wd_ref[0])
            total = part if first else acc_ref[rs, :] + part
            if last:
                y_ref[rs, :] = (gs_ref[rs, :] * total).astype(y_ref.dtype)
            else:
                acc_ref[rs, :] = total
        if last and done < EXP_TM:
            y_ref[done:, :] = jnp.zeros((EXP_TM - done, y_ref.shape[1]), y_ref.dtype)

    kinds = [(f == 0, True, n_steps == 1), (f == nf - 1, n_steps == 1, True)]
    if n_steps > 2:
        kinds.append((jnp.logical_and(f > 0, f < nf - 1), False, False))
    for units in range(max_units + 1):
        for cond, first, last in kinds[:1] if n_steps == 1 else kinds:
            @pl.when(jnp.logical_and(tp_ref[i] == units, cond))
            def _(units=units, first=first, last=last):
                run(units, first, last)


def _experts(xs, gs, w_gu, w_down, tiles, max_tiles):
    rows, d = xs.shape
    tm, tf = EXP_TM, EXP_TF
    nf = D_FF_EXPERT // tf

    def fsel(i, f, tp):
        return jnp.where(tp[i] > 0, f, nf - 1)

    grid_spec = pltpu.PrefetchScalarGridSpec(
        num_scalar_prefetch=3,
        grid=(max_tiles, nf),
        in_specs=[
            pl.BlockSpec((tm, d), lambda i, f, ti, te, tv: (ti[i], 0)),
            pl.BlockSpec((tm, 1), lambda i, f, ti, te, tv: (ti[i], 0)),
            pl.BlockSpec((1, d, tf), lambda i, f, ti, te, tv: (te[i], 0, fsel(i, f, tv))),
            pl.BlockSpec((1, d, tf), lambda i, f, ti, te, tv: (te[i], 0, nf + fsel(i, f, tv))),
            pl.BlockSpec((1, tf, d), lambda i, f, ti, te, tv: (te[i], fsel(i, f, tv), 0)),
        ],
        out_specs=pl.BlockSpec((tm, d), lambda i, f, ti, te, tv: (i, 0)),
        scratch_shapes=[pltpu.VMEM((tm, d), F32)],
    )
    return pl.pallas_call(
        _expert_kernel,
        grid_spec=grid_spec,
        out_shape=jax.ShapeDtypeStruct((rows, d), BF16),
        name="moe_experts",
    )(*tiles, xs, gs, w_gu, w_gu, w_down)


def _combine_kernel(ws_ref, wt_ref, rel_ref, need_ref, *refs):
    head_refs = refs[:N_EXPERTS]
    tail_refs = refs[N_EXPERTS:2 * N_EXPERTS]
    keyc_ref, x_ref, mod_ref, gpost_ref, o_ref, acc_ref = refs[2 * N_EXPERTS:]
    s = pl.program_id(0)
    ts = acc_ref.shape[0]

    def term(e, y_ref, a):
        key = keyc_ref[:, e:e + 1]
        rel = rel_ref[s * N_EXPERTS + e]
        slot = (rel + a + lax.broadcasted_iota(jnp.int32, (ts, y_ref.shape[0]), 1)).astype(F32)
        onehot = jnp.where(key == slot, 1.0, 0.0).astype(BF16)
        return _dot(onehot, y_ref[...])

    acc = term(0, head_refs[0], 0)
    for e in range(1, N_EXPERTS):
        acc = acc + term(e, head_refs[e], 0)
    acc_ref[...] = acc
    split = COMB_CHUNKS[0][1]
    for e in range(N_EXPERTS):
        @pl.when(need_ref[s * N_EXPERTS + e] > split)
        def _(e=e):
            acc_ref[...] += term(e, tail_refs[e], split)
    gt_f = mod_ref[0, 5:6, :]
    o_ref[...] = x_ref[...] + (1.0 + gt_f) * _rms(acc_ref[...], gpost_ref[...])


def _combine(y, key_c, x2, mod_l, g_post, lists, seq):
    n, d = x2.shape
    per_b = seq // TOK_TILE
    tok = lambda s, *_: (s, 0)

    (a0, b0), (a1, b1) = COMB_CHUNKS

    def head(e):
        return pl.BlockSpec((pl.Element(b0 - a0), pl.Element(d)),
                            lambda s, ws, wt, rel, need: (ws[s * N_EXPERTS + e] * COMB_ALIGN, 0))

    def tail(e):
        return pl.BlockSpec((pl.Element(b1 - a1), pl.Element(d)),
                            lambda s, ws, wt, rel, need: (wt[s * N_EXPERTS + e] * COMB_ALIGN, 0))

    grid_spec = pltpu.PrefetchScalarGridSpec(
        num_scalar_prefetch=4,
        grid=(n // TOK_TILE,),
        in_specs=[head(e) for e in range(N_EXPERTS)] + [tail(e) for e in range(N_EXPERTS)] + [
            pl.BlockSpec((TOK_TILE, LANES), tok),
            pl.BlockSpec((TOK_TILE, d), tok),
            pl.BlockSpec((1, N_MOD, d), lambda s, *_: (s // per_b, 0, 0)),
            pl.BlockSpec((1, d), lambda s, *_: (0, 0)),
        ],
        out_specs=pl.BlockSpec((TOK_TILE, d), tok),
        scratch_shapes=[pltpu.VMEM((TOK_TILE, d), F32)],
    )
    return pl.pallas_call(
        _combine_kernel,
        grid_spec=grid_spec,
        out_shape=jax.ShapeDtypeStruct((n, d), F32),
        name="moe_combine",
    )(*lists, *([y] * (2 * N_EXPERTS)), key_c, x2, mod_l, g_post)


def kernel(x, c, w_ada, b_ada, g_pre_mix, g_post_mix, g_pre_ffn, g_post_ffn, w_in, b_forget, g_v,
           w_spatial, b_spatial, g_out_attn, g_out_sgu, w_out, w_gate_up_dense, w_down_dense,
           w_router, w_gate_up_exp, w_down_exp):
    bsz, seq, d = x.shape
    depth = w_ada.shape[0]
    n = bsz * seq
    npair = N_HEADS // 2
    assert d == D_MODEL and seq % PROJ_TILE == 0 and seq % TOK_TILE == 0 and seq % ATT_TQ == 0

    w_gu_exp3 = w_gate_up_exp.reshape(-1, N_EXPERTS * d, 2 * D_FF_EXPERT)
    w_down_exp3 = w_down_exp.reshape(-1, N_EXPERTS * D_FF_EXPERT, d)
    riders = {("w_in", 0): (w_in, 0), ("w_out", 0): (w_out, 0)}
    mod, *casts = _modulation(c, w_ada, b_ada, tuple(riders.values()))
    bf16 = dict(zip(riders.keys(), casts))
    mod = mod.reshape(depth, bsz, N_MOD, d)
    x2 = x.reshape(n, d)
    for l in range(depth):
        mod_l = mod[l]
        moe = (l % 2 == 1)
        q, k, v, zf_t, u, sv = _inproj(x2, mod_l, g_pre_mix[l].reshape(1, d), bf16[("w_in", l)],
                                       g_v[l].reshape(1, SGU_W), seq)
        riders = {}
        if l + 1 < depth:
            riders[("w_in", l + 1)] = (w_in, l + 1)
            riders[("w_out", l + 1)] = (w_out, l + 1)
        if moe:
            riders[("w_gu_exp", l // 2)] = (w_gu_exp3, l // 2)
        else:
            riders[("w_gu_dense", l // 2)] = (w_gate_up_dense, l // 2)
            riders[("w_down_dense", l // 2)] = (w_down_dense, l // 2)
            if l + 1 < depth:
                riders[("w_down_exp", l // 2)] = (w_down_exp3, l // 2)
        attn, *casts = _attention(q, k, v, zf_t.reshape(npair, 2, n),
                                  b_forget[l].reshape(npair, 2, 1), bsz, seq,
                                  tuple(riders.values()))
        bf16.update(zip(riders.keys(), casts))
        sgun = _sgu(u, sv, w_spatial[l].astype(BF16), b_spatial[l].T,
                    g_out_sgu[l].reshape(1, SGU_W))
        w_r = None
        if moe:
            w_r = jnp.pad(w_router[l // 2], ((0, 0), (0, LANES - N_EXPERTS)))
        res = _outproj(attn, sgun, g_out_attn[l].reshape(1, ATTN_W), bf16[("w_out", l)],
                       x2, mod_l, g_post_mix[l].reshape(1, d), g_pre_ffn[l].reshape(1, d), w_r, seq)
        g_post = g_post_ffn[l].reshape(1, d)
        if not moe:
            x2, h2 = res
            x2 = _ffn_dense(h2, bf16[("w_gu_dense", l // 2)], bf16[("w_down_dense", l // 2)],
                            x2, mod_l, g_post, seq)
        else:
            x2, h2, logits = res
            key_c, gate_t, key_t, ends = _route(logits)
            wl = _work_lists(ends, n)
            xs, gs = _dispatch(h2, key_t, gate_t, wl["disp"], wl["max_tiles"])
            w_gu = bf16[("w_gu_exp", l // 2)].reshape(N_EXPERTS, d, 2 * D_FF_EXPERT)
            w_dn = bf16[("w_down_exp", l // 2)].reshape(N_EXPERTS, D_FF_EXPERT, d)
            y = _experts(xs, gs, w_gu, w_dn, wl["tiles"], wl["max_tiles"])
            x2 = _combine(y, key_c, x2, mod_l, g_post, wl["comb"], seq)
    return x2.reshape(bsz, seq, d)
```

```python
import functools
import math

import jax
import jax.numpy as jnp
from jax import lax
from jax.experimental import pallas as pl
from jax.experimental.pallas import tpu as pltpu

F32 = jnp.float32
BF16 = jnp.bfloat16

D_MODEL = 1024
HEAD_DIM = 64
N_HEADS = 8
N_GROUPS = 8
ATTN_W = N_HEADS * HEAD_DIM
SGU_W = N_GROUPS * HEAD_DIM
CHUNK = 128
D_FF_DENSE = 2816
N_EXPERTS = 8
D_FF_EXPERT = 3584
N_MOD = 6
EPS = 1e-6

LANES = 128
TOK_TILE = 512
PROJ_TILE = 1024
ROW_GROUP = 256
SGU_TILE = 2048
MOD_BLOCKS = 8
ATT_TQ = 256
MXU_TILE = 256
FFN_CHUNKS = ((0, 6 * MXU_TILE), (6 * MXU_TILE, D_FF_DENSE))
EXP_TM = 1024
EXP_SUB = 512
EXP_UNIT = 512
EXP_TF = 7 * MXU_TILE
LOG2E = math.log2(math.e)
DISP_TD = 256
DISP_PER_STEP = 4
COMB_ALIGN = 16
COMB_WIN = TOK_TILE + COMB_ALIGN
COMB_CHUNKS = ((0, MXU_TILE), (MXU_TILE, COMB_WIN))
UNROUTED = -float(2 ** 30)
GELU_C = math.sqrt(2.0 / math.pi)


def _rms(x, g):
    return x * lax.rsqrt(jnp.mean(x * x, axis=-1, keepdims=True) + EPS) * g


def _dot(a, b):
    return jnp.dot(a, b, preferred_element_type=F32)


def _row_groups(rows):
    return [slice(r, r + ROW_GROUP) for r in range(0, rows, ROW_GROUP)]


def _rider_specs(riders, steps, step_of):
    ins, outs, shapes = [], [], []
    for w, lead in riders:
        _, rows, cols = w.shape
        share = 1
        while rows % (steps // share) or (rows // (steps // share)) % COMB_ALIGN:
            share *= 2
        rb = rows // (steps // share)
        ins.append(pl.BlockSpec(
            (1, rb, cols), lambda *g, lead=lead, share=share: (lead, step_of(*g) // share, 0)))
        outs.append(pl.BlockSpec((rb, cols), lambda *g, share=share: (step_of(*g) // share, 0)))
        shapes.append(jax.ShapeDtypeStruct((rows, cols), BF16))
    return ins, outs, shapes


def _cast_riders(src_refs, dst_refs):
    for src, dst in zip(src_refs, dst_refs):
        dst[...] = src[0].astype(dst.dtype)


def _mod_kernel(c_ref, w_ref, b_ref, *rest):
    n_ride = (len(rest) - 1) // 2
    o_ref = rest[n_ride]
    _cast_riders(rest[:n_ride], rest[n_ride + 1:])
    c = c_ref[...]
    ca = (c * jax.nn.sigmoid(c)).astype(BF16)
    o_ref[0] = _dot(ca, w_ref[0].astype(BF16)) + b_ref[0]


def _modulation(c, w_ada, b_ada, riders=()):
    depth, d, m = w_ada.shape
    bsz = c.shape[0]
    nblk = MOD_BLOCKS
    cb = m // nblk
    r_in, r_out, r_shape = _rider_specs(riders, depth * nblk, lambda l, j: l * nblk + j)
    return pl.pallas_call(
        _mod_kernel,
        grid=(depth, nblk),
        in_specs=[
            pl.BlockSpec((bsz, d), lambda l, j: (0, 0)),
            pl.BlockSpec((1, d, cb), lambda l, j: (l, 0, j)),
            pl.BlockSpec((1, 1, cb), lambda l, j: (l, 0, j)),
        ] + r_in,
        out_specs=[pl.BlockSpec((1, bsz, cb), lambda l, j: (l, 0, j))] + r_out,
        out_shape=[jax.ShapeDtypeStruct((depth, bsz, m), F32)] + r_shape,
        name="adaln_mod",
    )(c, w_ada, b_ada.reshape(depth, 1, m), *[w for w, _ in riders])


def _inproj_kernel(x_ref, mod_ref, gpre_ref, w_ref, gv_ref,
                   q_ref, k_ref, v_ref, zf_ref, u_ref, sv_ref, wuv_ref):
    o3 = 3 * ATTN_W
    o4 = o3 + N_HEADS

    @pl.when(pl.program_id(0) == 0)
    def _():
        wuv_ref[...] = w_ref[:, o4:]

    sh = mod_ref[0, 0:1, :]
    sc = mod_ref[0, 1:2, :]
    for rs in _row_groups(x_ref.shape[0]):
        h = (_rms(x_ref[rs, :], gpre_ref[...]) * (1.0 + sc) + sh).astype(BF16)
        qkv = _dot(h, w_ref[:, :o3])
        q_ref[rs, :] = (qkv[:, :ATTN_W] * (HEAD_DIM ** -0.5 * LOG2E)).astype(BF16)
        k_ref[rs, :] = qkv[:, ATTN_W:2 * ATTN_W].astype(BF16)
        v_ref[rs, :] = qkv[:, 2 * ATTN_W:].astype(BF16)
        zf_ref[:, rs] = _dot(h, w_ref[:, o3:o3 + LANES]).T[:N_HEADS, :]
        uv = _dot(h, wuv_ref[...])
        uv = uv * (0.5 * (1.0 + jnp.tanh(GELU_C * (uv + 0.044715 * (uv * uv * uv)))))
        u_ref[rs, :] = uv[:, :SGU_W].astype(BF16)
        s = uv[:, SGU_W:]
        sc_ = s - jnp.mean(s, axis=-1, keepdims=True)
        sv = sc_ * lax.rsqrt(jnp.mean(sc_ * sc_, axis=-1, keepdims=True) + EPS) * gv_ref[...]
        sv_ref[rs, :] = sv.astype(BF16)


def _inproj(x2, mod_l, g_pre, w_in_l, g_v, seq):
    n, d = x2.shape
    tm = PROJ_TILE
    per_b = seq // tm
    row = lambda i: (i, 0)
    const = lambda i: (0, 0)
    resident = dict(pipeline_mode=pl.Buffered(1))
    outs = [jax.ShapeDtypeStruct((n, ATTN_W), BF16)] * 3 + [
        jax.ShapeDtypeStruct((N_HEADS, n), F32),
        jax.ShapeDtypeStruct((n, SGU_W), BF16),
        jax.ShapeDtypeStruct((n, SGU_W), BF16)]
    return pl.pallas_call(
        _inproj_kernel,
        grid=(n // tm,),
        in_specs=[
            pl.BlockSpec((tm, d), row),
            pl.BlockSpec((1, N_MOD, d), lambda i: (i // per_b, 0, 0)),
            pl.BlockSpec((1, d), const),
            pl.BlockSpec(w_in_l.shape, const, **resident),
            pl.BlockSpec((1, SGU_W), const),
        ],
        out_specs=[pl.BlockSpec((tm, ATTN_W), row)] * 3 + [
            pl.BlockSpec((N_HEADS, tm), lambda i: (0, i)),
            pl.BlockSpec((tm, SGU_W), row),
            pl.BlockSpec((tm, SGU_W), row)],
        out_shape=outs,
        scratch_shapes=[pltpu.VMEM((d, 2 * SGU_W), BF16)],
        name="inproj",
    )(x2, mod_l, g_pre, w_in_l, g_v)


def _cum_log_gates(z):
    ls = jnp.minimum(z, 0.0) - jnp.log1p(jnp.exp(-jnp.abs(z)))
    seq = ls.shape[-1]
    pos = lax.broadcasted_iota(jnp.int32, ls.shape, 1)
    shift = 1
    while shift < seq:
        ls = ls + jnp.where(pos >= shift, pltpu.roll(ls, shift, 1), 0.0)
        shift *= 2
    return ls * LOG2E


def _attn_kernel(q_ref, k_ref, v_ref, zf_ref, bf_ref, *rest):
    n_ride = (len(rest) - 1) // 2
    o_ref = rest[n_ride]
    _cast_riders(rest[:n_ride], rest[n_ride + 1:])
    tq = ATT_TQ
    seq = q_ref.shape[0]
    nq = seq // tq
    z2 = zf_ref[0] + bf_ref[0]
    l_keys = _cum_log_gates(jnp.concatenate([z2, jnp.zeros((8 - 2, seq), F32)], axis=0))
    l_rows = jnp.concatenate([l_keys, jnp.zeros((LANES - 8, seq), F32)], axis=0)
    l_cols = l_rows.T
    lane = lax.broadcasted_iota(jnp.int32, (tq, LANES), 1)
    left = lane < HEAD_DIM
    rows = lax.broadcasted_iota(jnp.int32, (tq, tq), 0)
    cols = lax.broadcasted_iota(jnp.int32, (tq, tq), 1)
    causal = cols <= rows
    zero = jnp.zeros((tq, LANES), BF16)
    one = jnp.ones((tq, LANES), BF16)
    def scores(qi):
        qs = slice(qi * tq, (qi + 1) * tq)
        q2 = q_ref[qs, :]
        qq = jnp.concatenate([jnp.where(left, q2, zero), jnp.where(left, zero, q2)], axis=0)
        t_blocks = [[], []]
        mt = [None, None]
        for j in range(qi + 1):
            ks = slice(j * tq, (j + 1) * tq)
            t2 = lax.dot_general(qq, k_ref[ks, :], (((1,), (1,)), ((), ())),
                                 preferred_element_type=F32)
            for hh in range(2):
                t = t2[hh * tq:(hh + 1) * tq] - l_keys[hh:hh + 1, ks]
                if j == qi:
                    t = jnp.where(causal, t, -jnp.inf)
                t_blocks[hh].append(t)
                mj = jnp.maximum(t[:, :LANES], t[:, LANES:])
                mt[hh] = mj if mt[hh] is None else jnp.maximum(mt[hh], mj)
        shift = []
        for hh in range(2):
            lq = l_cols[qs, hh:hh + 1]
            row_max = jnp.max(mt[hh], axis=-1, keepdims=True) + lq
            shift.append(row_max - lq)
        return t_blocks, shift

    def values(qi, t_blocks, shift):
        qs = slice(qi * tq, (qi + 1) * tq)
        accs = [None, None]
        for j in range(qi + 1):
            ks = slice(j * tq, (j + 1) * tq)
            pp = jnp.concatenate([jnp.exp2(t_blocks[hh][j] - shift[hh]) for hh in range(2)],
                                 axis=0).astype(BF16)
            vaug = jnp.concatenate([v_ref[ks, :], one], axis=1)
            part = _dot(pp, vaug)
            accs[j % 2] = part if accs[j % 2] is None else accs[j % 2] + part
        acc = accs[0] if accs[1] is None else accs[0] + accs[1]
        out0 = acc[:tq, :LANES] / acc[:tq, LANES:]
        out1 = acc[tq:, :LANES] / acc[tq:, LANES:]
        o_ref[qs, :] = jnp.where(left, out0, out1).astype(o_ref.dtype)

    order = list(range(nq - 1, -1, -1))
    pending = scores(order[0])
    for pos, qi in enumerate(order):
        nxt = scores(order[pos + 1]) if pos + 1 < nq else None
        values(qi, *pending)
        pending = nxt


def _attention(q, k, v, zf_pairs, bf_pairs, bsz, seq, riders=()):
    n = q.shape[0]
    npair = N_HEADS // 2
    blk = pl.BlockSpec((seq, LANES), lambda b, p: (b, p))
    r_in, r_out, r_shape = _rider_specs(riders, bsz * npair, lambda b, p: b * npair + p)
    return pl.pallas_call(
        _attn_kernel,
        grid=(bsz, npair),
        in_specs=[blk, blk, blk, pl.BlockSpec((1, 2, seq), lambda b, p: (p, 0, b)),
                  pl.BlockSpec((1, 2, 1), lambda b, p: (p, 0, 0))] + r_in,
        out_specs=[blk] + r_out,
        out_shape=[jax.ShapeDtypeStruct((n, ATTN_W), BF16)] + r_shape,
        name="fox_attention",
    )(q, k, v, zf_pairs, bf_pairs, *[w for w, _ in riders])


def _sgu_kernel(u_ref, sv_ref, w_ref, bt_ref, g_ref, o_ref):
    tm = u_ref.shape[0]
    npair = N_GROUPS // 2
    lane = lax.broadcasted_iota(jnp.int32, (CHUNK, LANES), 1)
    left = lane < HEAD_DIM
    r = lax.broadcasted_iota(jnp.int32, (CHUNK, CHUNK), 0)
    c = lax.broadcasted_iota(jnp.int32, (CHUNK, CHUNK), 1)
    causal = c <= r
    lhs, bias = [], []
    for j in range(npair):
        wa = jnp.where(causal, w_ref[2 * j], jnp.zeros((), BF16))
        wb = jnp.where(causal, w_ref[2 * j + 1], jnp.zeros((), BF16))
        lhs.append(jnp.concatenate([wa, wb], axis=1))
        bias.append(jnp.where(left, bt_ref[:, 2 * j:2 * j + 1], bt_ref[:, 2 * j + 1:2 * j + 2]))
    for ci in range(tm // CHUNK):
        rs = slice(ci * CHUNK, (ci + 1) * CHUNK)
        blks = []
        ssq = jnp.zeros((CHUNK, 1), F32)
        for j in range(npair):
            cs = slice(j * LANES, (j + 1) * LANES)
            svb = sv_ref[rs, cs]
            zero = jnp.zeros_like(svb)
            rhs = jnp.concatenate([jnp.where(left, svb, zero), jnp.where(left, zero, svb)], axis=0)
            mixed = _dot(lhs[j], rhs) + bias[j]
            ob = u_ref[rs, cs].astype(F32) * mixed
            ssq = ssq + jnp.sum(ob * ob, axis=-1, keepdims=True)
            blks.append(ob)
        inv = lax.rsqrt(ssq * (1.0 / SGU_W) + EPS)
        for j in range(npair):
            cs = slice(j * LANES, (j + 1) * LANES)
            o_ref[rs, cs] = (blks[j] * inv * g_ref[:, cs]).astype(o_ref.dtype)


def _sgu(u, sv, w_sp, b_sp_t, g_out):
    n = u.shape[0]
    tm = SGU_TILE
    row = lambda i: (i, 0)
    return pl.pallas_call(
        _sgu_kernel,
        grid=(n // tm,),
        in_specs=[
            pl.BlockSpec((tm, SGU_W), row),
            pl.BlockSpec((tm, SGU_W), row),
            pl.BlockSpec(w_sp.shape, lambda i: (0, 0, 0)),
            pl.BlockSpec(b_sp_t.shape, lambda i: (0, 0)),
            pl.BlockSpec((1, SGU_W), lambda i: (0, 0)),
        ],
        out_specs=pl.BlockSpec((tm, SGU_W), row),
        out_shape=jax.ShapeDtypeStruct((n, SGU_W), BF16),
        name="sgu",
    )(u, sv, w_sp, b_sp_t, g_out)


def _outproj_kernel(a_ref, s_ref, ga_ref, wo_ref, x_ref, mod_ref, gpost_ref, gffn_ref,
                    *rest, with_router):
    if with_router:
        wr_ref, xo_ref, h_ref, lg_ref = rest
    else:
        xo_ref, h_ref = rest
    gt_m = mod_ref[0, 2:3, :]
    sh_f = mod_ref[0, 3:4, :]
    sc_f = mod_ref[0, 4:5, :]
    if with_router:
        wr = wr_ref[...]
        w_hi = wr.astype(BF16)
        w_lo = (wr - w_hi.astype(F32)).astype(BF16)
    for rs in _row_groups(x_ref.shape[0]):
        a = _rms(a_ref[rs, :].astype(F32), ga_ref[...]).astype(BF16)
        y = _dot(a, wo_ref[:ATTN_W, :]) + _dot(s_ref[rs, :], wo_ref[ATTN_W:, :])
        xn = x_ref[rs, :] + (1.0 + gt_m) * _rms(y, gpost_ref[...])
        xo_ref[rs, :] = xn
        h = _rms(xn, gffn_ref[...]) * (1.0 + sc_f) + sh_f
        h_hi = h.astype(BF16)
        h_ref[rs, :] = h_hi
        if with_router:
            h_lo = (h - h_hi.astype(F32)).astype(BF16)
            lg_ref[rs, :] = _dot(h_hi, w_hi) + (_dot(h_hi, w_lo) + _dot(h_lo, w_hi))


def _outproj(attn, sgun, g_attn, w_o, x2, mod_l, g_post, g_ffn, w_router, seq):
    n, d = x2.shape
    tm = PROJ_TILE
    per_b = seq // tm
    row = lambda i: (i, 0)
    const = lambda i: (0, 0)
    with_router = w_router is not None
    in_specs = [
        pl.BlockSpec((tm, ATTN_W), row),
        pl.BlockSpec((tm, SGU_W), row),
        pl.BlockSpec((1, ATTN_W), const),
        pl.BlockSpec(w_o.shape, const),
        pl.BlockSpec((tm, d), row),
        pl.BlockSpec((1, N_MOD, d), lambda i: (i // per_b, 0, 0)),
        pl.BlockSpec((1, d), const),
        pl.BlockSpec((1, d), const),
    ]
    args = [attn, sgun, g_attn, w_o, x2, mod_l, g_post, g_ffn]
    out_specs = [pl.BlockSpec((tm, d), row), pl.BlockSpec((tm, d), row)]
    out_shape = [jax.ShapeDtypeStruct((n, d), F32), jax.ShapeDtypeStruct((n, d), BF16)]
    if with_router:
        in_specs.append(pl.BlockSpec(w_router.shape, const))
        args.append(w_router)
        out_specs.append(pl.BlockSpec((tm, LANES), row))
        out_shape.append(jax.ShapeDtypeStruct((n, LANES), F32))
    return pl.pallas_call(
        functools.partial(_outproj_kernel, with_router=with_router),
        grid=(n // tm,),
        in_specs=in_specs,
        out_specs=out_specs,
        out_shape=out_shape,
        name="outproj_router" if with_router else "outproj",
    )(*args)


def _ffn_kernel(h_ref, wgu_ref, wd_ref, x_ref, mod_ref, gpost_ref, o_ref):
    h = h_ref[...]
    y = None
    for a, b in FFN_CHUNKS:
        g = _dot(h, wgu_ref[:, a:b])
        u = _dot(h, wgu_ref[:, D_FF_DENSE + a:D_FF_DENSE + b])
        act = (g * jax.nn.sigmoid(g) * u).astype(BF16)
        part = _dot(act, wd_ref[a:b, :])
        y = part if y is None else y + part
    gt_f = mod_ref[0, 5:6, :]
    o_ref[...] = x_ref[...] + (1.0 + gt_f) * _rms(y, gpost_ref[...])


def _ffn_dense(h2, w_gu, w_down, x2, mod_l, g_post, seq):
    n, d = x2.shape
    tm = TOK_TILE
    per_b = seq // tm
    resident = dict(pipeline_mode=pl.Buffered(1))
    return pl.pallas_call(
        _ffn_kernel,
        grid=(n // tm,),
        in_specs=[
            pl.BlockSpec((tm, d), lambda i: (i, 0)),
            pl.BlockSpec(w_gu.shape, lambda i: (0, 0), **resident),
            pl.BlockSpec(w_down.shape, lambda i: (0, 0), **resident),
            pl.BlockSpec((tm, d), lambda i: (i, 0)),
            pl.BlockSpec((1, N_MOD, d), lambda i: (i // per_b, 0, 0)),
            pl.BlockSpec((1, d), lambda i: (0, 0)),
        ],
        out_specs=pl.BlockSpec((tm, d), lambda i: (i, 0)),
        out_shape=jax.ShapeDtypeStruct((n, d), F32),
        name="ffn_dense",
    )(h2, w_gu, w_down, x2, mod_l, g_post)


def _route_kernel(lg_ref, keyc_ref, gate_ref, keyt_ref, ends_ref, base_ref):
    s = pl.program_id(0)
    tm = lg_ref.shape[0]

    @pl.when(s == 0)
    def _():
        base_ref[...] = jnp.zeros_like(base_ref)

    lane = lax.broadcasted_iota(jnp.int32, (tm, LANES), 1)
    lg = jnp.where(lane < N_EXPERTS, lg_ref[...], -jnp.inf)
    m1 = jnp.max(lg, axis=-1, keepdims=True)
    i1 = jnp.min(jnp.where(lg == m1, lane, LANES), axis=-1, keepdims=True)
    sel1 = lane == i1
    lg2 = jnp.where(sel1, -jnp.inf, lg)
    m2 = jnp.max(lg2, axis=-1, keepdims=True)
    i2 = jnp.min(jnp.where(lg2 == m2, lane, LANES), axis=-1, keepdims=True)
    sel2 = lane == i2
    e2 = jnp.exp(m2 - m1)
    w1 = 1.0 / (1.0 + e2)
    w2 = e2 / (1.0 + e2)
    gate_ref[...] = (jnp.where(sel1, w1, 0.0) + jnp.where(sel2, w2, 0.0)).T
    chosen = jnp.logical_or(sel1, sel2)
    mask = jnp.where(chosen, 1.0, 0.0)
    r = lax.broadcasted_iota(jnp.int32, (tm, tm), 0)
    c = lax.broadcasted_iota(jnp.int32, (tm, tm), 1)
    tri = jnp.where(c < r, 1.0, 0.0).astype(BF16)
    base = base_ref[0:1, :]
    rank = _dot(tri, mask.astype(BF16)) + base
    key = jnp.where(chosen, rank, UNROUTED)
    keyc_ref[...] = key
    keyt_ref[...] = key.T
    new_base = base + jnp.sum(mask, axis=0, keepdims=True)
    base_ref[...] = jnp.broadcast_to(new_base, base_ref.shape)
    ends_ref[0] = jnp.broadcast_to(new_base, ends_ref.shape[1:])


def _route(logits):
    n = logits.shape[0]
    tm = TOK_TILE
    ns = n // tm
    row = lambda s: (s, 0)
    return pl.pallas_call(
        _route_kernel,
        grid=(ns,),
        in_specs=[pl.BlockSpec((tm, LANES), row)],
        out_specs=[
            pl.BlockSpec((tm, LANES), row),
            pl.BlockSpec((LANES, tm), lambda s: (0, s)),
            pl.BlockSpec((LANES, tm), lambda s: (0, s)),
            pl.BlockSpec((1, 8, LANES), lambda s: (s, 0, 0)),
        ],
        out_shape=[
            jax.ShapeDtypeStruct((n, LANES), F32),
            jax.ShapeDtypeStruct((LANES, n), F32),
            jax.ShapeDtypeStruct((LANES, n), F32),
            jax.ShapeDtypeStruct((ns, 8, LANES), F32),
        ],
        scratch_shapes=[pltpu.VMEM((8, LANES), F32)],
        name="moe_route",
    )(logits)


def _work_lists(ends, n_tokens):
    e_n = N_EXPERTS
    sub = EXP_TM // DISP_TD
    max_tiles = (2 * n_tokens) // EXP_TM + e_n
    i32 = jnp.int32

    ends_i = ends[:, 0, :e_n].astype(i32)
    base = jnp.concatenate([jnp.zeros((1, e_n), i32), ends_i], axis=0)
    cnt = base[-1]
    ntile = (cnt + EXP_TM - 1) // EXP_TM
    tile_end = jnp.cumsum(ntile)
    tile_start = tile_end - ntile
    total_tiles = tile_end[-1]
    nunit = (cnt + EXP_UNIT - 1) // EXP_UNIT
    off = tile_start * EXP_TM

    e_ids = jnp.arange(e_n, dtype=i32)

    def expert_of(tile):
        e = jnp.sum((tile_end[None, :] <= tile[:, None]).astype(i32), axis=1)
        return e, (e[:, None] == e_ids[None, :]).astype(i32)

    def pick(onehot, per_expert):
        return jnp.sum(onehot * per_expert[None, :], axis=1)

    t = jnp.arange(max_tiles, dtype=i32)
    tc = jnp.minimum(t, total_tiles - 1)
    tile_e, oh_t = expert_of(tc)
    tile_pass = jnp.clip(pick(oh_t, nunit) - (tc - pick(oh_t, tile_start)) * (EXP_TM // EXP_UNIT),
                         1, EXP_TM // EXP_UNIT)
    tile_pass = jnp.where(t < total_tiles, tile_pass, 0)

    j = jnp.arange(max_tiles * sub, dtype=i32)
    jt = jnp.minimum(j // sub, total_tiles - 1)
    je, oh_j = expert_of(jt)
    jm = (jt - pick(oh_j, tile_start)) * sub + j % sub
    lo_rank = jm * DISP_TD
    hi_rank = jnp.minimum(lo_rank + DISP_TD, pick(oh_j, cnt))
    base_e = jnp.sum(oh_j[:, :, None] * base.T[None, :, :], axis=1)
    s_lo = jnp.sum((base_e[:, 1:] <= lo_rank[:, None]).astype(i32), axis=1)
    s_hi = jnp.sum((base_e[:, :-1] < hi_rank[:, None]).astype(i32), axis=1)
    n_src = jnp.where(j // sub < total_tiles, jnp.maximum(s_hi - s_lo, 0), 0)

    rows = max_tiles * EXP_TM
    start = off[None, :] + base[:-1]
    stop = off[None, :] + base[1:]
    wstart = jnp.minimum(start // COMB_ALIGN * COMB_ALIGN, rows - COMB_WIN)
    need = jnp.where(stop > start, stop - wstart, 0)
    rel = wstart - off[None, :]
    split = COMB_CHUNKS[0][1]
    wtail = jnp.where(need > split, (wstart + split) // COMB_ALIGN, 0)
    return dict(
        disp=(je, jm, s_lo, n_src),
        comb=((wstart // COMB_ALIGN).reshape(-1), wtail.reshape(-1), rel.reshape(-1),
              need.reshape(-1)),
        tiles=(tc, tile_e, tile_pass),
        max_tiles=max_tiles)


def _dispatch_kernel(e_ref, m_ref, slo_ref, n_ref, h_ref, keyt_ref, gatet_ref, o_ref, gs_ref,
                     acc_ref, gacc_ref):
    td = DISP_TD
    for sub in range(DISP_PER_STEP):
        _dispatch_tile(pl.program_id(0) * DISP_PER_STEP + sub, slice(sub * td, (sub + 1) * td),
                       e_ref, m_ref, slo_ref, n_ref, h_ref, keyt_ref, gatet_ref, o_ref, gs_ref,
                       acc_ref, gacc_ref)


def _dispatch_tile(j, rs, e_ref, m_ref, slo_ref, n_ref, h_ref, keyt_ref, gatet_ref, o_ref, gs_ref,
                   acc_ref, gacc_ref):
    td = DISP_TD
    e = e_ref[j]
    slot = (m_ref[j] * td + lax.broadcasted_iota(jnp.int32, (td, TOK_TILE), 0)).astype(F32)

    def item(i, assign):
        ts = pl.multiple_of((slo_ref[j] + i) * TOK_TILE, TOK_TILE)
        key = keyt_ref[pl.ds(e, 1), pl.ds(ts, TOK_TILE)]
        hit = key == slot
        onehot = jnp.where(hit, 1.0, 0.0).astype(BF16)
        rows = _dot(onehot, h_ref[pl.ds(ts, TOK_TILE), :])
        gate = gatet_ref[pl.ds(e, 1), pl.ds(ts, TOK_TILE)]
        gsum = jnp.sum(jnp.where(hit, gate, 0.0), axis=-1, keepdims=True)
        if assign:
            acc_ref[...] = rows
            gacc_ref[...] = gsum
        else:
            acc_ref[...] += rows
            gacc_ref[...] += gsum

    @pl.when(n_ref[j] == 0)
    def _():
        acc_ref[...] = jnp.zeros_like(acc_ref)
        gacc_ref[...] = jnp.zeros_like(gacc_ref)

    @pl.when(n_ref[j] > 0)
    def _():
        item(0, True)

    def body(i, carry):
        item(i, False)
        return carry

    lax.fori_loop(1, n_ref[j], body, 0)
    o_ref[rs, :] = acc_ref[...].astype(o_ref.dtype)
    gs_ref[rs, :] = gacc_ref[...]


def _dispatch(h2, key_t, gate_t, lists, max_tiles):
    n, d = h2.shape
    rows = max_tiles * EXP_TM
    resident = dict(pipeline_mode=pl.Buffered(1))
    grid_spec = pltpu.PrefetchScalarGridSpec(
        num_scalar_prefetch=4,
        grid=(rows // (DISP_TD * DISP_PER_STEP),),
        in_specs=[
            pl.BlockSpec((n, d), lambda j, *_: (0, 0), **resident),
            pl.BlockSpec((8, n), lambda j, *_: (0, 0), **resident),
            pl.BlockSpec((8, n), lambda j, *_: (0, 0), **resident),
        ],
        out_specs=[pl.BlockSpec((DISP_TD * DISP_PER_STEP, d), lambda j, *_: (j, 0)),
                   pl.BlockSpec((DISP_TD * DISP_PER_STEP, 1), lambda j, *_: (j, 0))],
        scratch_shapes=[pltpu.VMEM((DISP_TD, d), F32), pltpu.VMEM((DISP_TD, 1), F32)],
    )
    return pl.pallas_call(
        _dispatch_kernel,
        grid_spec=grid_spec,
        out_shape=[jax.ShapeDtypeStruct((rows, d), BF16),
                   jax.ShapeDtypeStruct((rows, 1), F32)],
        name="moe_dispatch",
    )(*lists, h2, key_t, gate_t)


def _expert_kernel(tidx_ref, te_ref, tp_ref, x_ref, gs_ref, wg_ref, wu_ref, wd_ref, y_ref,
                   acc_ref):
    i = pl.program_id(0)
    f = pl.program_id(1)
    nf = pl.num_programs(1)

    max_units = EXP_TM // EXP_UNIT
    n_steps = D_FF_EXPERT // EXP_TF

    def run(units, first, last):
        done = 0
        while done < units * EXP_UNIT:
            size = EXP_SUB if units * EXP_UNIT - done >= EXP_SUB else EXP_UNIT
            rs = slice(done, done + size)
            done += size
            x = x_ref[rs, :]
            g = _dot(x, wg_ref[0])
            u = _dot(x, wu_ref[0])
            act = (g * jax.nn.sigmoid(g) * u).astype(BF16)
            part = _dot(act, wd_ref[0])
            total = part if first else acc_ref[rs, :] + part
            if last:
                y_ref[rs, :] = (gs_ref[rs, :] * total).astype(y_ref.dtype)
            else:
                acc_ref[rs, :] = total
        if last and done < EXP_TM:
            y_ref[done:, :] = jnp.zeros((EXP_TM - done, y_ref.shape[1]), y_ref.dtype)

    kinds = [(f == 0, True, n_steps == 1), (f == nf - 1, n_steps == 1, True)]
    if n_steps > 2:
        kinds.append((jnp.logical_and(f > 0, f < nf - 1), False, False))
    for units in range(max_units + 1):
        for cond, first, last in kinds[:1] if n_steps == 1 else kinds:
            @pl.when(jnp.logical_and(tp_ref[i] == units, cond))
            def _(units=units, first=first, last=last):
                run(units, first, last)


def _experts(xs, gs, w_gu, w_down, tiles, max_tiles):
    rows, d = xs.shape
    tm, tf = EXP_TM, EXP_TF
    nf = D_FF_EXPERT // tf

    def fsel(i, f, tp):
        return jnp.where(tp[i] > 0, f, nf - 1)

    grid_spec = pltpu.PrefetchScalarGridSpec(
        num_scalar_prefetch=3,
        grid=(max_tiles, nf),
        in_specs=[
            pl.BlockSpec((tm, d), lambda i, f, ti, te, tv: (ti[i], 0)),
            pl.BlockSpec((tm, 1), lambda i, f, ti, te, tv: (ti[i], 0)),
            pl.BlockSpec((1, d, tf), lambda i, f, ti, te, tv: (te[i], 0, fsel(i, f, tv))),
            pl.BlockSpec((1, d, tf), lambda i, f, ti, te, tv: (te[i], 0, nf + fsel(i, f, tv))),
            pl.BlockSpec((1, tf, d), lambda i, f, ti, te, tv: (te[i], fsel(i, f, tv), 0)),
        ],
        out_specs=pl.BlockSpec((tm, d), lambda i, f, ti, te, tv: (i, 0)),
        scratch_shapes=[pltpu.VMEM((tm, d), F32)],
    )
    return pl.pallas_call(
        _expert_kernel,
        grid_spec=grid_spec,
        out_shape=jax.ShapeDtypeStruct((rows, d), BF16),
        name="moe_experts",
    )(*tiles, xs, gs, w_gu, w_gu, w_down)


def _combine_kernel(ws_ref, wt_ref, rel_ref, need_ref, *refs):
    head_refs = refs[:N_EXPERTS]
    tail_refs = refs[N_EXPERTS:2 * N_EXPERTS]
    keyc_ref, x_ref, mod_ref, gpost_ref, o_ref, acc_ref = refs[2 * N_EXPERTS:]
    s = pl.program_id(0)
    ts = acc_ref.shape[0]

    def term(e, y_ref, a):
        key = keyc_ref[:, e:e + 1]
        rel = rel_ref[s * N_EXPERTS + e]
        slot = (rel + a + lax.broadcasted_iota(jnp.int32, (ts, y_ref.shape[0]), 1)).astype(F32)
        onehot = jnp.where(key == slot, 1.0, 0.0).astype(BF16)
        return _dot(onehot, y_ref[...])

    acc = term(0, head_refs[0], 0)
    for e in range(1, N_EXPERTS):
        acc = acc + term(e, head_refs[e], 0)
    acc_ref[...] = acc
    split = COMB_CHUNKS[0][1]
    for e in range(N_EXPERTS):
        @pl.when(need_ref[s * N_EXPERTS + e] > split)
        def _(e=e):
            acc_ref[...] += term(e, tail_refs[e], split)
    gt_f = mod_ref[0, 5:6, :]
    o_ref[...] = x_ref[...] + (1.0 + gt_f) * _rms(acc_ref[...], gpost_ref[...])


def _combine(y, key_c, x2, mod_l, g_post, lists, seq):
    n, d = x2.shape
    per_b = seq // TOK_TILE
    tok = lambda s, *_: (s, 0)

    (a0, b0), (a1, b1) = COMB_CHUNKS

    def head(e):
        return pl.BlockSpec((pl.Element(b0 - a0), pl.Element(d)),
                            lambda s, ws, wt, rel, need: (ws[s * N_EXPERTS + e] * COMB_ALIGN, 0))

    def tail(e):
        return pl.BlockSpec((pl.Element(b1 - a1), pl.Element(d)),
                            lambda s, ws, wt, rel, need: (wt[s * N_EXPERTS + e] * COMB_ALIGN, 0))

    grid_spec = pltpu.PrefetchScalarGridSpec(
        num_scalar_prefetch=4,
        grid=(n // TOK_TILE,),
        in_specs=[head(e) for e in range(N_EXPERTS)] + [tail(e) for e in range(N_EXPERTS)] + [
            pl.BlockSpec((TOK_TILE, LANES), tok),
            pl.BlockSpec((TOK_TILE, d), tok),
            pl.BlockSpec((1, N_MOD, d), lambda s, *_: (s // per_b, 0, 0)),
            pl.BlockSpec((1, d), lambda s, *_: (0, 0)),
        ],
        out_specs=pl.BlockSpec((TOK_TILE, d), tok),
        scratch_shapes=[pltpu.VMEM((TOK_TILE, d), F32)],
    )
    return pl.pallas_call(
        _combine_kernel,
        grid_spec=grid_spec,
        out_shape=jax.ShapeDtypeStruct((n, d), F32),
        name="moe_combine",
    )(*lists, *([y] * (2 * N_EXPERTS)), key_c, x2, mod_l, g_post)


def kernel(x, c, w_ada, b_ada, g_pre_mix, g_post_mix, g_pre_ffn, g_post_ffn, w_in, b_forget, g_v,
           w_spatial, b_spatial, g_out_attn, g_out_sgu, w_out, w_gate_up_dense, w_down_dense,
           w_router, w_gate_up_exp, w_down_exp):
    bsz, seq, d = x.shape
    depth = w_ada.shape[0]
    n = bsz * seq
    npair = N_HEADS // 2
    assert d == D_MODEL and seq % PROJ_TILE == 0 and seq % TOK_TILE == 0 and seq % ATT_TQ == 0

    w_gu_exp3 = w_gate_up_exp.reshape(-1, N_EXPERTS * d, 2 * D_FF_EXPERT)
    w_down_exp3 = w_down_exp.reshape(-1, N_EXPERTS * D_FF_EXPERT, d)
    riders = {("w_in", 0): (w_in, 0), ("w_out", 0): (w_out, 0)}
    mod, *casts = _modulation(c, w_ada, b_ada, tuple(riders.values()))
    bf16 = dict(zip(riders.keys(), casts))
    mod = mod.reshape(depth, bsz, N_MOD, d)
    x2 = x.reshape(n, d)
    for l in range(depth):
        mod_l = mod[l]
        moe = (l % 2 == 1)
        q, k, v, zf_t, u, sv = _inproj(x2, mod_l, g_pre_mix[l].reshape(1, d), bf16[("w_in", l)],
                                       g_v[l].reshape(1, SGU_W), seq)
        riders = {}
        if l + 1 < depth:
            riders[("w_in", l + 1)] = (w_in, l + 1)
            riders[("w_out", l + 1)] = (w_out, l + 1)
        if moe:
            riders[("w_gu_exp", l // 2)] = (w_gu_exp3, l // 2)
        else:
            riders[("w_gu_dense", l // 2)] = (w_gate_up_dense, l // 2)
            riders[("w_down_dense", l // 2)] = (w_down_dense, l // 2)
            if l + 1 < depth:
                riders[("w_down_exp", l // 2)] = (w_down_exp3, l // 2)
        attn, *casts = _attention(q, k, v, zf_t.reshape(npair, 2, n),
                                  b_forget[l].reshape(npair, 2, 1), bsz, seq,
                                  tuple(riders.values()))
        bf16.update(zip(riders.keys(), casts))
        sgun = _sgu(u, sv, w_spatial[l].astype(BF16), b_spatial[l].T,
                    g_out_sgu[l].reshape(1, SGU_W))
        w_r = None
        if moe:
            w_r = jnp.pad(w_router[l // 2], ((0, 0), (0, LANES - N_EXPERTS)))
        res = _outproj(attn, sgun, g_out_attn[l].reshape(1, ATTN_W), bf16[("w_out", l)],
                       x2, mod_l, g_post_mix[l].reshape(1, d), g_pre_ffn[l].reshape(1, d), w_r, seq)
        g_post = g_post_ffn[l].reshape(1, d)
        if not moe:
            x2, h2 = res
            x2 = _ffn_dense(h2, bf16[("w_gu_dense", l // 2)], bf16[("w_down_dense", l // 2)],
                            x2, mod_l, g_post, seq)
        else:
            x2, h2, logits = res
            key_c, gate_t, key_t, ends = _route(logits)
            wl = _work_lists(ends, n)
            xs, gs = _dispatch(h2, key_t, gate_t, wl["disp"], wl["max_tiles"])
            w_gu = bf16[("w_gu_exp", l // 2)].reshape(N_EXPERTS, d, 2 * D_FF_EXPERT)
            w_dn = bf16[("w_down_exp", l // 2)].reshape(N_EXPERTS, D_FF_EXPERT, d)
            y = _experts(xs, gs, w_gu, w_dn, wl["tiles"], wl["max_tiles"])
            x2 = _combine(y, key_c, x2, mod_l, g_post, wl["comb"], seq)
    return x2.reshape(bsz, seq, d)
```

```python
import functools
import math

import jax
import jax.numpy as jnp
from jax import lax
from jax.experimental import pallas as pl
from jax.experimental.pallas import tpu as pltpu

F32 = jnp.float32
BF16 = jnp.bfloat16

D_MODEL = 1024
HEAD_DIM = 64
N_HEADS = 8
N_GROUPS = 8
ATTN_W = N_HEADS * HEAD_DIM
SGU_W = N_GROUPS * HEAD_DIM
CHUNK = 128
D_FF_DENSE = 2816
N_EXPERTS = 8
D_FF_EXPERT = 3584
N_MOD = 6
EPS = 1e-6

LANES = 128
TOK_TILE = 512
PROJ_TILE = 1024
ROW_GROUP = 256
SGU_TILE = 2048
MOD_BLOCKS = 8
ATT_TQ = 256
ATT_AHEAD = 2
MXU_TILE = 256
FFN_CHUNKS = ((0, 6 * MXU_TILE), (6 * MXU_TILE, D_FF_DENSE))
EXP_TM = 1024
EXP_SUB = 512
EXP_UNIT = 512
EXP_TF = 7 * MXU_TILE
LOG2E = math.log2(math.e)
DISP_TD = 256
DISP_PER_STEP = 4
COMB_ALIGN = 16
COMB_WIN = TOK_TILE + COMB_ALIGN
COMB_CHUNKS = ((0, MXU_TILE), (MXU_TILE, COMB_WIN))
UNROUTED = -float(2 ** 30)
GELU_C = math.sqrt(2.0 / math.pi)


def _rms(x, g):
    return x * lax.rsqrt(jnp.mean(x * x, axis=-1, keepdims=True) + EPS) * g


def _dot(a, b):
    return jnp.dot(a, b, preferred_element_type=F32)


def _row_groups(rows):
    return [slice(r, r + ROW_GROUP) for r in range(0, rows, ROW_GROUP)]


def _rider_specs(riders, steps, step_of):
    ins, outs, shapes = [], [], []
    for w, lead in riders:
        _, rows, cols = w.shape
        share = 1
        while rows % (steps // share) or (rows // (steps // share)) % COMB_ALIGN:
            share *= 2
        rb = rows // (steps // share)
        ins.append(pl.BlockSpec(
            (1, rb, cols), lambda *g, lead=lead, share=share: (lead, step_of(*g) // share, 0)))
        outs.append(pl.BlockSpec((rb, cols), lambda *g, share=share: (step_of(*g) // share, 0)))
        shapes.append(jax.ShapeDtypeStruct((rows, cols), BF16))
    return ins, outs, shapes


def _cast_riders(src_refs, dst_refs):
    for src, dst in zip(src_refs, dst_refs):
        dst[...] = src[0].astype(dst.dtype)


def _mod_kernel(c_ref, w_ref, b_ref, *rest):
    n_ride = (len(rest) - 1) // 2
    o_ref = rest[n_ride]
    _cast_riders(rest[:n_ride], rest[n_ride + 1:])
    c = c_ref[...]
    ca = (c * jax.nn.sigmoid(c)).astype(BF16)
    o_ref[0] = _dot(ca, w_ref[0].astype(BF16)) + b_ref[0]


def _modulation(c, w_ada, b_ada, riders=()):
    depth, d, m = w_ada.shape
    bsz = c.shape[0]
    nblk = MOD_BLOCKS
    cb = m // nblk
    r_in, r_out, r_shape = _rider_specs(riders, depth * nblk, lambda l, j: l * nblk + j)
    return pl.pallas_call(
        _mod_kernel,
        grid=(depth, nblk),
        in_specs=[
            pl.BlockSpec((bsz, d), lambda l, j: (0, 0)),
            pl.BlockSpec((1, d, cb), lambda l, j: (l, 0, j)),
            pl.BlockSpec((1, 1, cb), lambda l, j: (l, 0, j)),
        ] + r_in,
        out_specs=[pl.BlockSpec((1, bsz, cb), lambda l, j: (l, 0, j))] + r_out,
        out_shape=[jax.ShapeDtypeStruct((depth, bsz, m), F32)] + r_shape,
        name="adaln_mod",
    )(c, w_ada, b_ada.reshape(depth, 1, m), *[w for w, _ in riders])


def _inproj_kernel(x_ref, mod_ref, gpre_ref, w_ref, gv_ref,
                   q_ref, k_ref, v_ref, zf_ref, u_ref, sv_ref, wuv_ref):
    o3 = 3 * ATTN_W
    o4 = o3 + N_HEADS

    @pl.when(pl.program_id(0) == 0)
    def _():
        wuv_ref[...] = w_ref[:, o4:]

    sh = mod_ref[0, 0:1, :]
    sc = mod_ref[0, 1:2, :]
    for rs in _row_groups(x_ref.shape[0]):
        h = (_rms(x_ref[rs, :], gpre_ref[...]) * (1.0 + sc) + sh).astype(BF16)
        qkv = _dot(h, w_ref[:, :o3])
        q_ref[rs, :] = (qkv[:, :ATTN_W] * (HEAD_DIM ** -0.5 * LOG2E)).astype(BF16)
        k_ref[rs, :] = qkv[:, ATTN_W:2 * ATTN_W].astype(BF16)
        v_ref[rs, :] = qkv[:, 2 * ATTN_W:].astype(BF16)
        zf_ref[:, rs] = _dot(h, w_ref[:, o3:o3 + LANES]).T[:N_HEADS, :]
        uv = _dot(h, wuv_ref[...])
        uv = uv * (0.5 * (1.0 + jnp.tanh(GELU_C * (uv + 0.044715 * (uv * uv * uv)))))
        u_ref[rs, :] = uv[:, :SGU_W].astype(BF16)
        s = uv[:, SGU_W:]
        sc_ = s - jnp.mean(s, axis=-1, keepdims=True)
        sv = sc_ * lax.rsqrt(jnp.mean(sc_ * sc_, axis=-1, keepdims=True) + EPS) * gv_ref[...]
        sv_ref[rs, :] = sv.astype(BF16)


def _inproj(x2, mod_l, g_pre, w_in_l, g_v, seq):
    n, d = x2.shape
    tm = PROJ_TILE
    per_b = seq // tm
    row = lambda i: (i, 0)
    const = lambda i: (0, 0)
    resident = dict(pipeline_mode=pl.Buffered(1))
    outs = [jax.ShapeDtypeStruct((n, ATTN_W), BF16)] * 3 + [
        jax.ShapeDtypeStruct((N_HEADS, n), F32),
        jax.ShapeDtypeStruct((n, SGU_W), BF16),
        jax.ShapeDtypeStruct((n, SGU_W), BF16)]
    return pl.pallas_call(
        _inproj_kernel,
        grid=(n // tm,),
        in_specs=[
            pl.BlockSpec((tm, d), row),
            pl.BlockSpec((1, N_MOD, d), lambda i: (i // per_b, 0, 0)),
            pl.BlockSpec((1, d), const),
            pl.BlockSpec(w_in_l.shape, const, **resident),
            pl.BlockSpec((1, SGU_W), const),
        ],
        out_specs=[pl.BlockSpec((tm, ATTN_W), row)] * 3 + [
            pl.BlockSpec((N_HEADS, tm), lambda i: (0, i)),
            pl.BlockSpec((tm, SGU_W), row),
            pl.BlockSpec((tm, SGU_W), row)],
        out_shape=outs,
        scratch_shapes=[pltpu.VMEM((d, 2 * SGU_W), BF16)],
        name="inproj",
    )(x2, mod_l, g_pre, w_in_l, g_v)


def _cum_log_gates(z):
    ls = jnp.minimum(z, 0.0) - jnp.log1p(jnp.exp(-jnp.abs(z)))
    seq = ls.shape[-1]
    pos = lax.broadcasted_iota(jnp.int32, ls.shape, 1)
    shift = 1
    while shift < seq:
        ls = ls + jnp.where(pos >= shift, pltpu.roll(ls, shift, 1), 0.0)
        shift *= 2
    return ls * LOG2E


def _attn_kernel(q_ref, k_ref, v_ref, zf_ref, bf_ref, *rest):
    n_ride = (len(rest) - 1) // 2
    o_ref = rest[n_ride]
    _cast_riders(rest[:n_ride], rest[n_ride + 1:])
    tq = ATT_TQ
    seq = q_ref.shape[0]
    nq = seq // tq
    z2 = zf_ref[0] + bf_ref[0]
    l_keys = _cum_log_gates(jnp.concatenate([z2, jnp.zeros((8 - 2, seq), F32)], axis=0))
    l_rows = jnp.concatenate([l_keys, jnp.zeros((LANES - 8, seq), F32)], axis=0)
    l_cols = l_rows.T
    lane = lax.broadcasted_iota(jnp.int32, (tq, LANES), 1)
    left = lane < HEAD_DIM
    rows = lax.broadcasted_iota(jnp.int32, (tq, tq), 0)
    cols = lax.broadcasted_iota(jnp.int32, (tq, tq), 1)
    causal = cols <= rows
    zero = jnp.zeros((tq, LANES), BF16)
    one = jnp.ones((tq, LANES), BF16)
    def scores(qi):
        qs = slice(qi * tq, (qi + 1) * tq)
        q2 = q_ref[qs, :]
        qq = jnp.concatenate([jnp.where(left, q2, zero), jnp.where(left, zero, q2)], axis=0)
        t_blocks = [[], []]
        mt = [None, None]
        for j in range(qi + 1):
            ks = slice(j * tq, (j + 1) * tq)
            t2 = lax.dot_general(qq, k_ref[ks, :], (((1,), (1,)), ((), ())),
                                 preferred_element_type=F32)
            for hh in range(2):
                t = t2[hh * tq:(hh + 1) * tq] - l_keys[hh:hh + 1, ks]
                if j == qi:
                    t = jnp.where(causal, t, -jnp.inf)
                t_blocks[hh].append(t)
                mj = jnp.maximum(t[:, :LANES], t[:, LANES:])
                mt[hh] = mj if mt[hh] is None else jnp.maximum(mt[hh], mj)
        shift = []
        for hh in range(2):
            lq = l_cols[qs, hh:hh + 1]
            row_max = jnp.max(mt[hh], axis=-1, keepdims=True) + lq
            shift.append(row_max - lq)
        return t_blocks, shift

    def values(qi, t_blocks, shift):
        qs = slice(qi * tq, (qi + 1) * tq)
        accs = [None, None]
        for j in range(qi + 1):
            ks = slice(j * tq, (j + 1) * tq)
            pp = jnp.concatenate([jnp.exp2(t_blocks[hh][j] - shift[hh]) for hh in range(2)],
                                 axis=0).astype(BF16)
            vaug = jnp.concatenate([v_ref[ks, :], one], axis=1)
            part = _dot(pp, vaug)
            accs[j % 2] = part if accs[j % 2] is None else accs[j % 2] + part
        acc = accs[0] if accs[1] is None else accs[0] + accs[1]
        out0 = acc[:tq, :LANES] / acc[:tq, LANES:]
        out1 = acc[tq:, :LANES] / acc[tq:, LANES:]
        o_ref[qs, :] = jnp.where(left, out0, out1).astype(o_ref.dtype)

    order = list(range(nq - 1, -1, -1))
    pending = [scores(qi) for qi in order[:ATT_AHEAD]]
    for pos, qi in enumerate(order):
        if pos + ATT_AHEAD < nq:
            pending.append(scores(order[pos + ATT_AHEAD]))
        values(qi, *pending.pop(0))


def _attention(q, k, v, zf_pairs, bf_pairs, bsz, seq, riders=()):
    n = q.shape[0]
    npair = N_HEADS // 2
    blk = pl.BlockSpec((seq, LANES), lambda b, p: (b, p))
    r_in, r_out, r_shape = _rider_specs(riders, bsz * npair, lambda b, p: b * npair + p)
    return pl.pallas_call(
        _attn_kernel,
        grid=(bsz, npair),
        in_specs=[blk, blk, blk, pl.BlockSpec((1, 2, seq), lambda b, p: (p, 0, b)),
                  pl.BlockSpec((1, 2, 1), lambda b, p: (p, 0, 0))] + r_in,
        out_specs=[blk] + r_out,
        out_shape=[jax.ShapeDtypeStruct((n, ATTN_W), BF16)] + r_shape,
        name="fox_attention",
    )(q, k, v, zf_pairs, bf_pairs, *[w for w, _ in riders])


def _sgu_kernel(u_ref, sv_ref, w_ref, bt_ref, g_ref, o_ref):
    tm = u_ref.shape[0]
    npair = N_GROUPS // 2
    lane = lax.broadcasted_iota(jnp.int32, (CHUNK, LANES), 1)
    left = lane < HEAD_DIM
    r = lax.broadcasted_iota(jnp.int32, (CHUNK, CHUNK), 0)
    c = lax.broadcasted_iota(jnp.int32, (CHUNK, CHUNK), 1)
    causal = c <= r
    lhs, bias = [], []
    for j in range(npair):
        wa = jnp.where(causal, w_ref[2 * j], jnp.zeros((), BF16))
        wb = jnp.where(causal, w_ref[2 * j + 1], jnp.zeros((), BF16))
        lhs.append(jnp.concatenate([wa, wb], axis=1))
        bias.append(jnp.where(left, bt_ref[:, 2 * j:2 * j + 1], bt_ref[:, 2 * j + 1:2 * j + 2]))
    for ci in range(tm // CHUNK):
        rs = slice(ci * CHUNK, (ci + 1) * CHUNK)
        blks = []
        ssq = jnp.zeros((CHUNK, 1), F32)
        for j in range(npair):
            cs = slice(j * LANES, (j + 1) * LANES)
            svb = sv_ref[rs, cs]
            zero = jnp.zeros_like(svb)
            rhs = jnp.concatenate([jnp.where(left, svb, zero), jnp.where(left, zero, svb)], axis=0)
            mixed = _dot(lhs[j], rhs) + bias[j]
            ob = u_ref[rs, cs].astype(F32) * mixed
            ssq = ssq + jnp.sum(ob * ob, axis=-1, keepdims=True)
            blks.append(ob)
        inv = lax.rsqrt(ssq * (1.0 / SGU_W) + EPS)
        for j in range(npair):
            cs = slice(j * LANES, (j + 1) * LANES)
            o_ref[rs, cs] = (blks[j] * inv * g_ref[:, cs]).astype(o_ref.dtype)


def _sgu(u, sv, w_sp, b_sp_t, g_out):
    n = u.shape[0]
    tm = SGU_TILE
    row = lambda i: (i, 0)
    return pl.pallas_call(
        _sgu_kernel,
        grid=(n // tm,),
        in_specs=[
            pl.BlockSpec((tm, SGU_W), row),
            pl.BlockSpec((tm, SGU_W), row),
            pl.BlockSpec(w_sp.shape, lambda i: (0, 0, 0)),
            pl.BlockSpec(b_sp_t.shape, lambda i: (0, 0)),
            pl.BlockSpec((1, SGU_W), lambda i: (0, 0)),
        ],
        out_specs=pl.BlockSpec((tm, SGU_W), row),
        out_shape=jax.ShapeDtypeStruct((n, SGU_W), BF16),
        name="sgu",
    )(u, sv, w_sp, b_sp_t, g_out)


def _outproj_kernel(a_ref, s_ref, ga_ref, wo_ref, x_ref, mod_ref, gpost_ref, gffn_ref,
                    *rest, with_router):
    if with_router:
        wr_ref, xo_ref, h_ref, lg_ref = rest
    else:
        xo_ref, h_ref = rest
    gt_m = mod_ref[0, 2:3, :]
    sh_f = mod_ref[0, 3:4, :]
    sc_f = mod_ref[0, 4:5, :]
    if with_router:
        wr = wr_ref[...]
        w_hi = wr.astype(BF16)
        w_lo = (wr - w_hi.astype(F32)).astype(BF16)
    for rs in _row_groups(x_ref.shape[0]):
        a = _rms(a_ref[rs, :].astype(F32), ga_ref[...]).astype(BF16)
        y = _dot(a, wo_ref[:ATTN_W, :]) + _dot(s_ref[rs, :], wo_ref[ATTN_W:, :])
        xn = x_ref[rs, :] + (1.0 + gt_m) * _rms(y, gpost_ref[...])
        xo_ref[rs, :] = xn
        h = _rms(xn, gffn_ref[...]) * (1.0 + sc_f) + sh_f
        h_hi = h.astype(BF16)
        h_ref[rs, :] = h_hi
        if with_router:
            h_lo = (h - h_hi.astype(F32)).astype(BF16)
            lg_ref[rs, :] = _dot(h_hi, w_hi) + (_dot(h_hi, w_lo) + _dot(h_lo, w_hi))


def _outproj(attn, sgun, g_attn, w_o, x2, mod_l, g_post, g_ffn, w_router, seq):
    n, d = x2.shape
    tm = PROJ_TILE
    per_b = seq // tm
    row = lambda i: (i, 0)
    const = lambda i: (0, 0)
    with_router = w_router is not None
    in_specs = [
        pl.BlockSpec((tm, ATTN_W), row),
        pl.BlockSpec((tm, SGU_W), row),
        pl.BlockSpec((1, ATTN_W), const),
        pl.BlockSpec(w_o.shape, const),
        pl.BlockSpec((tm, d), row),
        pl.BlockSpec((1, N_MOD, d), lambda i: (i // per_b, 0, 0)),
        pl.BlockSpec((1, d), const),
        pl.BlockSpec((1, d), const),
    ]
    args = [attn, sgun, g_attn, w_o, x2, mod_l, g_post, g_ffn]
    out_specs = [pl.BlockSpec((tm, d), row), pl.BlockSpec((tm, d), row)]
    out_shape = [jax.ShapeDtypeStruct((n, d), F32), jax.ShapeDtypeStruct((n, d), BF16)]
    if with_router:
        in_specs.append(pl.BlockSpec(w_router.shape, const))
        args.append(w_router)
        out_specs.append(pl.BlockSpec((tm, LANES), row))
        out_shape.append(jax.ShapeDtypeStruct((n, LANES), F32))
    return pl.pallas_call(
        functools.partial(_outproj_kernel, with_router=with_router),
        grid=(n // tm,),
        in_specs=in_specs,
        out_specs=out_specs,
        out_shape=out_shape,
        name="outproj_router" if with_router else "outproj",
    )(*args)


def _ffn_kernel(h_ref, wgu_ref, wd_ref, x_ref, mod_ref, gpost_ref, o_ref):
    h = h_ref[...]
    y = None
    for a, b in FFN_CHUNKS:
        g = _dot(h, wgu_ref[:, a:b])
        u = _dot(h, wgu_ref[:, D_FF_DENSE + a:D_FF_DENSE + b])
        act = (g * jax.nn.sigmoid(g) * u).astype(BF16)
        part = _dot(act, wd_ref[a:b, :])
        y = part if y is None else y + part
    gt_f = mod_ref[0, 5:6, :]
    o_ref[...] = x_ref[...] + (1.0 + gt_f) * _rms(y, gpost_ref[...])


def _ffn_dense(h2, w_gu, w_down, x2, mod_l, g_post, seq):
    n, d = x2.shape
    tm = TOK_TILE
    per_b = seq // tm
    resident = dict(pipeline_mode=pl.Buffered(1))
    return pl.pallas_call(
        _ffn_kernel,
        grid=(n // tm,),
        in_specs=[
            pl.BlockSpec((tm, d), lambda i: (i, 0)),
            pl.BlockSpec(w_gu.shape, lambda i: (0, 0), **resident),
            pl.BlockSpec(w_down.shape, lambda i: (0, 0), **resident),
            pl.BlockSpec((tm, d), lambda i: (i, 0)),
            pl.BlockSpec((1, N_MOD, d), lambda i: (i // per_b, 0, 0)),
            pl.BlockSpec((1, d), lambda i: (0, 0)),
        ],
        out_specs=pl.BlockSpec((tm, d), lambda i: (i, 0)),
        out_shape=jax.ShapeDtypeStruct((n, d), F32),
        name="ffn_dense",
    )(h2, w_gu, w_down, x2, mod_l, g_post)


def _route_kernel(lg_ref, keyc_ref, gate_ref, keyt_ref, ends_ref, base_ref):
    s = pl.program_id(0)
    tm = lg_ref.shape[0]

    @pl.when(s == 0)
    def _():
        base_ref[...] = jnp.zeros_like(base_ref)

    lane = lax.broadcasted_iota(jnp.int32, (tm, LANES), 1)
    lg = jnp.where(lane < N_EXPERTS, lg_ref[...], -jnp.inf)
    m1 = jnp.max(lg, axis=-1, keepdims=True)
    i1 = jnp.min(jnp.where(lg == m1, lane, LANES), axis=-1, keepdims=True)
    sel1 = lane == i1
    lg2 = jnp.where(sel1, -jnp.inf, lg)
    m2 = jnp.max(lg2, axis=-1, keepdims=True)
    i2 = jnp.min(jnp.where(lg2 == m2, lane, LANES), axis=-1, keepdims=True)
    sel2 = lane == i2
    e2 = jnp.exp(m2 - m1)
    w1 = 1.0 / (1.0 + e2)
    w2 = e2 / (1.0 + e2)
    gate_ref[...] = (jnp.where(sel1, w1, 0.0) + jnp.where(sel2, w2, 0.0)).T
    chosen = jnp.logical_or(sel1, sel2)
    mask = jnp.where(chosen, 1.0, 0.0)
    r = lax.broadcasted_iota(jnp.int32, (tm, tm), 0)
    c = lax.broadcasted_iota(jnp.int32, (tm, tm), 1)
    tri = jnp.where(c < r, 1.0, 0.0).astype(BF16)
    base = base_ref[0:1, :]
    rank = _dot(tri, mask.astype(BF16)) + base
    key = jnp.where(chosen, rank, UNROUTED)
    keyc_ref[...] = key
    keyt_ref[...] = key.T
    new_base = base + jnp.sum(mask, axis=0, keepdims=True)
    base_ref[...] = jnp.broadcast_to(new_base, base_ref.shape)
    ends_ref[0] = jnp.broadcast_to(new_base, ends_ref.shape[1:])


def _route(logits):
    n = logits.shape[0]
    tm = TOK_TILE
    ns = n // tm
    row = lambda s: (s, 0)
    return pl.pallas_call(
        _route_kernel,
        grid=(ns,),
        in_specs=[pl.BlockSpec((tm, LANES), row)],
        out_specs=[
            pl.BlockSpec((tm, LANES), row),
            pl.BlockSpec((LANES, tm), lambda s: (0, s)),
            pl.BlockSpec((LANES, tm), lambda s: (0, s)),
            pl.BlockSpec((1, 8, LANES), lambda s: (s, 0, 0)),
        ],
        out_shape=[
            jax.ShapeDtypeStruct((n, LANES), F32),
            jax.ShapeDtypeStruct((LANES, n), F32),
            jax.ShapeDtypeStruct((LANES, n), F32),
            jax.ShapeDtypeStruct((ns, 8, LANES), F32),
        ],
        scratch_shapes=[pltpu.VMEM((8, LANES), F32)],
        name="moe_route",
    )(logits)


def _work_lists(ends, n_tokens):
    e_n = N_EXPERTS
    sub = EXP_TM // DISP_TD
    max_tiles = (2 * n_tokens) // EXP_TM + e_n
    i32 = jnp.int32

    ends_i = ends[:, 0, :e_n].astype(i32)
    base = jnp.concatenate([jnp.zeros((1, e_n), i32), ends_i], axis=0)
    cnt = base[-1]
    ntile = (cnt + EXP_TM - 1) // EXP_TM
    tile_end = jnp.cumsum(ntile)
    tile_start = tile_end - ntile
    total_tiles = tile_end[-1]
    nunit = (cnt + EXP_UNIT - 1) // EXP_UNIT
    off = tile_start * EXP_TM

    e_ids = jnp.arange(e_n, dtype=i32)

    def expert_of(tile):
        e = jnp.sum((tile_end[None, :] <= tile[:, None]).astype(i32), axis=1)
        return e, (e[:, None] == e_ids[None, :]).astype(i32)

    def pick(onehot, per_expert):
        return jnp.sum(onehot * per_expert[None, :], axis=1)

    t = jnp.arange(max_tiles, dtype=i32)
    tc = jnp.minimum(t, total_tiles - 1)
    tile_e, oh_t = expert_of(tc)
    tile_pass = jnp.clip(pick(oh_t, nunit) - (tc - pick(oh_t, tile_start)) * (EXP_TM // EXP_UNIT),
                         1, EXP_TM // EXP_UNIT)
    tile_pass = jnp.where(t < total_tiles, tile_pass, 0)

    j = jnp.arange(max_tiles * sub, dtype=i32)
    jt = jnp.minimum(j // sub, total_tiles - 1)
    je, oh_j = expert_of(jt)
    jm = (jt - pick(oh_j, tile_start)) * sub + j % sub
    lo_rank = jm * DISP_TD
    hi_rank = jnp.minimum(lo_rank + DISP_TD, pick(oh_j, cnt))
    base_e = jnp.sum(oh_j[:, :, None] * base.T[None, :, :], axis=1)
    s_lo = jnp.sum((base_e[:, 1:] <= lo_rank[:, None]).astype(i32), axis=1)
    s_hi = jnp.sum((base_e[:, :-1] < hi_rank[:, None]).astype(i32), axis=1)
    n_src = jnp.where(j // sub < total_tiles, jnp.maximum(s_hi - s_lo, 0), 0)

    rows = max_tiles * EXP_TM
    start = off[None, :] + base[:-1]
    stop = off[None, :] + base[1:]
    wstart = jnp.minimum(start // COMB_ALIGN * COMB_ALIGN, rows - COMB_WIN)
    need = jnp.where(stop > start, stop - wstart, 0)
    rel = wstart - off[None, :]
    split = COMB_CHUNKS[0][1]
    wtail = jnp.where(need > split, (wstart + split) // COMB_ALIGN, 0)
    return dict(
        disp=(je, jm, s_lo, n_src),
        comb=((wstart // COMB_ALIGN).reshape(-1), wtail.reshape(-1), rel.reshape(-1),
              need.reshape(-1)),
        tiles=(tc, tile_e, tile_pass),
        max_tiles=max_tiles)


def _dispatch_kernel(e_ref, m_ref, slo_ref, n_ref, h_ref, keyt_ref, gatet_ref, o_ref, gs_ref,
                     acc_ref, gacc_ref):
    td = DISP_TD
    for sub in range(DISP_PER_STEP):
        _dispatch_tile(pl.program_id(0) * DISP_PER_STEP + sub, slice(sub * td, (sub + 1) * td),
                       e_ref, m_ref, slo_ref, n_ref, h_ref, keyt_ref, gatet_ref, o_ref, gs_ref,
                       acc_ref, gacc_ref)


def _dispatch_tile(j, rs, e_ref, m_ref, slo_ref, n_ref, h_ref, keyt_ref, gatet_ref, o_ref, gs_ref,
                   acc_ref, gacc_ref):
    td = DISP_TD
    e = e_ref[j]
    slot = (m_ref[j] * td + lax.broadcasted_iota(jnp.int32, (td, TOK_TILE), 0)).astype(F32)

    def item(i, assign):
        ts = pl.multiple_of((slo_ref[j] + i) * TOK_TILE, TOK_TILE)
        key = keyt_ref[pl.ds(e, 1), pl.ds(ts, TOK_TILE)]
        hit = key == slot
        onehot = jnp.where(hit, 1.0, 0.0).astype(BF16)
        rows = _dot(onehot, h_ref[pl.ds(ts, TOK_TILE), :])
        gate = gatet_ref[pl.ds(e, 1), pl.ds(ts, TOK_TILE)]
        gsum = jnp.sum(jnp.where(hit, gate, 0.0), axis=-1, keepdims=True)
        if assign:
            acc_ref[...] = rows
            gacc_ref[...] = gsum
        else:
            acc_ref[...] += rows
            gacc_ref[...] += gsum

    @pl.when(n_ref[j] == 0)
    def _():
        acc_ref[...] = jnp.zeros_like(acc_ref)
        gacc_ref[...] = jnp.zeros_like(gacc_ref)

    @pl.when(n_ref[j] > 0)
    def _():
        item(0, True)

    def body(i, carry):
        item(i, False)
        return carry

    lax.fori_loop(1, n_ref[j], body, 0)
    o_ref[rs, :] = acc_ref[...].astype(o_ref.dtype)
    gs_ref[rs, :] = gacc_ref[...]


def _dispatch(h2, key_t, gate_t, lists, max_tiles):
    n, d = h2.shape
    rows = max_tiles * EXP_TM
    resident = dict(pipeline_mode=pl.Buffered(1))
    grid_spec = pltpu.PrefetchScalarGridSpec(
        num_scalar_prefetch=4,
        grid=(rows // (DISP_TD * DISP_PER_STEP),),
        in_specs=[
            pl.BlockSpec((n, d), lambda j, *_: (0, 0), **resident),
            pl.BlockSpec((8, n), lambda j, *_: (0, 0), **resident),
            pl.BlockSpec((8, n), lambda j, *_: (0, 0), **resident),
        ],
        out_specs=[pl.BlockSpec((DISP_TD * DISP_PER_STEP, d), lambda j, *_: (j, 0)),
                   pl.BlockSpec((DISP_TD * DISP_PER_STEP, 1), lambda j, *_: (j, 0))],
        scratch_shapes=[pltpu.VMEM((DISP_TD, d), F32), pltpu.VMEM((DISP_TD, 1), F32)],
    )
    return pl.pallas_call(
        _dispatch_kernel,
        grid_spec=grid_spec,
        out_shape=[jax.ShapeDtypeStruct((rows, d), BF16),
                   jax.ShapeDtypeStruct((rows, 1), F32)],
        name="moe_dispatch",
    )(*lists, h2, key_t, gate_t)


def _expert_kernel(tidx_ref, te_ref, tp_ref, x_ref, gs_ref, wg_ref, wu_ref, wd_ref, y_ref,
                   acc_ref):
    i = pl.program_id(0)
    f = pl.program_id(1)
    nf = pl.num_programs(1)

    max_units = EXP_TM // EXP_UNIT
    n_steps = D_FF_EXPERT // EXP_TF

    def run(units, first, last):
        done = 0
        while done < units * EXP_UNIT:
            size = EXP_SUB if units * EXP_UNIT - done >= EXP_SUB else EXP_UNIT
            rs = slice(done, done + size)
            done += size
            x = x_ref[rs, :]
            g = _dot(x, wg_ref[0])
            u = _dot(x, wu_ref[0])
            act = (g * jax.nn.sigmoid(g) * u).astype(BF16)
            part = _dot(act, wd_ref[0])
            total = part if first else acc_ref[rs, :] + part
            if last:
                y_ref[rs, :] = (gs_ref[rs, :] * total).astype(y_ref.dtype)
            else:
                acc_ref[rs, :] = total
        if last and done < EXP_TM:
            y_ref[done:, :] = jnp.zeros((EXP_TM - done, y_ref.shape[1]), y_ref.dtype)

    kinds = [(f == 0, True, n_steps == 1), (f == nf - 1, n_steps == 1, True)]
    if n_steps > 2:
        kinds.append((jnp.logical_and(f > 0, f < nf - 1), False, False))
    for units in range(max_units + 1):
        for cond, first, last in kinds[:1] if n_steps == 1 else kinds:
            @pl.when(jnp.logical_and(tp_ref[i] == units, cond))
            def _(units=units, first=first, last=last):
                run(units, first, last)


def _experts(xs, gs, w_gu, w_down, tiles, max_tiles):
    rows, d = xs.shape
    tm, tf = EXP_TM, EXP_TF
    nf = D_FF_EXPERT // tf

    def fsel(i, f, tp):
        return jnp.where(tp[i] > 0, f, nf - 1)

    grid_spec = pltpu.PrefetchScalarGridSpec(
        num_scalar_prefetch=3,
        grid=(max_tiles, nf),
        in_specs=[
            pl.BlockSpec((tm, d), lambda i, f, ti, te, tv: (ti[i], 0)),
            pl.BlockSpec((tm, 1), lambda i, f, ti, te, tv: (ti[i], 0)),
            pl.BlockSpec((1, d, tf), lambda i, f, ti, te, tv: (te[i], 0, fsel(i, f, tv))),
            pl.BlockSpec((1, d, tf), lambda i, f, ti, te, tv: (te[i], 0, nf + fsel(i, f, tv))),
            pl.BlockSpec((1, tf, d), lambda i, f, ti, te, tv: (te[i], fsel(i, f, tv), 0)),
        ],
        out_specs=pl.BlockSpec((tm, d), lambda i, f, ti, te, tv: (i, 0)),
        scratch_shapes=[pltpu.VMEM((tm, d), F32)],
    )
    return pl.pallas_call(
        _expert_kernel,
        grid_spec=grid_spec,
        out_shape=jax.ShapeDtypeStruct((rows, d), BF16),
        name="moe_experts",
    )(*tiles, xs, gs, w_gu, w_gu, w_down)


def _combine_kernel(ws_ref, wt_ref, rel_ref, need_ref, *refs):
    head_refs = refs[:N_EXPERTS]
    tail_refs = refs[N_EXPERTS:2 * N_EXPERTS]
    keyc_ref, x_ref, mod_ref, gpost_ref, o_ref, acc_ref = refs[2 * N_EXPERTS:]
    s = pl.program_id(0)
    ts = acc_ref.shape[0]

    def term(e, y_ref, a):
        key = keyc_ref[:, e:e + 1]
        rel = rel_ref[s * N_EXPERTS + e]
        slot = (rel + a + lax.broadcasted_iota(jnp.int32, (ts, y_ref.shape[0]), 1)).astype(F32)
        onehot = jnp.where(key == slot, 1.0, 0.0).astype(BF16)
        return _dot(onehot, y_ref[...])

    acc = term(0, head_refs[0], 0)
    for e in range(1, N_EXPERTS):
        acc = acc + term(e, head_refs[e], 0)
    acc_ref[...] = acc
    split = COMB_CHUNKS[0][1]
    for e in range(N_EXPERTS):
        @pl.when(need_ref[s * N_EXPERTS + e] > split)
        def _(e=e):
            acc_ref[...] += term(e, tail_refs[e], split)
    gt_f = mod_ref[0, 5:6, :]
    o_ref[...] = x_ref[...] + (1.0 + gt_f) * _rms(acc_ref[...], gpost_ref[...])


def _combine(y, key_c, x2, mod_l, g_post, lists, seq):
    n, d = x2.shape
    per_b = seq // TOK_TILE
    tok = lambda s, *_: (s, 0)

    (a0, b0), (a1, b1) = COMB_CHUNKS

    def head(e):
        return pl.BlockSpec((pl.Element(b0 - a0), pl.Element(d)),
                            lambda s, ws, wt, rel, need: (ws[s * N_EXPERTS + e] * COMB_ALIGN, 0))

    def tail(e):
        return pl.BlockSpec((pl.Element(b1 - a1), pl.Element(d)),
                            lambda s, ws, wt, rel, need: (wt[s * N_EXPERTS + e] * COMB_ALIGN, 0))

    grid_spec = pltpu.PrefetchScalarGridSpec(
        num_scalar_prefetch=4,
        grid=(n // TOK_TILE,),
        in_specs=[head(e) for e in range(N_EXPERTS)] + [tail(e) for e in range(N_EXPERTS)] + [
            pl.BlockSpec((TOK_TILE, LANES), tok),
            pl.BlockSpec((TOK_TILE, d), tok),
            pl.BlockSpec((1, N_MOD, d), lambda s, *_: (s // per_b, 0, 0)),
            pl.BlockSpec((1, d), lambda s, *_: (0, 0)),
        ],
        out_specs=pl.BlockSpec((TOK_TILE, d), tok),
        scratch_shapes=[pltpu.VMEM((TOK_TILE, d), F32)],
    )
    return pl.pallas_call(
        _combine_kernel,
        grid_spec=grid_spec,
        out_shape=jax.ShapeDtypeStruct((n, d), F32),
        name="moe_combine",
    )(*lists, *([y] * (2 * N_EXPERTS)), key_c, x2, mod_l, g_post)


def kernel(x, c, w_ada, b_ada, g_pre_mix, g_post_mix, g_pre_ffn, g_post_ffn, w_in, b_forget, g_v,
           w_spatial, b_spatial, g_out_attn, g_out_sgu, w_out, w_gate_up_dense, w_down_dense,
           w_router, w_gate_up_exp, w_down_exp):
    bsz, seq, d = x.shape
    depth = w_ada.shape[0]
    n = bsz * seq
    npair = N_HEADS // 2
    assert d == D_MODEL and seq % PROJ_TILE == 0 and seq % TOK_TILE == 0 and seq % ATT_TQ == 0

    w_gu_exp3 = w_gate_up_exp.reshape(-1, N_EXPERTS * d, 2 * D_FF_EXPERT)
    w_down_exp3 = w_down_exp.reshape(-1, N_EXPERTS * D_FF_EXPERT, d)
    riders = {("w_in", 0): (w_in, 0), ("w_out", 0): (w_out, 0)}
    mod, *casts = _modulation(c, w_ada, b_ada, tuple(riders.values()))
    bf16 = dict(zip(riders.keys(), casts))
    mod = mod.reshape(depth, bsz, N_MOD, d)
    x2 = x.reshape(n, d)
    for l in range(depth):
        mod_l = mod[l]
        moe = (l % 2 == 1)
        q, k, v, zf_t, u, sv = _inproj(x2, mod_l, g_pre_mix[l].reshape(1, d), bf16[("w_in", l)],
                                       g_v[l].reshape(1, SGU_W), seq)
        riders = {}
        if l + 1 < depth:
            riders[("w_in", l + 1)] = (w_in, l + 1)
            riders[("w_out", l + 1)] = (w_out, l + 1)
        if moe:
            riders[("w_gu_exp", l // 2)] = (w_gu_exp3, l // 2)
        else:
            riders[("w_gu_dense", l // 2)] = (w_gate_up_dense, l // 2)
            riders[("w_down_dense", l // 2)] = (w_down_dense, l // 2)
            if l + 1 < depth:
                riders[("w_down_exp", l // 2)] = (w_down_exp3, l // 2)
        attn, *casts = _attention(q, k, v, zf_t.reshape(npair, 2, n),
                                  b_forget[l].reshape(npair, 2, 1), bsz, seq,
                                  tuple(riders.values()))
        bf16.update(zip(riders.keys(), casts))
        sgun = _sgu(u, sv, w_spatial[l].astype(BF16), b_spatial[l].T,
                    g_out_sgu[l].reshape(1, SGU_W))
        w_r = None
        if moe:
            w_r = jnp.pad(w_router[l // 2], ((0, 0), (0, LANES - N_EXPERTS)))
        res = _outproj(attn, sgun, g_out_attn[l].reshape(1, ATTN_W), bf16[("w_out", l)],
                       x2, mod_l, g_post_mix[l].reshape(1, d), g_pre_ffn[l].reshape(1, d), w_r, seq)
        g_post = g_post_ffn[l].reshape(1, d)
        if not moe:
            x2, h2 = res
            x2 = _ffn_dense(h2, bf16[("w_gu_dense", l // 2)], bf16[("w_down_dense", l // 2)],
                            x2, mod_l, g_post, seq)
        else:
            x2, h2, logits = res
            key_c, gate_t, key_t, ends = _route(logits)
            wl = _work_lists(ends, n)
            xs, gs = _dispatch(h2, key_t, gate_t, wl["disp"], wl["max_tiles"])
            w_gu = bf16[("w_gu_exp", l // 2)].reshape(N_EXPERTS, d, 2 * D_FF_EXPERT)
            w_dn = bf16[("w_down_exp", l // 2)].reshape(N_EXPERTS, D_FF_EXPERT, d)
            y = _experts(xs, gs, w_gu, w_dn, wl["tiles"], wl["max_tiles"])
            x2 = _combine(y, key_c, x2, mod_l, g_post, wl["comb"], seq)
    return x2.reshape(bsz, seq, d)
```

```python
import functools
import math

import jax
import jax.numpy as jnp
from jax import lax
from jax.experimental import pallas as pl
from jax.experimental.pallas import tpu as pltpu

F32 = jnp.float32
BF16 = jnp.bfloat16

D_MODEL = 1024
HEAD_DIM = 64
N_HEADS = 8
N_GROUPS = 8
ATTN_W = N_HEADS * HEAD_DIM
SGU_W = N_GROUPS * HEAD_DIM
CHUNK = 128
D_FF_DENSE = 2816
N_EXPERTS = 8
D_FF_EXPERT = 3584
N_MOD = 6
EPS = 1e-6

LANES = 128
TOK_TILE = 512
PROJ_TILE = 1024
ROW_GROUP = 256
SGU_TILE = 2048
MOD_BLOCKS = 8
ATT_TQ = 256
ATT_AHEAD = 2
MXU_TILE = 256
FFN_CHUNKS = ((0, 6 * MXU_TILE), (6 * MXU_TILE, D_FF_DENSE))
EXP_TM = 1024
EXP_SUB = 512
EXP_UNIT = 512
EXP_TF = 7 * MXU_TILE
LOG2E = math.log2(math.e)
DISP_TD = 256
DISP_PER_STEP = 4
COMB_ALIGN = 16
COMB_WIN = TOK_TILE + COMB_ALIGN
COMB_CHUNKS = ((0, MXU_TILE), (MXU_TILE, COMB_WIN))
UNROUTED = -float(2 ** 30)
V7X_VMEM_BYTES = 64 * 1024 * 1024
_BIG_VMEM = pltpu.CompilerParams(vmem_limit_bytes=V7X_VMEM_BYTES - 8 * 1024 * 1024)
GELU_C = math.sqrt(2.0 / math.pi)


def _rms(x, g):
    return x * lax.rsqrt(jnp.mean(x * x, axis=-1, keepdims=True) + EPS) * g


def _dot(a, b):
    return jnp.dot(a, b, preferred_element_type=F32)


def _row_groups(rows):
    return [slice(r, r + ROW_GROUP) for r in range(0, rows, ROW_GROUP)]


def _rider_specs(riders, steps, step_of):
    ins, outs, shapes = [], [], []
    for w, lead in riders:
        _, rows, cols = w.shape
        share = 1
        while rows % (steps // share) or (rows // (steps // share)) % COMB_ALIGN:
            share *= 2
        rb = rows // (steps // share)
        ins.append(pl.BlockSpec(
            (1, rb, cols), lambda *g, lead=lead, share=share: (lead, step_of(*g) // share, 0)))
        outs.append(pl.BlockSpec((rb, cols), lambda *g, share=share: (step_of(*g) // share, 0)))
        shapes.append(jax.ShapeDtypeStruct((rows, cols), BF16))
    return ins, outs, shapes


def _cast_riders(src_refs, dst_refs):
    for src, dst in zip(src_refs, dst_refs):
        dst[...] = src[0].astype(dst.dtype)


def _mod_kernel(c_ref, w_ref, b_ref, *rest):
    n_ride = (len(rest) - 1) // 2
    o_ref = rest[n_ride]
    _cast_riders(rest[:n_ride], rest[n_ride + 1:])
    c = c_ref[...]
    ca = (c * jax.nn.sigmoid(c)).astype(BF16)
    o_ref[0] = _dot(ca, w_ref[0].astype(BF16)) + b_ref[0]


def _modulation(c, w_ada, b_ada, riders=()):
    depth, d, m = w_ada.shape
    bsz = c.shape[0]
    nblk = MOD_BLOCKS
    cb = m // nblk
    r_in, r_out, r_shape = _rider_specs(riders, depth * nblk, lambda l, j: l * nblk + j)
    return pl.pallas_call(
        _mod_kernel,
        grid=(depth, nblk),
        in_specs=[
            pl.BlockSpec((bsz, d), lambda l, j: (0, 0)),
            pl.BlockSpec((1, d, cb), lambda l, j: (l, 0, j)),
            pl.BlockSpec((1, 1, cb), lambda l, j: (l, 0, j)),
        ] + r_in,
        out_specs=[pl.BlockSpec((1, bsz, cb), lambda l, j: (l, 0, j))] + r_out,
        out_shape=[jax.ShapeDtypeStruct((depth, bsz, m), F32)] + r_shape,
        name="adaln_mod",
    )(c, w_ada, b_ada.reshape(depth, 1, m), *[w for w, _ in riders])


def _inproj_kernel(x_ref, mod_ref, gpre_ref, w_ref, gv_ref,
                   q_ref, k_ref, v_ref, zf_ref, u_ref, sv_ref, wuv_ref):
    o3 = 3 * ATTN_W
    o4 = o3 + N_HEADS

    @pl.when(pl.program_id(0) == 0)
    def _():
        wuv_ref[...] = w_ref[:, o4:]

    sh = mod_ref[0, 0:1, :]
    sc = mod_ref[0, 1:2, :]
    for rs in _row_groups(x_ref.shape[0]):
        h = (_rms(x_ref[rs, :], gpre_ref[...]) * (1.0 + sc) + sh).astype(BF16)
        qkv = _dot(h, w_ref[:, :o3])
        q_ref[rs, :] = (qkv[:, :ATTN_W] * (HEAD_DIM ** -0.5 * LOG2E)).astype(BF16)
        k_ref[rs, :] = qkv[:, ATTN_W:2 * ATTN_W].astype(BF16)
        v_ref[rs, :] = qkv[:, 2 * ATTN_W:].astype(BF16)
        zf_ref[:, rs] = _dot(h, w_ref[:, o3:o3 + LANES]).T[:N_HEADS, :]
        uv = _dot(h, wuv_ref[...])
        uv = uv * (0.5 * (1.0 + jnp.tanh(GELU_C * (uv + 0.044715 * (uv * uv * uv)))))
        u_ref[rs, :] = uv[:, :SGU_W].astype(BF16)
        s = uv[:, SGU_W:]
        sc_ = s - jnp.mean(s, axis=-1, keepdims=True)
        sv = sc_ * lax.rsqrt(jnp.mean(sc_ * sc_, axis=-1, keepdims=True) + EPS) * gv_ref[...]
        sv_ref[rs, :] = sv.astype(BF16)


def _inproj(x2, mod_l, g_pre, w_in_l, g_v, seq):
    n, d = x2.shape
    tm = PROJ_TILE
    per_b = seq // tm
    row = lambda i: (i, 0)
    const = lambda i: (0, 0)
    resident = dict(pipeline_mode=pl.Buffered(1))
    outs = [jax.ShapeDtypeStruct((n, ATTN_W), BF16)] * 3 + [
        jax.ShapeDtypeStruct((N_HEADS, n), F32),
        jax.ShapeDtypeStruct((n, SGU_W), BF16),
        jax.ShapeDtypeStruct((n, SGU_W), BF16)]
    return pl.pallas_call(
        _inproj_kernel,
        grid=(n // tm,),
        in_specs=[
            pl.BlockSpec((tm, d), row),
            pl.BlockSpec((1, N_MOD, d), lambda i: (i // per_b, 0, 0)),
            pl.BlockSpec((1, d), const),
            pl.BlockSpec(w_in_l.shape, const, **resident),
            pl.BlockSpec((1, SGU_W), const),
        ],
        out_specs=[pl.BlockSpec((tm, ATTN_W), row)] * 3 + [
            pl.BlockSpec((N_HEADS, tm), lambda i: (0, i)),
            pl.BlockSpec((tm, SGU_W), row),
            pl.BlockSpec((tm, SGU_W), row)],
        out_shape=outs,
        scratch_shapes=[pltpu.VMEM((d, 2 * SGU_W), BF16)],
        name="inproj",
    )(x2, mod_l, g_pre, w_in_l, g_v)


def _cum_log_gates(z):
    ls = jnp.minimum(z, 0.0) - jnp.log1p(jnp.exp(-jnp.abs(z)))
    seq = ls.shape[-1]
    pos = lax.broadcasted_iota(jnp.int32, ls.shape, 1)
    shift = 1
    while shift < seq:
        ls = ls + jnp.where(pos >= shift, pltpu.roll(ls, shift, 1), 0.0)
        shift *= 2
    return ls * LOG2E


def _attn_kernel(q_ref, k_ref, v_ref, zf_ref, bf_ref, *rest):
    n_ride = (len(rest) - 1) // 2
    o_ref = rest[n_ride]
    _cast_riders(rest[:n_ride], rest[n_ride + 1:])
    tq = ATT_TQ
    seq = q_ref.shape[0]
    nq = seq // tq
    z2 = zf_ref[0] + bf_ref[0]
    l_keys = _cum_log_gates(jnp.concatenate([z2, jnp.zeros((8 - 2, seq), F32)], axis=0))
    l_rows = jnp.concatenate([l_keys, jnp.zeros((LANES - 8, seq), F32)], axis=0)
    l_cols = l_rows.T
    lane = lax.broadcasted_iota(jnp.int32, (tq, LANES), 1)
    left = lane < HEAD_DIM
    rows = lax.broadcasted_iota(jnp.int32, (tq, tq), 0)
    cols = lax.broadcasted_iota(jnp.int32, (tq, tq), 1)
    causal = cols <= rows
    zero = jnp.zeros((tq, LANES), BF16)
    one = jnp.ones((tq, LANES), BF16)
    def scores(qi):
        qs = slice(qi * tq, (qi + 1) * tq)
        q2 = q_ref[qs, :]
        qq = jnp.concatenate([jnp.where(left, q2, zero), jnp.where(left, zero, q2)], axis=0)
        t_blocks = [[], []]
        mt = [None, None]
        for j in range(qi + 1):
            ks = slice(j * tq, (j + 1) * tq)
            t2 = lax.dot_general(qq, k_ref[ks, :], (((1,), (1,)), ((), ())),
                                 preferred_element_type=F32)
            for hh in range(2):
                t = t2[hh * tq:(hh + 1) * tq] - l_keys[hh:hh + 1, ks]
                if j == qi:
                    t = jnp.where(causal, t, -jnp.inf)
                t_blocks[hh].append(t)
                mj = jnp.maximum(t[:, :LANES], t[:, LANES:])
                mt[hh] = mj if mt[hh] is None else jnp.maximum(mt[hh], mj)
        shift = []
        for hh in range(2):
            lq = l_cols[qs, hh:hh + 1]
            row_max = jnp.max(mt[hh], axis=-1, keepdims=True) + lq
            shift.append(row_max - lq)
        return t_blocks, shift

    def values(qi, t_blocks, shift):
        qs = slice(qi * tq, (qi + 1) * tq)
        acc = None
        for j in range(qi + 1):
            ks = slice(j * tq, (j + 1) * tq)
            pp = jnp.concatenate([jnp.exp2(t_blocks[hh][j] - shift[hh]) for hh in range(2)],
                                 axis=0).astype(BF16)
            vaug = jnp.concatenate([v_ref[ks, :], one], axis=1)
            part = _dot(pp, vaug)
            acc = part if acc is None else acc + part
        out0 = acc[:tq, :LANES] / acc[:tq, LANES:]
        out1 = acc[tq:, :LANES] / acc[tq:, LANES:]
        o_ref[qs, :] = jnp.where(left, out0, out1).astype(o_ref.dtype)

    order = list(range(nq - 1, -1, -1))
    pending = [scores(qi) for qi in order[:ATT_AHEAD]]
    for pos, qi in enumerate(order):
        if pos + ATT_AHEAD < nq:
            pending.append(scores(order[pos + ATT_AHEAD]))
        values(qi, *pending.pop(0))


def _attention(q, k, v, zf_pairs, bf_pairs, bsz, seq, riders=()):
    n = q.shape[0]
    npair = N_HEADS // 2
    blk = pl.BlockSpec((seq, LANES), lambda b, p: (b, p))
    r_in, r_out, r_shape = _rider_specs(riders, bsz * npair, lambda b, p: b * npair + p)
    return pl.pallas_call(
        _attn_kernel,
        grid=(bsz, npair),
        in_specs=[blk, blk, blk, pl.BlockSpec((1, 2, seq), lambda b, p: (p, 0, b)),
                  pl.BlockSpec((1, 2, 1), lambda b, p: (p, 0, 0))] + r_in,
        out_specs=[blk] + r_out,
        out_shape=[jax.ShapeDtypeStruct((n, ATTN_W), BF16)] + r_shape,
        name="fox_attention",
    )(q, k, v, zf_pairs, bf_pairs, *[w for w, _ in riders])


def _sgu_kernel(u_ref, sv_ref, w_ref, bt_ref, g_ref, o_ref):
    tm = u_ref.shape[0]
    npair = N_GROUPS // 2
    lane = lax.broadcasted_iota(jnp.int32, (CHUNK, LANES), 1)
    left = lane < HEAD_DIM
    r = lax.broadcasted_iota(jnp.int32, (CHUNK, CHUNK), 0)
    c = lax.broadcasted_iota(jnp.int32, (CHUNK, CHUNK), 1)
    causal = c <= r
    lhs, bias = [], []
    for j in range(npair):
        wa = jnp.where(causal, w_ref[2 * j], jnp.zeros((), BF16))
        wb = jnp.where(causal, w_ref[2 * j + 1], jnp.zeros((), BF16))
        lhs.append(jnp.concatenate([wa, wb], axis=1))
        bias.append(jnp.where(left, bt_ref[:, 2 * j:2 * j + 1], bt_ref[:, 2 * j + 1:2 * j + 2]))
    for ci in range(tm // CHUNK):
        rs = slice(ci * CHUNK, (ci + 1) * CHUNK)
        blks = []
        ssq = jnp.zeros((CHUNK, 1), F32)
        for j in range(npair):
            cs = slice(j * LANES, (j + 1) * LANES)
            svb = sv_ref[rs, cs]
            zero = jnp.zeros_like(svb)
            rhs = jnp.concatenate([jnp.where(left, svb, zero), jnp.where(left, zero, svb)], axis=0)
            mixed = _dot(lhs[j], rhs) + bias[j]
            ob = u_ref[rs, cs].astype(F32) * mixed
            ssq = ssq + jnp.sum(ob * ob, axis=-1, keepdims=True)
            blks.append(ob)
        inv = lax.rsqrt(ssq * (1.0 / SGU_W) + EPS)
        for j in range(npair):
            cs = slice(j * LANES, (j + 1) * LANES)
            o_ref[rs, cs] = (blks[j] * inv * g_ref[:, cs]).astype(o_ref.dtype)


def _sgu(u, sv, w_sp, b_sp_t, g_out):
    n = u.shape[0]
    tm = SGU_TILE
    row = lambda i: (i, 0)
    return pl.pallas_call(
        _sgu_kernel,
        grid=(n // tm,),
        in_specs=[
            pl.BlockSpec((tm, SGU_W), row),
            pl.BlockSpec((tm, SGU_W), row),
            pl.BlockSpec(w_sp.shape, lambda i: (0, 0, 0)),
            pl.BlockSpec(b_sp_t.shape, lambda i: (0, 0)),
            pl.BlockSpec((1, SGU_W), lambda i: (0, 0)),
        ],
        out_specs=pl.BlockSpec((tm, SGU_W), row),
        out_shape=jax.ShapeDtypeStruct((n, SGU_W), BF16),
        name="sgu",
    )(u, sv, w_sp, b_sp_t, g_out)


def _outproj_kernel(a_ref, s_ref, ga_ref, wo_ref, x_ref, mod_ref, gpost_ref, gffn_ref,
                    *rest, with_router):
    if with_router:
        wr_ref, xo_ref, h_ref, lg_ref = rest
    else:
        xo_ref, h_ref = rest
    gt_m = mod_ref[0, 2:3, :]
    sh_f = mod_ref[0, 3:4, :]
    sc_f = mod_ref[0, 4:5, :]
    if with_router:
        wr = wr_ref[...]
        w_hi = wr.astype(BF16)
        w_lo = (wr - w_hi.astype(F32)).astype(BF16)
    for rs in _row_groups(x_ref.shape[0]):
        a = _rms(a_ref[rs, :].astype(F32), ga_ref[...]).astype(BF16)
        y = _dot(a, wo_ref[:ATTN_W, :]) + _dot(s_ref[rs, :], wo_ref[ATTN_W:, :])
        xn = x_ref[rs, :] + (1.0 + gt_m) * _rms(y, gpost_ref[...])
        xo_ref[rs, :] = xn
        h = _rms(xn, gffn_ref[...]) * (1.0 + sc_f) + sh_f
        h_hi = h.astype(BF16)
        h_ref[rs, :] = h_hi
        if with_router:
            h_lo = (h - h_hi.astype(F32)).astype(BF16)
            lg_ref[rs, :] = _dot(h_hi, w_hi) + (_dot(h_hi, w_lo) + _dot(h_lo, w_hi))


def _outproj(attn, sgun, g_attn, w_o, x2, mod_l, g_post, g_ffn, w_router, seq):
    n, d = x2.shape
    tm = PROJ_TILE
    per_b = seq // tm
    row = lambda i: (i, 0)
    const = lambda i: (0, 0)
    with_router = w_router is not None
    in_specs = [
        pl.BlockSpec((tm, ATTN_W), row),
        pl.BlockSpec((tm, SGU_W), row),
        pl.BlockSpec((1, ATTN_W), const),
        pl.BlockSpec(w_o.shape, const),
        pl.BlockSpec((tm, d), row),
        pl.BlockSpec((1, N_MOD, d), lambda i: (i // per_b, 0, 0)),
        pl.BlockSpec((1, d), const),
        pl.BlockSpec((1, d), const),
    ]
    args = [attn, sgun, g_attn, w_o, x2, mod_l, g_post, g_ffn]
    out_specs = [pl.BlockSpec((tm, d), row), pl.BlockSpec((tm, d), row)]
    out_shape = [jax.ShapeDtypeStruct((n, d), F32), jax.ShapeDtypeStruct((n, d), BF16)]
    if with_router:
        in_specs.append(pl.BlockSpec(w_router.shape, const))
        args.append(w_router)
        out_specs.append(pl.BlockSpec((tm, LANES), row))
        out_shape.append(jax.ShapeDtypeStruct((n, LANES), F32))
    return pl.pallas_call(
        functools.partial(_outproj_kernel, with_router=with_router),
        grid=(n // tm,),
        in_specs=in_specs,
        out_specs=out_specs,
        out_shape=out_shape,
        name="outproj_router" if with_router else "outproj",
    )(*args)


def _ffn_kernel(h_ref, wgu_ref, wd_ref, x_ref, mod_ref, gpost_ref, o_ref):
    h = h_ref[...]
    y = None
    for a, b in FFN_CHUNKS:
        g = _dot(h, wgu_ref[:, a:b])
        u = _dot(h, wgu_ref[:, D_FF_DENSE + a:D_FF_DENSE + b])
        act = (g * jax.nn.sigmoid(g) * u).astype(BF16)
        part = _dot(act, wd_ref[a:b, :])
        y = part if y is None else y + part
    gt_f = mod_ref[0, 5:6, :]
    o_ref[...] = x_ref[...] + (1.0 + gt_f) * _rms(y, gpost_ref[...])


def _ffn_dense(h2, w_gu, w_down, x2, mod_l, g_post, seq):
    n, d = x2.shape
    tm = TOK_TILE
    per_b = seq // tm
    resident = dict(pipeline_mode=pl.Buffered(1))
    return pl.pallas_call(
        _ffn_kernel,
        grid=(n // tm,),
        in_specs=[
            pl.BlockSpec((tm, d), lambda i: (i, 0)),
            pl.BlockSpec(w_gu.shape, lambda i: (0, 0), **resident),
            pl.BlockSpec(w_down.shape, lambda i: (0, 0), **resident),
            pl.BlockSpec((tm, d), lambda i: (i, 0)),
            pl.BlockSpec((1, N_MOD, d), lambda i: (i // per_b, 0, 0)),
            pl.BlockSpec((1, d), lambda i: (0, 0)),
        ],
        out_specs=pl.BlockSpec((tm, d), lambda i: (i, 0)),
        out_shape=jax.ShapeDtypeStruct((n, d), F32),
        compiler_params=_BIG_VMEM,
        name="ffn_dense",
    )(h2, w_gu, w_down, x2, mod_l, g_post)


def _route_kernel(lg_ref, keyc_ref, gate_ref, keyt_ref, ends_ref, base_ref):
    s = pl.program_id(0)
    tm = lg_ref.shape[0]

    @pl.when(s == 0)
    def _():
        base_ref[...] = jnp.zeros_like(base_ref)

    lane = lax.broadcasted_iota(jnp.int32, (tm, LANES), 1)
    lg = jnp.where(lane < N_EXPERTS, lg_ref[...], -jnp.inf)
    m1 = jnp.max(lg, axis=-1, keepdims=True)
    i1 = jnp.min(jnp.where(lg == m1, lane, LANES), axis=-1, keepdims=True)
    sel1 = lane == i1
    lg2 = jnp.where(sel1, -jnp.inf, lg)
    m2 = jnp.max(lg2, axis=-1, keepdims=True)
    i2 = jnp.min(jnp.where(lg2 == m2, lane, LANES), axis=-1, keepdims=True)
    sel2 = lane == i2
    e2 = jnp.exp(m2 - m1)
    w1 = 1.0 / (1.0 + e2)
    w2 = e2 / (1.0 + e2)
    gate_ref[...] = (jnp.where(sel1, w1, 0.0) + jnp.where(sel2, w2, 0.0)).T
    chosen = jnp.logical_or(sel1, sel2)
    mask = jnp.where(chosen, 1.0, 0.0)
    r = lax.broadcasted_iota(jnp.int32, (tm, tm), 0)
    c = lax.broadcasted_iota(jnp.int32, (tm, tm), 1)
    tri = jnp.where(c < r, 1.0, 0.0).astype(BF16)
    base = base_ref[0:1, :]
    rank = _dot(tri, mask.astype(BF16)) + base
    key = jnp.where(chosen, rank, UNROUTED)
    keyc_ref[...] = key
    keyt_ref[...] = key.T
    new_base = base + jnp.sum(mask, axis=0, keepdims=True)
    base_ref[...] = jnp.broadcast_to(new_base, base_ref.shape)
    ends_ref[0] = jnp.broadcast_to(new_base, ends_ref.shape[1:])


def _route(logits):
    n = logits.shape[0]
    tm = TOK_TILE
    ns = n // tm
    row = lambda s: (s, 0)
    return pl.pallas_call(
        _route_kernel,
        grid=(ns,),
        in_specs=[pl.BlockSpec((tm, LANES), row)],
        out_specs=[
            pl.BlockSpec((tm, LANES), row),
            pl.BlockSpec((LANES, tm), lambda s: (0, s)),
            pl.BlockSpec((LANES, tm), lambda s: (0, s)),
            pl.BlockSpec((1, 8, LANES), lambda s: (s, 0, 0)),
        ],
        out_shape=[
            jax.ShapeDtypeStruct((n, LANES), F32),
            jax.ShapeDtypeStruct((LANES, n), F32),
            jax.ShapeDtypeStruct((LANES, n), F32),
            jax.ShapeDtypeStruct((ns, 8, LANES), F32),
        ],
        scratch_shapes=[pltpu.VMEM((8, LANES), F32)],
        name="moe_route",
    )(logits)


def _work_lists(ends, n_tokens):
    e_n = N_EXPERTS
    sub = EXP_TM // DISP_TD
    max_tiles = (2 * n_tokens) // EXP_TM + e_n
    i32 = jnp.int32

    ends_i = ends[:, 0, :e_n].astype(i32)
    base = jnp.concatenate([jnp.zeros((1, e_n), i32), ends_i], axis=0)
    cnt = base[-1]
    ntile = (cnt + EXP_TM - 1) // EXP_TM
    tile_end = jnp.cumsum(ntile)
    tile_start = tile_end - ntile
    total_tiles = tile_end[-1]
    nunit = (cnt + EXP_UNIT - 1) // EXP_UNIT
    off = tile_start * EXP_TM

    e_ids = jnp.arange(e_n, dtype=i32)

    def expert_of(tile):
        e = jnp.sum((tile_end[None, :] <= tile[:, None]).astype(i32), axis=1)
        return e, (e[:, None] == e_ids[None, :]).astype(i32)

    def pick(onehot, per_expert):
        return jnp.sum(onehot * per_expert[None, :], axis=1)

    t = jnp.arange(max_tiles, dtype=i32)
    tc = jnp.minimum(t, total_tiles - 1)
    tile_e, oh_t = expert_of(tc)
    tile_pass = jnp.clip(pick(oh_t, nunit) - (tc - pick(oh_t, tile_start)) * (EXP_TM // EXP_UNIT),
                         1, EXP_TM // EXP_UNIT)
    tile_pass = jnp.where(t < total_tiles, tile_pass, 0)

    j = jnp.arange(max_tiles * sub, dtype=i32)
    jt = jnp.minimum(j // sub, total_tiles - 1)
    je, oh_j = expert_of(jt)
    jm = (jt - pick(oh_j, tile_start)) * sub + j % sub
    lo_rank = jm * DISP_TD
    hi_rank = jnp.minimum(lo_rank + DISP_TD, pick(oh_j, cnt))
    base_e = jnp.sum(oh_j[:, :, None] * base.T[None, :, :], axis=1)
    s_lo = jnp.sum((base_e[:, 1:] <= lo_rank[:, None]).astype(i32), axis=1)
    s_hi = jnp.sum((base_e[:, :-1] < hi_rank[:, None]).astype(i32), axis=1)
    n_src = jnp.where(j // sub < total_tiles, jnp.maximum(s_hi - s_lo, 0), 0)

    rows = max_tiles * EXP_TM
    start = off[None, :] + base[:-1]
    stop = off[None, :] + base[1:]
    wstart = jnp.minimum(start // COMB_ALIGN * COMB_ALIGN, rows - COMB_WIN)
    need = jnp.where(stop > start, stop - wstart, 0)
    rel = wstart - off[None, :]
    split = COMB_CHUNKS[0][1]
    wtail = jnp.where(need > split, (wstart + split) // COMB_ALIGN, 0)
    return dict(
        disp=(je, jm, s_lo, n_src),
        comb=((wstart // COMB_ALIGN).reshape(-1), wtail.reshape(-1), rel.reshape(-1),
              need.reshape(-1)),
        tiles=(tc, tile_e, tile_pass),
        max_tiles=max_tiles)


def _dispatch_kernel(e_ref, m_ref, slo_ref, n_ref, h_ref, keyt_ref, gatet_ref, o_ref, gs_ref,
                     acc_ref, gacc_ref):
    td = DISP_TD
    for sub in range(DISP_PER_STEP):
        _dispatch_tile(pl.program_id(0) * DISP_PER_STEP + sub, slice(sub * td, (sub + 1) * td),
                       e_ref, m_ref, slo_ref, n_ref, h_ref, keyt_ref, gatet_ref, o_ref, gs_ref,
                       acc_ref, gacc_ref)


def _dispatch_tile(j, rs, e_ref, m_ref, slo_ref, n_ref, h_ref, keyt_ref, gatet_ref, o_ref, gs_ref,
                   acc_ref, gacc_ref):
    td = DISP_TD
    e = e_ref[j]
    slot = (m_ref[j] * td + lax.broadcasted_iota(jnp.int32, (td, TOK_TILE), 0)).astype(F32)

    def item(i, assign):
        ts = pl.multiple_of((slo_ref[j] + i) * TOK_TILE, TOK_TILE)
        key = keyt_ref[pl.ds(e, 1), pl.ds(ts, TOK_TILE)]
        hit = key == slot
        onehot = jnp.where(hit, 1.0, 0.0).astype(BF16)
        rows = _dot(onehot, h_ref[pl.ds(ts, TOK_TILE), :])
        gate = gatet_ref[pl.ds(e, 1), pl.ds(ts, TOK_TILE)]
        gsum = jnp.sum(jnp.where(hit, gate, 0.0), axis=-1, keepdims=True)
        if assign:
            acc_ref[...] = rows
            gacc_ref[...] = gsum
        else:
            acc_ref[...] += rows
            gacc_ref[...] += gsum

    @pl.when(n_ref[j] == 0)
    def _():
        acc_ref[...] = jnp.zeros_like(acc_ref)
        gacc_ref[...] = jnp.zeros_like(gacc_ref)

    @pl.when(n_ref[j] > 0)
    def _():
        item(0, True)

    def body(i, carry):
        item(i, False)
        return carry

    lax.fori_loop(1, n_ref[j], body, 0)
    o_ref[rs, :] = acc_ref[...].astype(o_ref.dtype)
    gs_ref[rs, :] = gacc_ref[...]


def _dispatch(h2, key_t, gate_t, lists, max_tiles):
    n, d = h2.shape
    rows = max_tiles * EXP_TM
    resident = dict(pipeline_mode=pl.Buffered(1))
    grid_spec = pltpu.PrefetchScalarGridSpec(
        num_scalar_prefetch=4,
        grid=(rows // (DISP_TD * DISP_PER_STEP),),
        in_specs=[
            pl.BlockSpec((n, d), lambda j, *_: (0, 0), **resident),
            pl.BlockSpec((8, n), lambda j, *_: (0, 0), **resident),
            pl.BlockSpec((8, n), lambda j, *_: (0, 0), **resident),
        ],
        out_specs=[pl.BlockSpec((DISP_TD * DISP_PER_STEP, d), lambda j, *_: (j, 0)),
                   pl.BlockSpec((DISP_TD * DISP_PER_STEP, 1), lambda j, *_: (j, 0))],
        scratch_shapes=[pltpu.VMEM((DISP_TD, d), F32), pltpu.VMEM((DISP_TD, 1), F32)],
    )
    return pl.pallas_call(
        _dispatch_kernel,
        grid_spec=grid_spec,
        out_shape=[jax.ShapeDtypeStruct((rows, d), BF16),
                   jax.ShapeDtypeStruct((rows, 1), F32)],
        compiler_params=_BIG_VMEM,
        name="moe_dispatch",
    )(*lists, h2, key_t, gate_t)


def _expert_kernel(tidx_ref, te_ref, tp_ref, x_ref, gs_ref, wg_ref, wu_ref, wd_ref, y_ref,
                   acc_ref):
    i = pl.program_id(0)
    f = pl.program_id(1)
    nf = pl.num_programs(1)

    max_units = EXP_TM // EXP_UNIT
    n_steps = D_FF_EXPERT // EXP_TF

    def run(units, first, last):
        done = 0
        while done < units * EXP_UNIT:
            size = EXP_SUB if units * EXP_UNIT - done >= EXP_SUB else EXP_UNIT
            rs = slice(done, done + size)
            done += size
            x = x_ref[rs, :]
            g = _dot(x, wg_ref[0])
            u = _dot(x, wu_ref[0])
            act = (g * jax.nn.sigmoid(g) * u).astype(BF16)
            part = _dot(act, wd_ref[0])
            total = part if first else acc_ref[rs, :] + part
            if last:
                y_ref[rs, :] = (gs_ref[rs, :] * total).astype(y_ref.dtype)
            else:
                acc_ref[rs, :] = total
        if last and done < EXP_TM:
            y_ref[done:, :] = jnp.zeros((EXP_TM - done, y_ref.shape[1]), y_ref.dtype)

    kinds = [(f == 0, True, n_steps == 1), (f == nf - 1, n_steps == 1, True)]
    if n_steps > 2:
        kinds.append((jnp.logical_and(f > 0, f < nf - 1), False, False))
    for units in range(max_units + 1):
        for cond, first, last in kinds[:1] if n_steps == 1 else kinds:
            @pl.when(jnp.logical_and(tp_ref[i] == units, cond))
            def _(units=units, first=first, last=last):
                run(units, first, last)


def _experts(xs, gs, w_gu, w_down, tiles, max_tiles):
    rows, d = xs.shape
    tm, tf = EXP_TM, EXP_TF
    nf = D_FF_EXPERT // tf

    def fsel(i, f, tp):
        return jnp.where(tp[i] > 0, f, nf - 1)

    grid_spec = pltpu.PrefetchScalarGridSpec(
        num_scalar_prefetch=3,
        grid=(max_tiles, nf),
        in_specs=[
            pl.BlockSpec((tm, d), lambda i, f, ti, te, tv: (ti[i], 0)),
            pl.BlockSpec((tm, 1), lambda i, f, ti, te, tv: (ti[i], 0)),
            pl.BlockSpec((1, d, tf), lambda i, f, ti, te, tv: (te[i], 0, fsel(i, f, tv))),
            pl.BlockSpec((1, d, tf), lambda i, f, ti, te, tv: (te[i], 0, nf + fsel(i, f, tv))),
            pl.BlockSpec((1, tf, d), lambda i, f, ti, te, tv: (te[i], fsel(i, f, tv), 0)),
        ],
        out_specs=pl.BlockSpec((tm, d), lambda i, f, ti, te, tv: (i, 0)),
        scratch_shapes=[pltpu.VMEM((tm, d), F32)],
    )
    return pl.pallas_call(
        _expert_kernel,
        grid_spec=grid_spec,
        out_shape=jax.ShapeDtypeStruct((rows, d), BF16),
        compiler_params=_BIG_VMEM,
        name="moe_experts",
    )(*tiles, xs, gs, w_gu, w_gu, w_down)


def _combine_kernel(ws_ref, wt_ref, rel_ref, need_ref, *refs):
    head_refs = refs[:N_EXPERTS]
    tail_refs = refs[N_EXPERTS:2 * N_EXPERTS]
    keyc_ref, x_ref, mod_ref, gpost_ref, o_ref, acc_ref = refs[2 * N_EXPERTS:]
    s = pl.program_id(0)
    ts = acc_ref.shape[0]

    def term(e, y_ref, a):
        key = keyc_ref[:, e:e + 1]
        rel = rel_ref[s * N_EXPERTS + e]
        slot = (rel + a + lax.broadcasted_iota(jnp.int32, (ts, y_ref.shape[0]), 1)).astype(F32)
        onehot = jnp.where(key == slot, 1.0, 0.0).astype(BF16)
        return _dot(onehot, y_ref[...])

    acc = term(0, head_refs[0], 0)
    for e in range(1, N_EXPERTS):
        acc = acc + term(e, head_refs[e], 0)
    acc_ref[...] = acc
    split = COMB_CHUNKS[0][1]
    for e in range(N_EXPERTS):
        @pl.when(need_ref[s * N_EXPERTS + e] > split)
        def _(e=e):
            acc_ref[...] += term(e, tail_refs[e], split)
    gt_f = mod_ref[0, 5:6, :]
    o_ref[...] = x_ref[...] + (1.0 + gt_f) * _rms(acc_ref[...], gpost_ref[...])


def _combine(y, key_c, x2, mod_l, g_post, lists, seq):
    n, d = x2.shape
    per_b = seq // TOK_TILE
    tok = lambda s, *_: (s, 0)

    (a0, b0), (a1, b1) = COMB_CHUNKS

    def head(e):
        return pl.BlockSpec((pl.Element(b0 - a0), pl.Element(d)),
                            lambda s, ws, wt, rel, need: (ws[s * N_EXPERTS + e] * COMB_ALIGN, 0))

    def tail(e):
        return pl.BlockSpec((pl.Element(b1 - a1), pl.Element(d)),
                            lambda s, ws, wt, rel, need: (wt[s * N_EXPERTS + e] * COMB_ALIGN, 0))

    grid_spec = pltpu.PrefetchScalarGridSpec(
        num_scalar_prefetch=4,
        grid=(n // TOK_TILE,),
        in_specs=[head(e) for e in range(N_EXPERTS)] + [tail(e) for e in range(N_EXPERTS)] + [
            pl.BlockSpec((TOK_TILE, LANES), tok),
            pl.BlockSpec((TOK_TILE, d), tok),
            pl.BlockSpec((1, N_MOD, d), lambda s, *_: (s // per_b, 0, 0)),
            pl.BlockSpec((1, d), lambda s, *_: (0, 0)),
        ],
        out_specs=pl.BlockSpec((TOK_TILE, d), tok),
        scratch_shapes=[pltpu.VMEM((TOK_TILE, d), F32)],
    )
    return pl.pallas_call(
        _combine_kernel,
        grid_spec=grid_spec,
        out_shape=jax.ShapeDtypeStruct((n, d), F32),
        name="moe_combine",
    )(*lists, *([y] * (2 * N_EXPERTS)), key_c, x2, mod_l, g_post)


def kernel(x, c, w_ada, b_ada, g_pre_mix, g_post_mix, g_pre_ffn, g_post_ffn, w_in, b_forget, g_v,
           w_spatial, b_spatial, g_out_attn, g_out_sgu, w_out, w_gate_up_dense, w_down_dense,
           w_router, w_gate_up_exp, w_down_exp):
    bsz, seq, d = x.shape
    depth = w_ada.shape[0]
    n = bsz * seq
    npair = N_HEADS // 2
    assert d == D_MODEL and seq % PROJ_TILE == 0 and seq % TOK_TILE == 0 and seq % ATT_TQ == 0

    w_gu_exp3 = w_gate_up_exp.reshape(-1, N_EXPERTS * d, 2 * D_FF_EXPERT)
    w_down_exp3 = w_down_exp.reshape(-1, N_EXPERTS * D_FF_EXPERT, d)
    riders = {("w_in", 0): (w_in, 0), ("w_out", 0): (w_out, 0)}
    mod, *casts = _modulation(c, w_ada, b_ada, tuple(riders.values()))
    bf16 = dict(zip(riders.keys(), casts))
    mod = mod.reshape(depth, bsz, N_MOD, d)
    x2 = x.reshape(n, d)
    for l in range(depth):
        mod_l = mod[l]
        moe = (l % 2 == 1)
        q, k, v, zf_t, u, sv = _inproj(x2, mod_l, g_pre_mix[l].reshape(1, d), bf16[("w_in", l)],
                                       g_v[l].reshape(1, SGU_W), seq)
        riders = {}
        if l + 1 < depth:
            riders[("w_in", l + 1)] = (w_in, l + 1)
            riders[("w_out", l + 1)] = (w_out, l + 1)
        if moe:
            riders[("w_gu_exp", l // 2)] = (w_gu_exp3, l // 2)
        else:
            riders[("w_gu_dense", l // 2)] = (w_gate_up_dense, l // 2)
            riders[("w_down_dense", l // 2)] = (w_down_dense, l // 2)
            if l + 1 < depth:
                riders[("w_down_exp", l // 2)] = (w_down_exp3, l // 2)
        attn, *casts = _attention(q, k, v, zf_t.reshape(npair, 2, n),
                                  b_forget[l].reshape(npair, 2, 1), bsz, seq,
                                  tuple(riders.values()))
        bf16.update(zip(riders.keys(), casts))
        sgun = _sgu(u, sv, w_spatial[l].astype(BF16), b_spatial[l].T,
                    g_out_sgu[l].reshape(1, SGU_W))
        w_r = None
        if moe:
            w_r = jnp.pad(w_router[l // 2], ((0, 0), (0, LANES - N_EXPERTS)))
        res = _outproj(attn, sgun, g_out_attn[l].reshape(1, ATTN_W), bf16[("w_out", l)],
                       x2, mod_l, g_post_mix[l].reshape(1, d), g_pre_ffn[l].reshape(1, d), w_r, seq)
        g_post = g_post_ffn[l].reshape(1, d)
        if not moe:
            x2, h2 = res
            x2 = _ffn_dense(h2, bf16[("w_gu_dense", l // 2)], bf16[("w_down_dense", l // 2)],
                            x2, mod_l, g_post, seq)
        else:
            x2, h2, logits = res
            key_c, gate_t, key_t, ends = _route(logits)
            wl = _work_lists(ends, n)
            xs, gs = _dispatch(h2, key_t, gate_t, wl["disp"], wl["max_tiles"])
            w_gu = bf16[("w_gu_exp", l // 2)].reshape(N_EXPERTS, d, 2 * D_FF_EXPERT)
            w_dn = bf16[("w_down_exp", l // 2)].reshape(N_EXPERTS, D_FF_EXPERT, d)
            y = _experts(xs, gs, w_gu, w_dn, wl["tiles"], wl["max_tiles"])
            x2 = _combine(y, key_c, x2, mod_l, g_post, wl["comb"], seq)
    return x2.reshape(bsz, seq, d)
```
